```python
import math
import jax, jax.numpy as jnp
from jax import lax
import numpy as np

D_MODEL = 1024
BATCH = 8
SEQ = 2048
DEPTH = 2

HALF = D_MODEL // 2
HGRN_HEADS = 4
HGRN_DIM = HALF // HGRN_HEADS
HGRN_CHUNK = 64
FOX_HEADS = 8
FOX_DIM = HALF // FOX_HEADS
LRU_WIDTH = HALF
LRU_BLOCKS = 8
LRU_BLOCK_DIM = LRU_WIDTH // LRU_BLOCKS
LRU_C = 8.0
CONV_WIDTH = 4
SB_HEADS = 8
SB_DIM = HALF // SB_HEADS
Q_BLOCK = 128
D_FF = -(-8 * D_MODEL // (3 * 256)) * 256
N_EVEN = (DEPTH + 1) // 2
N_ODD = DEPTH // 2
ALPHA = (2 * DEPTH) ** 0.25
BETA = (8 * DEPTH) ** -0.25
EVEN_IN = 7 * HALF + FOX_HEADS
ODD_IN = 2 * LRU_WIDTH + 3 * HALF
EPS = 1e-5

kernel_name = 'hybrid_hgrn2_fox_rglru_stickbreaking'


def layer_norm(x, g, b):
    xf = x.astype(jnp.float32)
    mu = jnp.mean(xf, axis=-1, keepdims=True)
    xc = xf - mu
    var = jnp.mean(xc * xc, axis=-1, keepdims=True)
    y = xc * lax.rsqrt(var + EPS) * g.astype(jnp.float32) + b.astype(jnp.float32)
    return y.astype(x.dtype)


def rms_norm_f32(x, g):
    xf = x.astype(jnp.float32)
    return xf * lax.rsqrt(jnp.mean(xf * xf, axis=-1, keepdims=True) + EPS) * g.astype(jnp.float32)


def split_cols(t, sizes):
    return jnp.split(t, np.cumsum(sizes)[:-1].tolist(), axis=-1)


def hgrn2(q, f_logit, inp, g, lb, norm_g):
    B, S, _ = q.shape
    nc = S // HGRN_CHUNK
    z = f_logit.astype(jnp.float32)
    log_f = jnp.log(lb + (1.0 - lb) * jax.nn.sigmoid(z))
    k = (1.0 - lb) * jax.nn.sigmoid(-z)

    def to_chunks(t):
        return t.astype(jnp.float32).reshape(B, nc, HGRN_CHUNK, HGRN_HEADS, HGRN_DIM).transpose(1, 0, 3, 2, 4)

    qc, kc, vc, lc = to_chunks(q), to_chunks(k), to_chunks(inp), to_chunks(log_f)
    causal = jnp.tril(jnp.ones((HGRN_CHUNK, HGRN_CHUNK), dtype=bool))[:, :, None]

    def step(state, chunk):
        qt, kt, vt, lt = chunk
        b = jnp.cumsum(lt, axis=2)
        o_inter = jnp.einsum('bhtk,bhkv->bhtv', qt * jnp.exp(b), state)
        rel = b[:, :, :, None, :] - b[:, :, None, :, :]
        decay = jnp.exp(jnp.where(causal, rel, -jnp.inf))
        scores = jnp.einsum('bhtk,bhtsk,bhsk->bhts', qt, decay, kt)
        o = o_inter + jnp.einsum('bhts,bhsv->bhtv', scores, vt)
        b_last = b[:, :, -1:, :]
        state = jnp.exp(b_last[:, :, 0, :])[..., None] * state + jnp.einsum(
            'bhsk,bhsv->bhkv', kt * jnp.exp(b_last - b), vt)
        return state, o

    s0 = jnp.zeros((B, HGRN_HEADS, HGRN_DIM, HGRN_DIM), jnp.float32)
    _, o = lax.scan(step, s0, (qc, kc, vc, lc))
    o = o.transpose(1, 0, 3, 2, 4).reshape(B, S, HGRN_HEADS, HGRN_DIM)
    o = rms_norm_f32(o, norm_g)
    gate = jax.nn.silu(g.astype(jnp.float32)).reshape(B, S, HGRN_HEADS, HGRN_DIM)
    return (o * gate).reshape(B, S, HALF).astype(g.dtype)


def forgetting_attention(q, k, v, f_logit, b_f):
    B, S, _ = q.shape
    q = q.reshape(B, S, FOX_HEADS, FOX_DIM)
    k = k.reshape(B, S, FOX_HEADS, FOX_DIM)
    v = v.reshape(B, S, FOX_HEADS, FOX_DIM)
    log_f = jax.nn.log_sigmoid(f_logit.astype(jnp.float32) + b_f.astype(jnp.float32))
    c = jnp.cumsum(log_f, axis=1).transpose(0, 2, 1)
    scale = FOX_DIM ** -0.5
    outs = []
    for blk in range(S // Q_BLOCK):
        q0, q1 = blk * Q_BLOCK, (blk + 1) * Q_BLOCK
        logits = jnp.einsum('bqhd,bkhd->bhqk', q[:, q0:q1], k[:, :q1]).astype(jnp.float32) * scale
        logits = logits + c[:, :, q0:q1, None] - c[:, :, None, :q1]
        qpos = jnp.arange(q0, q1)[:, None]
        kpos = jnp.arange(q1)[None, :]
        logits = jnp.where(kpos <= qpos, logits, -jnp.inf)
        p = jax.nn.softmax(logits, axis=-1)
        outs.append(jnp.einsum('bhqk,bkhd->bqhd', p.astype(v.dtype), v[:, :q1]))
    return jnp.concatenate(outs, axis=1).reshape(B, S, HALF)


def stick_breaking_attention(q, k, v):
    B, S, _ = q.shape
    q = q.reshape(B, S, SB_HEADS, SB_DIM)
    k = k.reshape(B, S, SB_HEADS, SB_DIM)
    v = v.reshape(B, S, SB_HEADS, SB_DIM)
    scale = SB_DIM ** -0.5
    outs = []
    for blk in range(S // Q_BLOCK):
        q0, q1 = blk * Q_BLOCK, (blk + 1) * Q_BLOCK
        z = jnp.einsum('bqhd,bkhd->bhqk', q[:, q0:q1], k[:, :q1]).astype(jnp.float32) * scale
        qpos = jnp.arange(q0, q1)[:, None]
        kpos = jnp.arange(q1)[None, :]
        mask = kpos < qpos
        log_one_minus = jnp.where(mask, jax.nn.log_sigmoid(-z), 0.0)
        suffix = lax.cumsum(log_one_minus, axis=3, reverse=True) - log_one_minus
        log_w = jax.nn.log_sigmoid(z) + suffix
        w = jnp.where(mask, jnp.exp(log_w), 0.0)
        outs.append(jnp.einsum('bhqk,bkhd->bqhd', w.astype(v.dtype), v[:, :q1]))
    return jnp.concatenate(outs, axis=1).reshape(B, S, HALF)


def rg_lru_branch(xr, gate, conv_w, conv_b, wa, ba, wx, bx, lam):
    B, S, W = xr.shape
    xpad = jnp.pad(xr, ((0, 0), (CONV_WIDTH - 1, 0), (0, 0)))
    xc = conv_b + sum(conv_w[j] * xpad[:, j:j + S] for j in range(CONV_WIDTH))
    xb = xc.reshape(B, S, LRU_BLOCKS, LRU_BLOCK_DIM)
    r = jax.nn.sigmoid(jnp.einsum('bsni,nij->bsnj', xb, wa).reshape(B, S, W) + ba)
    i = jax.nn.sigmoid(jnp.einsum('bsni,nij->bsnj', xb, wx).reshape(B, S, W) + bx)
    log_a = -LRU_C * r.astype(jnp.float32) * jax.nn.softplus(-lam.astype(jnp.float32))
    a = jnp.exp(log_a)
    u = jnp.sqrt(-jnp.expm1(2.0 * log_a)) * (i * xc).astype(jnp.float32)

    def combine(left, right):
        a1, b1 = left
        a2, b2 = right
        return a1 * a2, a2 * b1 + b2

    _, h = lax.associative_scan(combine, (a, u), axis=1)
    return (h * jax.nn.gelu(gate.astype(jnp.float32), approximate=True)).astype(xr.dtype)


def swiglu(x, w13, w2):
    a, b = jnp.split(x @ w13, 2, axis=-1)
    return (jax.nn.silu(a) * b) @ w2


def setup_inputs(seed: int = 0) -> dict:
    key = jax.random.key(seed)
    ks = jax.random.split(key, 24)
    f32 = jnp.float32
    nrm = lambda k, shp, s: jax.random.normal(k, shp, f32) * s
    x = jax.random.normal(ks[0], (BATCH, SEQ, D_MODEL), f32)
    ev_w_in = nrm(ks[1], (N_EVEN, D_MODEL, EVEN_IN), D_MODEL ** -0.5)
    ev_fox_bf = 1.0 + nrm(ks[2], (N_EVEN, FOX_HEADS), 0.1)
    hgrn_lb = nrm(ks[3], (DEPTH + 1, HALF), 0.1)
    ev_hgrn_norm_g = 1.0 + nrm(ks[4], (N_EVEN, HGRN_DIM), 0.02)
    ev_w_out = nrm(ks[5], (N_EVEN, D_MODEL, D_MODEL), BETA * D_MODEL ** -0.5)
    od_w_in = nrm(ks[6], (N_ODD, D_MODEL, ODD_IN), D_MODEL ** -0.5)
    od_conv_w = nrm(ks[7], (N_ODD, CONV_WIDTH, LRU_WIDTH), CONV_WIDTH ** -0.5)
    od_conv_b = nrm(ks[8], (N_ODD, LRU_WIDTH), 0.02)
    od_gate_a_w = nrm(ks[9], (N_ODD, LRU_BLOCKS, LRU_BLOCK_DIM, LRU_BLOCK_DIM), LRU_BLOCK_DIM ** -0.5)
    od_gate_a_b = nrm(ks[10], (N_ODD, LRU_WIDTH), 0.02)
    od_gate_x_w = nrm(ks[11], (N_ODD, LRU_BLOCKS, LRU_BLOCK_DIM, LRU_BLOCK_DIM), LRU_BLOCK_DIM ** -0.5)
    od_gate_x_b = nrm(ks[12], (N_ODD, LRU_WIDTH), 0.02)
    a_c = jax.random.uniform(ks[13], (N_ODD, LRU_WIDTH), f32, 0.9, 0.999) ** (1.0 / LRU_C)
    od_lru_lambda = jnp.log(a_c) - jnp.log1p(-a_c)
    od_w_out = nrm(ks[14], (N_ODD, D_MODEL, D_MODEL), BETA * D_MODEL ** -0.5)
    ffn_w13 = nrm(ks[15], (DEPTH, D_MODEL, 2 * D_FF), D_MODEL ** -0.5)
    ffn_w2 = nrm(ks[16], (DEPTH, D_FF, D_MODEL), BETA * D_FF ** -0.5)
    ln_g = 1.0 + nrm(ks[17], (DEPTH, 2, D_MODEL), 0.02)
    ln_b = nrm(ks[18], (DEPTH, 2, D_MODEL), 0.02)
    return {'x': x, 'ev_w_in': ev_w_in, 'ev_fox_bf': ev_fox_bf, 'hgrn_lb': hgrn_lb,
            'ev_hgrn_norm_g': ev_hgrn_norm_g, 'ev_w_out': ev_w_out, 'od_w_in': od_w_in,
            'od_conv_w': od_conv_w, 'od_conv_b': od_conv_b, 'od_gate_a_w': od_gate_a_w,
            'od_gate_a_b': od_gate_a_b, 'od_gate_x_w': od_gate_x_w, 'od_gate_x_b': od_gate_x_b,
            'od_lru_lambda': od_lru_lambda, 'od_w_out': od_w_out, 'ffn_w13': ffn_w13,
            'ffn_w2': ffn_w2, 'ln_g': ln_g, 'ln_b': ln_b}


def reference(x, ev_w_in, ev_fox_bf, hgrn_lb, ev_hgrn_norm_g, ev_w_out, od_w_in, od_conv_w,
              od_conv_b, od_gate_a_w, od_gate_a_b, od_gate_x_w, od_gate_x_b, od_lru_lambda,
              od_w_out, ffn_w13, ffn_w2, ln_g, ln_b):
    lb_all = jnp.cumsum(jax.nn.softmax(hgrn_lb.astype(jnp.float32), axis=0), axis=0)
    for layer in range(DEPTH):
        if layer % 2 == 0:
            e = layer // 2
            proj = x @ ev_w_in[e]
            qa, fa, ia, ga, qb, kb, vb, fb = split_cols(proj, [HALF] * 7 + [FOX_HEADS])
            ya = hgrn2(qa, fa, ia, ga, lb_all[layer], ev_hgrn_norm_g[e])
            yb = forgetting_attention(qb, kb, vb, fb, ev_fox_bf[e])
            mix = jnp.concatenate([ya, yb], axis=-1) @ ev_w_out[e]
        else:
            o = layer // 2
            proj = x @ od_w_in[o]
            xr, gr, qd, kd, vd = split_cols(proj, [LRU_WIDTH, LRU_WIDTH, HALF, HALF, HALF])
            yc = rg_lru_branch(xr, gr, od_conv_w[o], od_conv_b[o], od_gate_a_w[o], od_gate_a_b[o],
                               od_gate_x_w[o], od_gate_x_b[o], od_lru_lambda[o])
            yd = stick_breaking_attention(qd, kd, vd)
            mix = jnp.concatenate([yc, yd], axis=-1) @ od_w_out[o]
        x = layer_norm(ALPHA * x + mix, ln_g[layer, 0], ln_b[layer, 0])
        x = layer_norm(ALPHA * x + swiglu(x, ffn_w13[layer], ffn_w2[layer]), ln_g[layer, 1], ln_b[layer, 1])
    return x
```

```python
import functools

import jax
import jax.numpy as jnp
from jax import lax
from jax.experimental import pallas as pl
from jax.experimental.pallas import tpu as pltpu

F32 = jnp.float32
BF16 = jnp.bfloat16

SUBLANES = 8
LANES = 128

D_MODEL = 1024
HALF = D_MODEL // 2
HEAD_DIM = 64
HGRN_DIM = 128
LRU_BLOCKS = 8
LRU_C = 8.0
CONV_WIDTH = 4
DEPTH = 2
ALPHA = (2 * DEPTH) ** 0.25
EPS = 1e-5
D_FF = 2816

HGRN_CHUNK = 128
ATT_BLOCK = 128
VMEM_LIMIT = 56 * 1024 * 1024


def _cparams(sem):
    return pltpu.CompilerParams(dimension_semantics=sem, vmem_limit_bytes=VMEM_LIMIT)


def _proj_kernel(x_ref, wa_ref, wb_ref, oa_ref, ob_ref):
    xb = x_ref[...].astype(BF16)
    oa_ref[...] = jnp.dot(xb, wa_ref[...], preferred_element_type=F32)
    ob_ref[...] = jnp.dot(xb, wb_ref[...], preferred_element_type=F32).astype(BF16)


def _proj(x, wa, wb, tm=512):
    t, d = x.shape
    na, nb = wa.shape[1], wb.shape[1]
    return pl.pallas_call(
        _proj_kernel,
        grid=(t // tm,),
        in_specs=[
            pl.BlockSpec((tm, d), lambda i: (i, 0)),
            pl.BlockSpec((d, na), lambda i: (0, 0)),
            pl.BlockSpec((d, nb), lambda i: (0, 0)),
        ],
        out_specs=[
            pl.BlockSpec((tm, na), lambda i: (i, 0)),
            pl.BlockSpec((tm, nb), lambda i: (i, 0)),
        ],
        out_shape=[jax.ShapeDtypeStruct((t, na), F32), jax.ShapeDtypeStruct((t, nb), BF16)],
        compiler_params=_cparams(("parallel",)),
        name="in_proj",
    )(x, wa, wb)


def _layer_norm_rows(y, g, b):
    mu = jnp.mean(y, axis=-1, keepdims=True)
    yc = y - mu
    var = jnp.mean(yc * yc, axis=-1, keepdims=True)
    return yc * lax.rsqrt(var + EPS) * g + b


def _outproj_kernel(ya_ref, yb_ref, x_ref, w_ref, g_ref, b_ref, o_ref):
    acc = jnp.dot(ya_ref[...], w_ref[0:HALF, :], preferred_element_type=F32)
    acc = acc + jnp.dot(yb_ref[...], w_ref[HALF:D_MODEL, :], preferred_element_type=F32)
    y = ALPHA * x_ref[...] + acc
    o_ref[...] = _layer_norm_rows(y, g_ref[...], b_ref[...])


def _outproj(ya, yb, x, w, g, b, tm=512):
    t = x.shape[0]
    return pl.pallas_call(
        _outproj_kernel,
        grid=(t // tm,),
        in_specs=[
            pl.BlockSpec((tm, HALF), lambda i: (i, 0)),
            pl.BlockSpec((tm, HALF), lambda i: (i, 0)),
            pl.BlockSpec((tm, D_MODEL), lambda i: (i, 0)),
            pl.BlockSpec((D_MODEL, D_MODEL), lambda i: (0, 0)),
            pl.BlockSpec((1, D_MODEL), lambda i: (0, 0)),
            pl.BlockSpec((1, D_MODEL), lambda i: (0, 0)),
        ],
        out_specs=pl.BlockSpec((tm, D_MODEL), lambda i: (i, 0)),
        out_shape=jax.ShapeDtypeStruct((t, D_MODEL), F32),
        compiler_params=_cparams(("parallel",)),
        name="out_proj_ln",
    )(ya, yb, x, w, g, b)


def _ffn_kernel(x_ref, wa_ref, wb_ref, w2_ref, g_ref, b_ref, o_ref, xb_ref, acc_ref):
    f = pl.program_id(1)

    @pl.when(f == 0)
    def _():
        xb_ref[...] = x_ref[...].astype(BF16)

    xb = xb_ref[...]
    a = jnp.dot(xb, wa_ref[...], preferred_element_type=F32)
    b = jnp.dot(xb, wb_ref[...], preferred_element_type=F32)
    h = (a * jax.nn.sigmoid(a) * b).astype(BF16)
    part = jnp.dot(h, w2_ref[...], preferred_element_type=F32)

    @pl.when(f == 0)
    def _():
        acc_ref[...] = part

    @pl.when(f > 0)
    def _():
        acc_ref[...] += part

    @pl.when(f == pl.num_programs(1) - 1)
    def _():
        y = ALPHA * x_ref[...] + acc_ref[...]
        o_ref[...] = _layer_norm_rows(y, g_ref[...], b_ref[...])


def _ffn(x, w13, w2, g, b, tm=1024, tf=256):
    t = x.shape[0]
    nf = D_FF // tf
    return pl.pallas_call(
        _ffn_kernel,
        grid=(t // tm, nf),
        in_specs=[
            pl.BlockSpec((tm, D_MODEL), lambda i, f: (i, 0)),
            pl.BlockSpec((D_MODEL, tf), lambda i, f: (0, f)),
            pl.BlockSpec((D_MODEL, tf), lambda i, f: (0, f + nf)),
            pl.BlockSpec((tf, D_MODEL), lambda i, f: (f, 0)),
            pl.BlockSpec((1, D_MODEL), lambda i, f: (0, 0)),
            pl.BlockSpec((1, D_MODEL), lambda i, f: (0, 0)),
        ],
        out_specs=pl.BlockSpec((tm, D_MODEL), lambda i, f: (i, 0)),
        out_shape=jax.ShapeDtypeStruct((t, D_MODEL), F32),
        scratch_shapes=[pltpu.VMEM((tm, D_MODEL), BF16), pltpu.VMEM((tm, D_MODEL), F32)],
        compiler_params=_cparams(("parallel", "arbitrary")),
        name="ffn_ln",
    )(x, w13, w13, w2, g, b)


def _tile_rows(n):
    return lax.broadcasted_iota(jnp.int32, (SUBLANES, n), 0)


def _bcast_row(tile, r):
    return jnp.broadcast_to(tile[r:r + 1, :], tile.shape)


def _segmented_scans(x):
    n = len(x)
    row = _tile_rows(x[0].shape[1])
    zero = jnp.zeros_like(x[0])
    r3 = row & 3
    up4 = (row & 4) != 0

    p = {1: list(x)}
    p[2] = [t + jnp.where((row & 1) == 1, pltpu.roll(t, 1, 0), 0.0) for t in x]
    p[4] = [t + jnp.where(r3 == 2, pltpu.roll(t, 1, 0),
                          jnp.where(r3 == 3, pltpu.roll(t, 2, 0), 0.0)) for t in p[2]]
    p[8] = [t + jnp.where(up4, _bcast_row(t, 3), 0.0) for t in p[4]]

    e = {1: [zero] * n}
    e[2] = [jnp.where((row & 1) == 0, pltpu.roll(t, SUBLANES - 1, 0), 0.0) for t in x]
    r2 = [a + b for a, b in zip(e[2], x)]
    e[4] = [t + jnp.where(r3 == 1, pltpu.roll(r, SUBLANES - 1, 0),
                          jnp.where(r3 == 0, pltpu.roll(r, SUBLANES - 2, 0), 0.0))
            for t, r in zip(e[2], r2)]
    r4 = [a + b for a, b in zip(e[4], x)]
    e[8] = [t + jnp.where(up4, 0.0, _bcast_row(r, 4)) for t, r in zip(e[4], r4)]

    m = SUBLANES
    while m < SUBLANES * n:
        nt = m // SUBLANES
        pn, en = [], []
        for g in range(n // (2 * nt)):
            lo = slice(2 * nt * g, 2 * nt * g + nt)
            hi = slice(2 * nt * g + nt, 2 * nt * (g + 1))
            tot = _bcast_row(p[m][lo][-1], SUBLANES - 1)
            pn += p[m][lo] + [t + tot for t in p[m][hi]]
            first = e[m][hi][0][0:1, :] + x[hi][0][0:1, :]
            tot_hi = jnp.broadcast_to(first, zero.shape)
            en += [t + tot_hi for t in e[m][lo]] + e[m][hi]
        p[2 * m], e[2 * m] = pn, en
        m *= 2
    return p, e


def _level_map(c):
    t = lax.broadcasted_iota(jnp.int32, (c, c), 0)
    s = lax.broadcasted_iota(jnp.int32, (c, c), 1)
    x = t ^ s
    lv = jnp.full((c, c), -1, jnp.int32)
    m = 1
    while m < c:
        lv = lv + (x >= m).astype(jnp.int32)
        m *= 2
    return jnp.where(t < s, -2, lv)


def _hgrn_kernel(q_ref, f_ref, g_ref, v_ref, lb_ref, ng_ref, o_ref, state_ref):
    c = HGRN_CHUNK
    n_tiles = c // SUBLANES
    n_chunks = q_ref.shape[0] // c
    lb = lb_ref[...]
    one_m_lb = 1.0 - lb
    ng = ng_ref[...]
    lv = _level_map(c)
    state_ref[...] = jnp.zeros_like(state_ref)

    def chunk(ci, _):
        r0 = pl.multiple_of(ci * c, c)
        z = f_ref[pl.ds(r0, c), :]
        q = q_ref[pl.ds(r0, c), :]
        v = v_ref[pl.ds(r0, c), :]
        ez = jnp.exp(-jnp.abs(z))
        rz = 1.0 / (1.0 + ez)
        pos = z >= 0.0
        sig = jnp.where(pos, rz, ez * rz)
        nsig = jnp.where(pos, ez * rz, rz)
        lf = jnp.log(lb + one_m_lb * sig)
        kk = one_m_lb * nsig

        lf_tiles = [lf[SUBLANES * i:SUBLANES * (i + 1), :] for i in range(n_tiles)]
        p, e = _segmented_scans(lf_tiles)

        def cat(ts):
            return jnp.concatenate(ts, axis=0)

        nt_dims = (((1,), (1,)), ((), ()))
        scores = jnp.where(
            lv == -1,
            lax.dot_general(q.astype(BF16), kk.astype(BF16), nt_dims, preferred_element_type=F32),
            0.0)
        m, idx = 1, 0
        while m < c:
            qm = (q * jnp.exp(cat(p[m]))).astype(BF16)
            km = kk if m == 1 else kk * jnp.exp(cat(e[m]))
            sm = lax.dot_general(qm, km.astype(BF16), nt_dims, preferred_element_type=F32)
            scores = jnp.where(lv == idx, sm, scores)
            m *= 2
            idx += 1

        b = cat(p[c])
        qc = (q * jnp.exp(b)).astype(BF16)
        kc = (kk * jnp.exp(cat(e[c]))).astype(BF16)
        st = state_ref[...]
        o = jnp.dot(qc, st.astype(BF16), preferred_element_type=F32)
        o = o + jnp.dot(scores.astype(BF16), v, preferred_element_type=F32)

        decay = jnp.exp(b[c - 1:c, :])
        kv = lax.dot_general(kc, v, (((0,), (0,)), ((), ())), preferred_element_type=F32)
        state_ref[...] = st * jnp.broadcast_to(decay, st.shape).T + kv

        ms = jnp.mean(o * o, axis=-1, keepdims=True)
        on = o * lax.rsqrt(ms + EPS) * ng
        gg = g_ref[pl.ds(r0, c), :]
        o_ref[pl.ds(r0, c), :] = (on * (gg * jax.nn.sigmoid(gg))).astype(o_ref.dtype)
        return 0

    lax.fori_loop(0, n_chunks, chunk, 0)


def _hgrn(pa, pb, lb, ng, batch, seq):
    t = pa.shape[0]
    nh = HALF // HGRN_DIM
    return pl.pallas_call(
        _hgrn_kernel,
        grid=(batch, nh),
        in_specs=[
            pl.BlockSpec((seq, HGRN_DIM), lambda b, h: (b, h)),
            pl.BlockSpec((seq, HGRN_DIM), lambda b, h: (b, nh + h)),
            pl.BlockSpec((seq, HGRN_DIM), lambda b, h: (b, 2 * nh + h)),
            pl.BlockSpec((seq, HGRN_DIM), lambda b, h: (b, h)),
            pl.BlockSpec((1, HGRN_DIM), lambda b, h: (0, h)),
            pl.BlockSpec((1, HGRN_DIM), lambda b, h: (0, 0)),
        ],
        out_specs=pl.BlockSpec((seq, HGRN_DIM), lambda b, h: (b, h)),
        out_shape=jax.ShapeDtypeStruct((t, HALF), BF16),
        scratch_shapes=[pltpu.VMEM((HGRN_DIM, HGRN_DIM), F32)],
        compiler_params=_cparams(("parallel", "parallel")),
        name="hgrn2",
    )(pa, pa, pa, pb, lb, ng)


def _fox_gate_kernel(f_ref, bf_ref, c_ref):
    s = f_ref.shape[0]
    ft = f_ref[...].T[0:SUBLANES, :]
    x = ft + bf_ref[...][:, 0:1]
    lf = jnp.minimum(x, 0.0) - jnp.log1p(jnp.exp(-jnp.abs(x)))
    lane = lax.broadcasted_iota(jnp.int32, lf.shape, 1)
    d = 1
    while d < s:
        lf = lf + jnp.where(lane >= d, pltpu.roll(lf, d, 1), 0.0)
        d *= 2
    c_ref[0] = lf


def _fox_gate(pa, col_block, bf_pad, batch, seq):
    return pl.pallas_call(
        _fox_gate_kernel,
        grid=(batch,),
        in_specs=[
            pl.BlockSpec((seq, LANES), lambda b: (b, col_block)),
            pl.BlockSpec((SUBLANES, LANES), lambda b: (0, 0)),
        ],
        out_specs=pl.BlockSpec((1, SUBLANES, seq), lambda b: (b, 0, 0)),
        out_shape=jax.ShapeDtypeStruct((batch, SUBLANES, seq), F32),
        compiler_params=_cparams(("parallel",)),
        name="fox_gate",
    )(pa, bf_pad)


def _head_lane_masks(shape):
    lane = lax.broadcasted_iota(jnp.int32, shape, len(shape) - 1)
    first = lane < HEAD_DIM
    return first, jnp.logical_not(first)


def _fox_kernel(q_ref, k_ref, v_ref, c_ref, o_ref):
    i = pl.program_id(2)
    tq = q_ref.shape[0]
    tk = ATT_BLOCK
    scale = HEAD_DIM ** -0.5
    q = q_ref[...] * jnp.asarray(scale, BF16)
    h0, h1 = _head_lane_masks(q.shape)
    zq = jnp.zeros_like(q)
    qh = (jnp.where(h0, q, zq), jnp.where(h1, q, zq))
    nt_dims = (((1,), (1,)), ((), ()))
    v0m, v1m = _head_lane_masks((tk, LANES))
    o0m, _ = _head_lane_masks((tq, LANES))

    def block(j, carry, diag):
        m, l, acc = carry
        k0 = pl.multiple_of(j * tk, tk)
        k = k_ref[pl.ds(k0, tk), :]
        v = v_ref[pl.ds(k0, tk), :]
        zv = jnp.zeros_like(v)
        vh = (jnp.where(v0m, v, zv), jnp.where(v1m, v, zv))
        cs = c_ref[0, 0, :, pl.ds(k0, tk)]
        new_m, new_l, alphas, pvs = [], [], [], []
        for h in range(2):
            s = lax.dot_general(qh[h], k, nt_dims, preferred_element_type=F32)
            s = s - cs[h:h + 1, :]
            if diag:
                qpos = lax.broadcasted_iota(jnp.int32, s.shape, 0)
                kpos = lax.broadcasted_iota(jnp.int32, s.shape, 1)
                s = jnp.where(kpos <= qpos, s, -jnp.inf)
            mh = jnp.maximum(m[h], jnp.max(s, axis=-1, keepdims=True))
            alpha = jnp.exp(m[h] - mh)
            p = jnp.exp(s - mh)
            new_l.append(alpha * l[h] + jnp.sum(p, axis=-1, keepdims=True))
            new_m.append(mh)
            alphas.append(alpha)
            pvs.append(jnp.dot(p.astype(BF16), vh[h], preferred_element_type=F32))
        acc = acc * jnp.where(o0m, alphas[0], alphas[1]) + pvs[0] + pvs[1]
        return tuple(new_m), tuple(new_l), acc

    neg = jnp.full((tq, 1), -jnp.inf, F32)
    zero = jnp.zeros((tq, 1), F32)
    init = ((neg, neg), (zero, zero), jnp.zeros((tq, LANES), F32))
    carry = lax.fori_loop(0, i, lambda j, cr: block(j, cr, False), init)
    _, l, acc = block(i, carry, True)
    o_ref[...] = (acc / jnp.where(o0m, l[0], l[1])).astype(o_ref.dtype)


def _fox(pb, col0, cpair, batch, seq):
    t = pb.shape[0]
    npair = HALF // LANES
    nq = seq // ATT_BLOCK
    return pl.pallas_call(
        _fox_kernel,
        grid=(batch, npair, nq),
        in_specs=[
            pl.BlockSpec((ATT_BLOCK, LANES), lambda b, p, i: (b * nq + i, col0 + p)),
            pl.BlockSpec((seq, LANES), lambda b, p, i: (b, col0 + npair + p)),
            pl.BlockSpec((seq, LANES), lambda b, p, i: (b, col0 + 2 * npair + p)),
            pl.BlockSpec((1, 1, 2, seq), lambda b, p, i: (b, p, 0, 0)),
        ],
        out_specs=pl.BlockSpec((ATT_BLOCK, LANES), lambda b, p, i: (b * nq + i, p)),
        out_shape=jax.ShapeDtypeStruct((t, HALF), BF16),
        compiler_params=_cparams(("parallel", "parallel", "arbitrary")),
        name="fox_attention",
    )(pb, pb, pb, cpair)


def _suffix_matrix(tk):
    r = lax.broadcasted_iota(jnp.int32, (2 * tk, 2 * tk), 0) % tk
    cidx = lax.broadcasted_iota(jnp.int32, (2 * tk, 2 * tk), 1)
    keep = jnp.logical_or(cidx >= tk, r > cidx)
    return jnp.where(keep, 1.0, 0.0).astype(BF16)


def _sb_kernel(q_ref, k_ref, v_ref, o_ref):
    i = pl.program_id(2)
    tq = q_ref.shape[0]
    tk = ATT_BLOCK
    scale = HEAD_DIM ** -0.5
    q = q_ref[...] * jnp.asarray(scale, BF16)
    h0, h1 = _head_lane_masks(q.shape)
    zq = jnp.zeros_like(q)
    qh = (jnp.where(h0, q, zq), jnp.where(h1, q, zq))
    nt_dims = (((1,), (1,)), ((), ()))
    v0m, v1m = _head_lane_masks((tk, LANES))
    umat = _suffix_matrix(tk)

    def block(j, carry, diag):
        run, acc = carry
        k0 = pl.multiple_of(j * tk, tk)
        k = k_ref[pl.ds(k0, tk), :]
        v = v_ref[pl.ds(k0, tk), :]
        zv = jnp.zeros_like(v)
        vh = (jnp.where(v0m, v, zv), jnp.where(v1m, v, zv))
        new_run = []
        for h in range(2):
            z = lax.dot_general(qh[h], k, nt_dims, preferred_element_type=F32)
            sp = jnp.maximum(z, 0.0) + jnp.log1p(jnp.exp(-jnp.abs(z)))
            lom = -sp
            if diag:
                qpos = lax.broadcasted_iota(jnp.int32, z.shape, 0)
                kpos = lax.broadcasted_iota(jnp.int32, z.shape, 1)
                mask = kpos < qpos
                lom = jnp.where(mask, lom, 0.0)
            hi = lom.astype(BF16)
            lo = (lom - hi.astype(F32)).astype(BF16)
            r = jnp.dot(jnp.concatenate([hi, lo], axis=1), umat, preferred_element_type=F32)
            log_w = (z - sp) + r[:, 0:tk] + run[h]
            w = jnp.exp(log_w)
            if diag:
                w = jnp.where(mask, w, 0.0)
            acc = acc + jnp.dot(w.astype(BF16), vh[h], preferred_element_type=F32)
            new_run.append(run[h] + r[:, tk:2 * tk])
        return tuple(new_run), acc

    zero = jnp.zeros((tq, tk), F32)
    carry = block(i, ((zero, zero), jnp.zeros((tq, LANES), F32)), True)
    _, acc = lax.fori_loop(0, i, lambda n, cr: block(i - 1 - n, cr, False), carry)
    o_ref[...] = acc.astype(o_ref.dtype)


def _sb(pb, col0, batch, seq):
    t = pb.shape[0]
    npair = HALF // LANES
    nq = seq // ATT_BLOCK
    return pl.pallas_call(
        _sb_kernel,
        grid=(batch, npair, nq),
        in_specs=[
            pl.BlockSpec((ATT_BLOCK, LANES), lambda b, p, i: (b * nq + i, col0 + p)),
            pl.BlockSpec((seq, LANES), lambda b, p, i: (b, col0 + npair + p)),
            pl.BlockSpec((seq, LANES), lambda b, p, i: (b, col0 + 2 * npair + p)),
        ],
        out_specs=pl.BlockSpec((ATT_BLOCK, LANES), lambda b, p, i: (b * nq + i, p)),
        out_shape=jax.ShapeDtypeStruct((t, HALF), BF16),
        compiler_params=_cparams(("parallel", "parallel", "arbitrary")),
        name="sb_attention",
    )(pb, pb, pb)


def _lru_kernel(x_ref, gate_ref, cw_ref, cb_ref, wg_ref, bg_ref, lam_ref, o_ref,
                xe_ref, a_ref, u_ref, h_ref):
    sblk = pl.program_id(1)
    ts, w = x_ref.shape
    pad = SUBLANES

    @pl.when(sblk == 0)
    def _():
        xe_ref[0:pad, :] = jnp.zeros((pad, w), F32)
        h_ref[...] = jnp.zeros_like(h_ref)

    @pl.when(sblk > 0)
    def _():
        xe_ref[0:pad, :] = xe_ref[ts:ts + pad, :]

    x = x_ref[...]
    xe_ref[pad:pad + ts, :] = x
    xc = cb_ref[...] + cw_ref[CONV_WIDTH - 1:CONV_WIDTH, :] * x
    for d in range(1, CONV_WIDTH):
        xc = xc + cw_ref[CONV_WIDTH - 1 - d:CONV_WIDTH - d, :] * xe_ref[pad - d:pad - d + ts, :]

    gates = jnp.dot(xc.astype(BF16), wg_ref[...], preferred_element_type=F32) + bg_ref[...]
    r = jax.nn.sigmoid(gates[:, 0:w])
    ig = jax.nn.sigmoid(gates[:, w:2 * w])
    lam = lam_ref[...]
    softplus_neg_lam = jnp.maximum(-lam, 0.0) + jnp.log1p(jnp.exp(-jnp.abs(lam)))
    log_a = -LRU_C * r * softplus_neg_lam
    a_ref[...] = jnp.exp(log_a)
    th = jnp.tanh(-log_a)
    u_ref[...] = jnp.sqrt(2.0 * th / (1.0 + th)) * (ig * xc)

    row = _tile_rows(w)

    def tile(ti, h):
        r0 = pl.multiple_of(ti * SUBLANES, SUBLANES)
        a = a_ref[pl.ds(r0, SUBLANES), :]
        u = u_ref[pl.ds(r0, SUBLANES), :]
        for d in (1, 2, 4):
            sel = row >= d
            a_s = jnp.where(sel, pltpu.roll(a, d, 0), 1.0)
            u_s = jnp.where(sel, pltpu.roll(u, d, 0), 0.0)
            u = a * u_s + u
            a = a * a_s
        hh = a * h + u
        gt = gate_ref[pl.ds(r0, SUBLANES), :]
        o_ref[pl.ds(r0, SUBLANES), :] = (hh * jax.nn.gelu(gt, approximate=True)).astype(o_ref.dtype)
        return hh[SUBLANES - 1:SUBLANES, :]

    h_ref[...] = lax.fori_loop(0, ts // SUBLANES, tile, h_ref[...])


def _lru(pa, cw, cb, wg, bg, lam, batch, seq, ts=256):
    t = pa.shape[0]
    w = HALF
    ns = seq // ts
    return pl.pallas_call(
        _lru_kernel,
        grid=(batch, ns),
        in_specs=[
            pl.BlockSpec((ts, w), lambda b, s: (b * ns + s, 0)),
            pl.BlockSpec((ts, w), lambda b, s: (b * ns + s, 1)),
            pl.BlockSpec((CONV_WIDTH, w), lambda b, s: (0, 0)),
            pl.BlockSpec((1, w), lambda b, s: (0, 0)),
            pl.BlockSpec((w, 2 * w), lambda b, s: (0, 0)),
            pl.BlockSpec((1, 2 * w), lambda b, s: (0, 0)),
            pl.BlockSpec((1, w), lambda b, s: (0, 0)),
        ],
        out_specs=pl.BlockSpec((ts, w), lambda b, s: (b * ns + s, 0)),
        out_shape=jax.ShapeDtypeStruct((t, w), BF16),
        scratch_shapes=[
            pltpu.VMEM((ts + SUBLANES, w), F32),
            pltpu.VMEM((ts, w), F32),
            pltpu.VMEM((ts, w), F32),
            pltpu.VMEM((1, w), F32),
        ],
        compiler_params=_cparams(("parallel", "arbitrary")),
        name="rg_lru",
    )(pa, pa, cw, cb, wg, bg, lam)


def _block_diag(wb):
    n, bd, _ = wb.shape
    eye = jnp.eye(n, dtype=wb.dtype)
    return (eye[:, None, :, None] * wb[:, :, None, :]).reshape(n * bd, n * bd)


def kernel(x, ev_w_in, ev_fox_bf, hgrn_lb, ev_hgrn_norm_g, ev_w_out, od_w_in, od_conv_w, od_conv_b,
           od_gate_a_w, od_gate_a_b, od_gate_x_w, od_gate_x_b, od_lru_lambda, od_w_out, ffn_w13,
           ffn_w2, ln_g, ln_b):
    batch, seq, d = x.shape
    t = batch * seq
    xf = x.reshape(t, d)
    lb_all = jnp.cumsum(jax.nn.softmax(hgrn_lb.astype(F32), axis=0), axis=0)
    h = HALF

    w = ev_w_in[0]
    nfox = ev_fox_bf.shape[1]
    w_fb = jnp.pad(w[:, 7 * h:7 * h + nfox], ((0, 0), (0, LANES - nfox)))
    wa = jnp.concatenate([w[:, 0:2 * h], w[:, 3 * h:4 * h], w_fb], axis=1).astype(BF16)
    wb = jnp.concatenate([w[:, 2 * h:3 * h], w[:, 4 * h:7 * h]], axis=1).astype(BF16)
    pa, pb = _proj(xf, wa, wb)
    ya = _hgrn(pa, pb, lb_all[0].reshape(1, h), ev_hgrn_norm_g[0].reshape(1, HGRN_DIM), batch, seq)
    bf_pad = jnp.broadcast_to(ev_fox_bf[0].astype(F32).reshape(nfox, 1), (nfox, LANES))
    cum = _fox_gate(pa, 3 * h // LANES, bf_pad, batch, seq)
    yb = _fox(pb, h // LANES, cum.reshape(batch, nfox // 2, 2, seq), batch, seq)
    x1 = _outproj(ya, yb, xf, ev_w_out[0].astype(BF16), ln_g[0, 0].reshape(1, d), ln_b[0, 0].reshape(1, d))
    x2 = _ffn(x1, ffn_w13[0].astype(BF16), ffn_w2[0].astype(BF16),
              ln_g[0, 1].reshape(1, d), ln_b[0, 1].reshape(1, d))

    w = od_w_in[0]
    wa = w[:, 0:2 * h].astype(BF16)
    wb = w[:, 2 * h:5 * h].astype(BF16)
    pa, pb = _proj(x2, wa, wb)
    wg = jnp.concatenate([_block_diag(od_gate_a_w[0]), _block_diag(od_gate_x_w[0])], axis=1).astype(BF16)
    bg = jnp.concatenate([od_gate_a_b[0], od_gate_x_b[0]]).reshape(1, 2 * h).astype(F32)
    yc = _lru(pa, od_conv_w[0], od_conv_b[0].reshape(1, h), wg, bg, od_lru_lambda[0].reshape(1, h),
              batch, seq)
    yd = _sb(pb, 0, batch, seq)
    x3 = _outproj(yc, yd, x2, od_w_out[0].astype(BF16), ln_g[1, 0].reshape(1, d), ln_b[1, 0].reshape(1, d))
    x4 = _ffn(x3, ffn_w13[1].astype(BF16), ffn_w2[1].astype(BF16),
              ln_g[1, 1].reshape(1, d), ln_b[1, 1].reshape(1, d))
    return x4.reshape(batch, seq, d)
```

```python
import functools

import jax
import jax.numpy as jnp
from jax import lax
from jax.experimental import pallas as pl
from jax.experimental.pallas import tpu as pltpu

F32 = jnp.float32
BF16 = jnp.bfloat16

SUBLANES = 8
LANES = 128

D_MODEL = 1024
HALF = D_MODEL // 2
HEAD_DIM = 64
HGRN_DIM = 128
LRU_BLOCKS = 8
LRU_C = 8.0
CONV_WIDTH = 4
DEPTH = 2
ALPHA = (2 * DEPTH) ** 0.25
EPS = 1e-5
D_FF = 2816

HGRN_CHUNK = 128
ATT_BLOCK = 256
ATT_UNROLL = 4
VMEM_LIMIT = 56 * 1024 * 1024


def _cparams(sem):
    return pltpu.CompilerParams(dimension_semantics=sem, vmem_limit_bytes=VMEM_LIMIT)


def _proj_kernel(x_ref, wa_ref, wb_ref, oa_ref, ob_ref):
    xb = x_ref[...].astype(BF16)
    oa_ref[...] = jnp.dot(xb, wa_ref[...], preferred_element_type=F32)
    ob_ref[...] = jnp.dot(xb, wb_ref[...], preferred_element_type=F32).astype(BF16)


def _proj(x, wa, wb, tm=512):
    t, d = x.shape
    na, nb = wa.shape[1], wb.shape[1]
    return pl.pallas_call(
        _proj_kernel,
        grid=(t // tm,),
        in_specs=[
            pl.BlockSpec((tm, d), lambda i: (i, 0)),
            pl.BlockSpec((d, na), lambda i: (0, 0)),
            pl.BlockSpec((d, nb), lambda i: (0, 0)),
        ],
        out_specs=[
            pl.BlockSpec((tm, na), lambda i: (i, 0)),
            pl.BlockSpec((tm, nb), lambda i: (i, 0)),
        ],
        out_shape=[jax.ShapeDtypeStruct((t, na), F32), jax.ShapeDtypeStruct((t, nb), BF16)],
        compiler_params=_cparams(("parallel",)),
        name="in_proj",
    )(x, wa, wb)


def _layer_norm_rows(y, g, b):
    mu = jnp.mean(y, axis=-1, keepdims=True)
    yc = y - mu
    var = jnp.mean(yc * yc, axis=-1, keepdims=True)
    return yc * lax.rsqrt(var + EPS) * g + b


def _outproj_kernel(ya_ref, yb_ref, x_ref, w_ref, g_ref, b_ref, o_ref):
    acc = jnp.dot(ya_ref[...], w_ref[0:HALF, :], preferred_element_type=F32)
    acc = acc + jnp.dot(yb_ref[...], w_ref[HALF:D_MODEL, :], preferred_element_type=F32)
    y = ALPHA * x_ref[...] + acc
    o_ref[...] = _layer_norm_rows(y, g_ref[...], b_ref[...])


def _outproj(ya, yb, x, w, g, b, tm=512):
    t = x.shape[0]
    return pl.pallas_call(
        _outproj_kernel,
        grid=(t // tm,),
        in_specs=[
            pl.BlockSpec((tm, HALF), lambda i: (i, 0)),
            pl.BlockSpec((tm, HALF), lambda i: (i, 0)),
            pl.BlockSpec((tm, D_MODEL), lambda i: (i, 0)),
            pl.BlockSpec((D_MODEL, D_MODEL), lambda i: (0, 0)),
            pl.BlockSpec((1, D_MODEL), lambda i: (0, 0)),
            pl.BlockSpec((1, D_MODEL), lambda i: (0, 0)),
        ],
        out_specs=pl.BlockSpec((tm, D_MODEL), lambda i: (i, 0)),
        out_shape=jax.ShapeDtypeStruct((t, D_MODEL), F32),
        compiler_params=_cparams(("parallel",)),
        name="out_proj_ln",
    )(ya, yb, x, w, g, b)


def _ffn_kernel(x_ref, wa_ref, wb_ref, w2_ref, g_ref, b_ref, o_ref, xb_ref, acc_ref):
    f = pl.program_id(1)

    @pl.when(f == 0)
    def _():
        xb_ref[...] = x_ref[...].astype(BF16)

    xb = xb_ref[...]
    a = jnp.dot(xb, wa_ref[...], preferred_element_type=F32)
    b = jnp.dot(xb, wb_ref[...], preferred_element_type=F32)
    h = (a * jax.nn.sigmoid(a) * b).astype(BF16)
    part = jnp.dot(h, w2_ref[...], preferred_element_type=F32)

    @pl.when(f == 0)
    def _():
        acc_ref[...] = part

    @pl.when(f > 0)
    def _():
        acc_ref[...] += part

    @pl.when(f == pl.num_programs(1) - 1)
    def _():
        y = ALPHA * x_ref[...] + acc_ref[...]
        o_ref[...] = _layer_norm_rows(y, g_ref[...], b_ref[...])


def _ffn(x, w13, w2, g, b, tm=1024, tf=256):
    t = x.shape[0]
    nf = D_FF // tf
    return pl.pallas_call(
        _ffn_kernel,
        grid=(t // tm, nf),
        in_specs=[
            pl.BlockSpec((tm, D_MODEL), lambda i, f: (i, 0)),
            pl.BlockSpec((D_MODEL, tf), lambda i, f: (0, f)),
            pl.BlockSpec((D_MODEL, tf), lambda i, f: (0, f + nf)),
            pl.BlockSpec((tf, D_MODEL), lambda i, f: (f, 0)),
            pl.BlockSpec((1, D_MODEL), lambda i, f: (0, 0)),
            pl.BlockSpec((1, D_MODEL), lambda i, f: (0, 0)),
        ],
        out_specs=pl.BlockSpec((tm, D_MODEL), lambda i, f: (i, 0)),
        out_shape=jax.ShapeDtypeStruct((t, D_MODEL), F32),
        scratch_shapes=[pltpu.VMEM((tm, D_MODEL), BF16), pltpu.VMEM((tm, D_MODEL), F32)],
        compiler_params=_cparams(("parallel", "arbitrary")),
        name="ffn_ln",
    )(x, w13, w13, w2, g, b)


def _tile_rows(n):
    return lax.broadcasted_iota(jnp.int32, (SUBLANES, n), 0)


def _bcast_row(tile, r):
    return jnp.broadcast_to(tile[r:r + 1, :], tile.shape)


def _segmented_scans(x):
    n = len(x)
    row = _tile_rows(x[0].shape[1])
    zero = jnp.zeros_like(x[0])
    r3 = row & 3
    up4 = (row & 4) != 0

    p = {1: list(x)}
    p[2] = [t + jnp.where((row & 1) == 1, pltpu.roll(t, 1, 0), 0.0) for t in x]
    p[4] = [t + jnp.where(r3 == 2, pltpu.roll(t, 1, 0),
                          jnp.where(r3 == 3, pltpu.roll(t, 2, 0), 0.0)) for t in p[2]]
    p[8] = [t + jnp.where(up4, _bcast_row(t, 3), 0.0) for t in p[4]]

    e = {1: [zero] * n}
    e[2] = [jnp.where((row & 1) == 0, pltpu.roll(t, SUBLANES - 1, 0), 0.0) for t in x]
    r2 = [a + b for a, b in zip(e[2], x)]
    e[4] = [t + jnp.where(r3 == 1, pltpu.roll(r, SUBLANES - 1, 0),
                          jnp.where(r3 == 0, pltpu.roll(r, SUBLANES - 2, 0), 0.0))
            for t, r in zip(e[2], r2)]
    r4 = [a + b for a, b in zip(e[4], x)]
    e[8] = [t + jnp.where(up4, 0.0, _bcast_row(r, 4)) for t, r in zip(e[4], r4)]

    m = SUBLANES
    while m < SUBLANES * n:
        nt = m // SUBLANES
        pn, en = [], []
        for g in range(n // (2 * nt)):
            lo = slice(2 * nt * g, 2 * nt * g + nt)
            hi = slice(2 * nt * g + nt, 2 * nt * (g + 1))
            tot = _bcast_row(p[m][lo][-1], SUBLANES - 1)
            pn += p[m][lo] + [t + tot for t in p[m][hi]]
            first = e[m][hi][0][0:1, :] + x[hi][0][0:1, :]
            tot_hi = jnp.broadcast_to(first, zero.shape)
            en += [t + tot_hi for t in e[m][lo]] + e[m][hi]
        p[2 * m], e[2 * m] = pn, en
        m *= 2
    return p, e


def _level_map(c):
    t = lax.broadcasted_iota(jnp.int32, (c, c), 0)
    s = lax.broadcasted_iota(jnp.int32, (c, c), 1)
    x = t ^ s
    lv = jnp.full((c, c), -1, jnp.int32)
    m = 1
    while m < c:
        lv = lv + (x >= m).astype(jnp.int32)
        m *= 2
    return jnp.where(t < s, -2, lv)


def _hgrn_kernel(q_ref, f_ref, g_ref, v_ref, lb_ref, ng_ref, o_ref, state_ref):
    c = HGRN_CHUNK
    n_tiles = c // SUBLANES
    n_chunks = q_ref.shape[0] // c
    lb = lb_ref[...]
    one_m_lb = 1.0 - lb
    ng = ng_ref[...]
    lv = _level_map(c)
    state_ref[...] = jnp.zeros_like(state_ref)

    def chunk(ci, _):
        r0 = pl.multiple_of(ci * c, c)
        z = f_ref[pl.ds(r0, c), :]
        q = q_ref[pl.ds(r0, c), :]
        v = v_ref[pl.ds(r0, c), :]
        ez = jnp.exp(-jnp.abs(z))
        rz = 1.0 / (1.0 + ez)
        pos = z >= 0.0
        sig = jnp.where(pos, rz, ez * rz)
        nsig = jnp.where(pos, ez * rz, rz)
        lf = jnp.log(lb + one_m_lb * sig)
        kk = one_m_lb * nsig

        lf_tiles = [lf[SUBLANES * i:SUBLANES * (i + 1), :] for i in range(n_tiles)]
        p, e = _segmented_scans(lf_tiles)

        def cat(ts):
            return jnp.concatenate(ts, axis=0)

        nt_dims = (((1,), (1,)), ((), ()))
        scores = jnp.where(
            lv == -1,
            lax.dot_general(q.astype(BF16), kk.astype(BF16), nt_dims, preferred_element_type=F32),
            0.0)
        m, idx = 1, 0
        while m < c:
            qm = (q * jnp.exp(cat(p[m]))).astype(BF16)
            km = kk if m == 1 else kk * jnp.exp(cat(e[m]))
            sm = lax.dot_general(qm, km.astype(BF16), nt_dims, preferred_element_type=F32)
            scores = jnp.where(lv == idx, sm, scores)
            m *= 2
            idx += 1

        b = cat(p[c])
        qc = (q * jnp.exp(b)).astype(BF16)
        kc = (kk * jnp.exp(cat(e[c]))).astype(BF16)
        st = state_ref[...]
        o = jnp.dot(qc, st.astype(BF16), preferred_element_type=F32)
        o = o + jnp.dot(scores.astype(BF16), v, preferred_element_type=F32)

        decay = jnp.exp(b[c - 1:c, :])
        kv = lax.dot_general(kc, v, (((0,), (0,)), ((), ())), preferred_element_type=F32)
        state_ref[...] = st * jnp.broadcast_to(decay, st.shape).T + kv

        ms = jnp.mean(o * o, axis=-1, keepdims=True)
        on = o * lax.rsqrt(ms + EPS) * ng
        gg = g_ref[pl.ds(r0, c), :]
        o_ref[pl.ds(r0, c), :] = (on * (gg * jax.nn.sigmoid(gg))).astype(o_ref.dtype)
        return 0

    lax.fori_loop(0, n_chunks, chunk, 0)


def _hgrn(pa, pb, lb, ng, batch, seq):
    t = pa.shape[0]
    nh = HALF // HGRN_DIM
    return pl.pallas_call(
        _hgrn_kernel,
        grid=(batch, nh),
        in_specs=[
            pl.BlockSpec((seq, HGRN_DIM), lambda b, h: (b, h)),
            pl.BlockSpec((seq, HGRN_DIM), lambda b, h: (b, nh + h)),
            pl.BlockSpec((seq, HGRN_DIM), lambda b, h: (b, 2 * nh + h)),
            pl.BlockSpec((seq, HGRN_DIM), lambda b, h: (b, h)),
            pl.BlockSpec((1, HGRN_DIM), lambda b, h: (0, h)),
            pl.BlockSpec((1, HGRN_DIM), lambda b, h: (0, 0)),
        ],
        out_specs=pl.BlockSpec((seq, HGRN_DIM), lambda b, h: (b, h)),
        out_shape=jax.ShapeDtypeStruct((t, HALF), BF16),
        scratch_shapes=[pltpu.VMEM((HGRN_DIM, HGRN_DIM), F32)],
        compiler_params=_cparams(("parallel", "parallel")),
        name="hgrn2",
    )(pa, pa, pa, pb, lb, ng)


def _fox_gate_kernel(f_ref, bf_ref, c_ref):
    s = f_ref.shape[0]
    ft = f_ref[...].T[0:SUBLANES, :]
    x = ft + bf_ref[...][:, 0:1]
    lf = jnp.minimum(x, 0.0) - jnp.log1p(jnp.exp(-jnp.abs(x)))
    lane = lax.broadcasted_iota(jnp.int32, lf.shape, 1)
    d = 1
    while d < s:
        lf = lf + jnp.where(lane >= d, pltpu.roll(lf, d, 1), 0.0)
        d *= 2
    c_ref[0] = lf


def _fox_gate(pa, col_block, bf_pad, batch, seq):
    return pl.pallas_call(
        _fox_gate_kernel,
        grid=(batch,),
        in_specs=[
            pl.BlockSpec((seq, LANES), lambda b: (b, col_block)),
            pl.BlockSpec((SUBLANES, LANES), lambda b: (0, 0)),
        ],
        out_specs=pl.BlockSpec((1, SUBLANES, seq), lambda b: (b, 0, 0)),
        out_shape=jax.ShapeDtypeStruct((batch, SUBLANES, seq), F32),
        compiler_params=_cparams(("parallel",)),
        name="fox_gate",
    )(pa, bf_pad)


def _head_lane_masks(shape):
    lane = lax.broadcasted_iota(jnp.int32, shape, len(shape) - 1)
    first = lane < HEAD_DIM
    return first, jnp.logical_not(first)


def _causal_tile_order(nq):
    pairs = [(i, i - d) for d in range(nq) for i in range(d, nq)]
    qi = jnp.asarray([p[0] for p in pairs], jnp.int32)
    kj = jnp.asarray([p[1] for p in pairs], jnp.int32)
    return qi, kj


def _split_heads(q_ref, v_ref, qh_ref, vh_ref):
    scale = HEAD_DIM ** -0.5
    q = q_ref[...] * jnp.asarray(scale, BF16)
    v = v_ref[...]
    h0, h1 = _head_lane_masks(q.shape)
    zero = jnp.zeros_like(q)
    qh_ref[0] = jnp.where(h0, q, zero)
    qh_ref[1] = jnp.where(h1, q, zero)
    vh_ref[0] = jnp.where(h0, v, zero)
    vh_ref[1] = jnp.where(h1, v, zero)


def _tile_loops(n_diag, n_tiles, tile_fn):
    u = ATT_UNROLL

    def body(it, diag):
        for k in range(u):
            tile_fn(it * u + k, diag)
        return 0

    lax.fori_loop(0, n_diag // u, lambda it, _: body(it, True), 0)
    lax.fori_loop(n_diag // u, n_tiles // u, lambda it, _: body(it, False), 0)


def _fox_kernel(qi_ref, kj_ref, q_ref, k_ref, v_ref, c_ref, o_ref,
                qh_ref, vh_ref, m_ref, l_ref, acc_ref):
    tq = tk = ATT_BLOCK
    nq = q_ref.shape[0] // tq
    nt_dims = (((1,), (1,)), ((), ()))
    o0m, _ = _head_lane_masks((tq, LANES))

    _split_heads(q_ref, v_ref, qh_ref, vh_ref)
    m_ref[...] = jnp.full(m_ref.shape, -jnp.inf, F32)
    l_ref[...] = jnp.zeros(l_ref.shape, F32)
    acc_ref[...] = jnp.zeros(acc_ref.shape, F32)

    def tile(n, diag):
        i = qi_ref[n]
        j = kj_ref[n]
        q0 = pl.multiple_of(i * tq, tq)
        k0 = pl.multiple_of(j * tk, tk)
        k = k_ref[pl.ds(k0, tk), :]
        alphas, pvs = [], []
        for h in range(2):
            s = lax.dot_general(qh_ref[h, pl.ds(q0, tq), :], k, nt_dims, preferred_element_type=F32)
            s = s - c_ref[0, 0, h:h + 1, pl.ds(k0, tk)]
            if diag:
                qpos = lax.broadcasted_iota(jnp.int32, s.shape, 0)
                kpos = lax.broadcasted_iota(jnp.int32, s.shape, 1)
                s = jnp.where(kpos <= qpos, s, -jnp.inf)
            m_old = m_ref[i, h]
            m_new = jnp.maximum(m_old, jnp.max(s, axis=-1, keepdims=True))
            alpha = jnp.exp(m_old - m_new)
            p = jnp.exp(s - jnp.concatenate([m_new] * (tk // LANES), axis=1))
            psum = p[:, 0:LANES]
            for c in range(1, tk // LANES):
                psum = psum + p[:, c * LANES:(c + 1) * LANES]
            l_ref[i, h] = alpha * l_ref[i, h] + psum
            m_ref[i, h] = m_new
            alphas.append(alpha)
            pvs.append(jnp.dot(p.astype(BF16), vh_ref[h, pl.ds(k0, tk), :], preferred_element_type=F32))
        acc_ref[i] = acc_ref[i] * jnp.where(o0m, alphas[0], alphas[1]) + pvs[0] + pvs[1]

    _tile_loops(nq, qi_ref.shape[0], tile)

    for i in range(nq):
        l0 = jnp.sum(l_ref[i, 0], axis=-1, keepdims=True)
        l1 = jnp.sum(l_ref[i, 1], axis=-1, keepdims=True)
        o_ref[i * tq:(i + 1) * tq, :] = (acc_ref[i] / jnp.where(o0m, l0, l1)).astype(o_ref.dtype)


def _att_scratch(seq, nq):
    return [
        pltpu.VMEM((2, seq, LANES), BF16),
        pltpu.VMEM((2, seq, LANES), BF16),
    ]


def _fox(pb, col0, cpair, batch, seq):
    t = pb.shape[0]
    npair = HALF // LANES
    nq = seq // ATT_BLOCK
    qi, kj = _causal_tile_order(nq)
    grid_spec = pltpu.PrefetchScalarGridSpec(
        num_scalar_prefetch=2,
        grid=(batch, npair),
        in_specs=[
            pl.BlockSpec((seq, LANES), lambda b, p, qi, kj: (b, col0 + p)),
            pl.BlockSpec((seq, LANES), lambda b, p, qi, kj: (b, col0 + npair + p)),
            pl.BlockSpec((seq, LANES), lambda b, p, qi, kj: (b, col0 + 2 * npair + p)),
            pl.BlockSpec((1, 1, 2, seq), lambda b, p, qi, kj: (b, p, 0, 0)),
        ],
        out_specs=pl.BlockSpec((seq, LANES), lambda b, p, qi, kj: (b, p)),
        scratch_shapes=_att_scratch(seq, nq) + [
            pltpu.VMEM((nq, 2, ATT_BLOCK, LANES), F32),
            pltpu.VMEM((nq, 2, ATT_BLOCK, LANES), F32),
            pltpu.VMEM((nq, ATT_BLOCK, LANES), F32),
        ],
    )
    return pl.pallas_call(
        _fox_kernel,
        grid_spec=grid_spec,
        out_shape=jax.ShapeDtypeStruct((t, HALF), BF16),
        compiler_params=_cparams(("parallel", "parallel")),
        name="fox_attention",
    )(qi, kj, pb, pb, pb, cpair)


def _suffix_matrix(tk):
    r = lax.broadcasted_iota(jnp.int32, (2 * tk, 2 * tk), 0) % tk
    cidx = lax.broadcasted_iota(jnp.int32, (2 * tk, 2 * tk), 1)
    keep = jnp.logical_or(cidx >= tk, r > cidx)
    return jnp.where(keep, 1.0, 0.0).astype(BF16)


def _sb_kernel(qi_ref, kj_ref, q_ref, k_ref, v_ref, o_ref, qh_ref, vh_ref, run_ref, acc_ref):
    tq = tk = ATT_BLOCK
    nq = q_ref.shape[0] // tq
    nsub = tk // LANES
    nt_dims = (((1,), (1,)), ((), ()))
    umat = _suffix_matrix(LANES)

    _split_heads(q_ref, v_ref, qh_ref, vh_ref)
    run_ref[...] = jnp.zeros(run_ref.shape, F32)
    acc_ref[...] = jnp.zeros(acc_ref.shape, F32)

    def tile(n, diag):
        i = qi_ref[n]
        j = kj_ref[n]
        q0 = pl.multiple_of(i * tq, tq)
        k0 = pl.multiple_of(j * tk, tk)
        k = k_ref[pl.ds(k0, tk), :]
        acc = acc_ref[i]
        for h in range(2):
            z = lax.dot_general(qh_ref[h, pl.ds(q0, tq), :], k, nt_dims, preferred_element_type=F32)
            lom = jnp.minimum(-z, 0.0) - jnp.log(1.0 + jnp.exp(-jnp.abs(z)))
            if diag:
                qpos = lax.broadcasted_iota(jnp.int32, z.shape, 0)
                kpos = lax.broadcasted_iota(jnp.int32, z.shape, 1)
                mask = kpos < qpos
                lom = jnp.where(mask, lom, 0.0)
            carry = run_ref[i, h]
            suffix = [None] * nsub
            for c in reversed(range(nsub)):
                part = lom[:, c * LANES:(c + 1) * LANES]
                hi = part.astype(BF16)
                lo = (part - hi.astype(F32)).astype(BF16)
                r = jnp.dot(jnp.concatenate([hi, lo], axis=1), umat, preferred_element_type=F32)
                suffix[c] = r[:, 0:LANES] + carry
                carry = carry + r[:, LANES:2 * LANES]
            run_ref[i, h] = carry
            w = jnp.exp(lom + z + jnp.concatenate(suffix, axis=1))
            if diag:
                w = jnp.where(mask, w, 0.0)
            acc = acc + jnp.dot(w.astype(BF16), vh_ref[h, pl.ds(k0, tk), :], preferred_element_type=F32)
        acc_ref[i] = acc

    _tile_loops(nq, qi_ref.shape[0], tile)

    for i in range(nq):
        o_ref[i * tq:(i + 1) * tq, :] = acc_ref[i].astype(o_ref.dtype)


def _sb(pb, col0, batch, seq):
    t = pb.shape[0]
    npair = HALF // LANES
    nq = seq // ATT_BLOCK
    qi, kj = _causal_tile_order(nq)
    grid_spec = pltpu.PrefetchScalarGridSpec(
        num_scalar_prefetch=2,
        grid=(batch, npair),
        in_specs=[
            pl.BlockSpec((seq, LANES), lambda b, p, qi, kj: (b, col0 + p)),
            pl.BlockSpec((seq, LANES), lambda b, p, qi, kj: (b, col0 + npair + p)),
            pl.BlockSpec((seq, LANES), lambda b, p, qi, kj: (b, col0 + 2 * npair + p)),
        ],
        out_specs=pl.BlockSpec((seq, LANES), lambda b, p, qi, kj: (b, p)),
        scratch_shapes=_att_scratch(seq, nq) + [
            pltpu.VMEM((nq, 2, ATT_BLOCK, LANES), F32),
            pltpu.VMEM((nq, ATT_BLOCK, LANES), F32),
        ],
    )
    return pl.pallas_call(
        _sb_kernel,
        grid_spec=grid_spec,
        out_shape=jax.ShapeDtypeStruct((t, HALF), BF16),
        compiler_params=_cparams(("parallel", "parallel")),
        name="sb_attention",
    )(qi, kj, pb, pb, pb)


def _lru_kernel(x_ref, gate_ref, cw_ref, cb_ref, wg_ref, bg_ref, lam_ref, o_ref,
                xe_ref, a_ref, u_ref, h_ref):
    sblk = pl.program_id(1)
    ts, w = x_ref.shape
    pad = SUBLANES

    @pl.when(sblk == 0)
    def _():
        xe_ref[0:pad, :] = jnp.zeros((pad, w), F32)
        h_ref[...] = jnp.zeros_like(h_ref)

    @pl.when(sblk > 0)
    def _():
        xe_ref[0:pad, :] = xe_ref[ts:ts + pad, :]

    x = x_ref[...]
    xe_ref[pad:pad + ts, :] = x
    xc = cb_ref[...] + cw_ref[CONV_WIDTH - 1:CONV_WIDTH, :] * x
    for d in range(1, CONV_WIDTH):
        xc = xc + cw_ref[CONV_WIDTH - 1 - d:CONV_WIDTH - d, :] * xe_ref[pad - d:pad - d + ts, :]

    gates = jnp.dot(xc.astype(BF16), wg_ref[...], preferred_element_type=F32) + bg_ref[...]
    r = jax.nn.sigmoid(gates[:, 0:w])
    ig = jax.nn.sigmoid(gates[:, w:2 * w])
    lam = lam_ref[...]
    softplus_neg_lam = jnp.maximum(-lam, 0.0) + jnp.log1p(jnp.exp(-jnp.abs(lam)))
    log_a = -LRU_C * r * softplus_neg_lam
    a_ref[...] = jnp.exp(log_a)
    th = jnp.tanh(-log_a)
    u_ref[...] = jnp.sqrt(2.0 * th / (1.0 + th)) * (ig * xc)

    row = _tile_rows(w)

    def tile(ti, h):
        r0 = pl.multiple_of(ti * SUBLANES, SUBLANES)
        a = a_ref[pl.ds(r0, SUBLANES), :]
        u = u_ref[pl.ds(r0, SUBLANES), :]
        for d in (1, 2, 4):
            sel = row >= d
            a_s = jnp.where(sel, pltpu.roll(a, d, 0), 1.0)
            u_s = jnp.where(sel, pltpu.roll(u, d, 0), 0.0)
            u = a * u_s + u
            a = a * a_s
        hh = a * h + u
        gt = gate_ref[pl.ds(r0, SUBLANES), :]
        o_ref[pl.ds(r0, SUBLANES), :] = (hh * jax.nn.gelu(gt, approximate=True)).astype(o_ref.dtype)
        return hh[SUBLANES - 1:SUBLANES, :]

    h_ref[...] = lax.fori_loop(0, ts // SUBLANES, tile, h_ref[...])


def _lru(pa, cw, cb, wg, bg, lam, batch, seq, ts=256):
    t = pa.shape[0]
    w = HALF
    ns = seq // ts
    return pl.pallas_call(
        _lru_kernel,
        grid=(batch, ns),
        in_specs=[
            pl.BlockSpec((ts, w), lambda b, s: (b * ns + s, 0)),
            pl.BlockSpec((ts, w), lambda b, s: (b * ns + s, 1)),
            pl.BlockSpec((CONV_WIDTH, w), lambda b, s: (0, 0)),
            pl.BlockSpec((1, w), lambda b, s: (0, 0)),
            pl.BlockSpec((w, 2 * w), lambda b, s: (0, 0)),
            pl.BlockSpec((1, 2 * w), lambda b, s: (0, 0)),
            pl.BlockSpec((1, w), lambda b, s: (0, 0)),
        ],
        out_specs=pl.BlockSpec((ts, w), lambda b, s: (b * ns + s, 0)),
        out_shape=jax.ShapeDtypeStruct((t, w), BF16),
        scratch_shapes=[
            pltpu.VMEM((ts + SUBLANES, w), F32),
            pltpu.VMEM((ts, w), F32),
            pltpu.VMEM((ts, w), F32),
            pltpu.VMEM((1, w), F32),
        ],
        compiler_params=_cparams(("parallel", "arbitrary")),
        name="rg_lru",
    )(pa, pa, cw, cb, wg, bg, lam)


def _block_diag(wb):
    n, bd, _ = wb.shape
    eye = jnp.eye(n, dtype=wb.dtype)
    return (eye[:, None, :, None] * wb[:, :, None, :]).reshape(n * bd, n * bd)


def kernel(x, ev_w_in, ev_fox_bf, hgrn_lb, ev_hgrn_norm_g, ev_w_out, od_w_in, od_conv_w, od_conv_b,
           od_gate_a_w, od_gate_a_b, od_gate_x_w, od_gate_x_b, od_lru_lambda, od_w_out, ffn_w13,
           ffn_w2, ln_g, ln_b):
    batch, seq, d = x.shape
    t = batch * seq
    xf = x.reshape(t, d)
    lb_all = jnp.cumsum(jax.nn.softmax(hgrn_lb.astype(F32), axis=0), axis=0)
    h = HALF

    w = ev_w_in[0]
    nfox = ev_fox_bf.shape[1]
    w_fb = jnp.pad(w[:, 7 * h:7 * h + nfox], ((0, 0), (0, LANES - nfox)))
    wa = jnp.concatenate([w[:, 0:2 * h], w[:, 3 * h:4 * h], w_fb], axis=1).astype(BF16)
    wb = jnp.concatenate([w[:, 2 * h:3 * h], w[:, 4 * h:7 * h]], axis=1).astype(BF16)
    pa, pb = _proj(xf, wa, wb)
    ya = _hgrn(pa, pb, lb_all[0].reshape(1, h), ev_hgrn_norm_g[0].reshape(1, HGRN_DIM), batch, seq)
    bf_pad = jnp.broadcast_to(ev_fox_bf[0].astype(F32).reshape(nfox, 1), (nfox, LANES))
    cum = _fox_gate(pa, 3 * h // LANES, bf_pad, batch, seq)
    yb = _fox(pb, h // LANES, cum.reshape(batch, nfox // 2, 2, seq), batch, seq)
    x1 = _outproj(ya, yb, xf, ev_w_out[0].astype(BF16), ln_g[0, 0].reshape(1, d), ln_b[0, 0].reshape(1, d))
    x2 = _ffn(x1, ffn_w13[0].astype(BF16), ffn_w2[0].astype(BF16),
              ln_g[0, 1].reshape(1, d), ln_b[0, 1].reshape(1, d))

    w = od_w_in[0]
    wa = w[:, 0:2 * h].astype(BF16)
    wb = w[:, 2 * h:5 * h].astype(BF16)
    pa, pb = _proj(x2, wa, wb)
    wg = jnp.concatenate([_block_diag(od_gate_a_w[0]), _block_diag(od_gate_x_w[0])], axis=1).astype(BF16)
    bg = jnp.concatenate([od_gate_a_b[0], od_gate_x_b[0]]).reshape(1, 2 * h).astype(F32)
    yc = _lru(pa, od_conv_w[0], od_conv_b[0].reshape(1, h), wg, bg, od_lru_lambda[0].reshape(1, h),
              batch, seq)
    yd = _sb(pb, 0, batch, seq)
    x3 = _outproj(yc, yd, x2, od_w_out[0].astype(BF16), ln_g[1, 0].reshape(1, d), ln_b[1, 0].reshape(1, d))
    x4 = _ffn(x3, ffn_w13[1].astype(BF16), ffn_w2[1].astype(BF16),
              ln_g[1, 1].reshape(1, d), ln_b[1, 1].reshape(1, d))
    return x4.reshape(batch, seq, d)
```

```python
import functools

import jax
import jax.numpy as jnp
from jax import lax
from jax.experimental import pallas as pl
from jax.experimental.pallas import tpu as pltpu

F32 = jnp.float32
BF16 = jnp.bfloat16

SUBLANES = 8
LANES = 128

D_MODEL = 1024
HALF = D_MODEL // 2
HEAD_DIM = 64
HGRN_DIM = 128
LRU_BLOCKS = 8
LRU_C = 8.0
CONV_WIDTH = 4
DEPTH = 2
ALPHA = (2 * DEPTH) ** 0.25
EPS = 1e-5
D_FF = 2816
LOG2_E = 1.4426950408889634

HGRN_CHUNK = 128
HGRN_UNROLL = 4
ATT_BLOCK = 256
ATT_UNROLL = 4
VMEM_LIMIT = 56 * 1024 * 1024


def _cparams(sem):
    return pltpu.CompilerParams(dimension_semantics=sem, vmem_limit_bytes=VMEM_LIMIT)


def _proj_kernel(x_ref, wa_ref, wb_ref, oa_ref, ob_ref):
    xb = x_ref[...].astype(BF16)
    oa_ref[...] = jnp.dot(xb, wa_ref[...], preferred_element_type=F32)
    ob_ref[...] = jnp.dot(xb, wb_ref[...], preferred_element_type=F32).astype(BF16)


def _proj(x, wa, wb, tm=512):
    t, d = x.shape
    na, nb = wa.shape[1], wb.shape[1]
    return pl.pallas_call(
        _proj_kernel,
        grid=(t // tm,),
        in_specs=[
            pl.BlockSpec((tm, d), lambda i: (i, 0)),
            pl.BlockSpec((d, na), lambda i: (0, 0)),
            pl.BlockSpec((d, nb), lambda i: (0, 0)),
        ],
        out_specs=[
            pl.BlockSpec((tm, na), lambda i: (i, 0)),
            pl.BlockSpec((tm, nb), lambda i: (i, 0)),
        ],
        out_shape=[jax.ShapeDtypeStruct((t, na), F32), jax.ShapeDtypeStruct((t, nb), BF16)],
        compiler_params=_cparams(("parallel",)),
        name="in_proj",
    )(x, wa, wb)


def _lockstep(gens):
    results = [None] * len(gens)
    live = list(range(len(gens)))
    while live:
        for k in list(live):
            try:
                next(gens[k])
            except StopIteration as done:
                results[k] = done.value
                live.remove(k)
    return results


def _layer_norm_rows(y, g, b):
    mu = jnp.mean(y, axis=-1, keepdims=True)
    yc = y - mu
    var = jnp.mean(yc * yc, axis=-1, keepdims=True)
    return yc * lax.rsqrt(var + EPS) * g + b


def _outproj_kernel(ya_ref, yb_ref, x_ref, w_ref, g_ref, b_ref, o_ref):
    acc = jnp.dot(ya_ref[...], w_ref[0:HALF, :], preferred_element_type=F32)
    acc = acc + jnp.dot(yb_ref[...], w_ref[HALF:D_MODEL, :], preferred_element_type=F32)
    y = ALPHA * x_ref[...] + acc
    o_ref[...] = _layer_norm_rows(y, g_ref[...], b_ref[...])


def _outproj(ya, yb, x, w, g, b, tm=512):
    t = x.shape[0]
    return pl.pallas_call(
        _outproj_kernel,
        grid=(t // tm,),
        in_specs=[
            pl.BlockSpec((tm, HALF), lambda i: (i, 0)),
            pl.BlockSpec((tm, HALF), lambda i: (i, 0)),
            pl.BlockSpec((tm, D_MODEL), lambda i: (i, 0)),
            pl.BlockSpec((D_MODEL, D_MODEL), lambda i: (0, 0)),
            pl.BlockSpec((1, D_MODEL), lambda i: (0, 0)),
            pl.BlockSpec((1, D_MODEL), lambda i: (0, 0)),
        ],
        out_specs=pl.BlockSpec((tm, D_MODEL), lambda i: (i, 0)),
        out_shape=jax.ShapeDtypeStruct((t, D_MODEL), F32),
        compiler_params=_cparams(("parallel",)),
        name="out_proj_ln",
    )(ya, yb, x, w, g, b)


def _ffn_kernel(x_ref, w13_ref, w2_ref, g_ref, b_ref, o_ref, xb_ref, h_ref):
    nf = w2_ref.shape[0]
    xb_ref[...] = x_ref[...].astype(BF16)

    def hidden(c):
        xb = xb_ref[...]
        a = jnp.dot(xb, w13_ref[c], preferred_element_type=F32)
        b = jnp.dot(xb, w13_ref[nf + c], preferred_element_type=F32)
        h_ref[c] = (a * jax.nn.sigmoid(a) * b).astype(BF16)

    hidden(0)

    def pair(n, _):
        hidden(2 * n + 1)
        hidden(2 * n + 2)
        return 0

    lax.fori_loop(0, (nf - 1) // 2, pair, 0)

    acc = jnp.dot(h_ref[0], w2_ref[0], preferred_element_type=F32)
    for c in range(1, nf):
        acc = acc + jnp.dot(h_ref[c], w2_ref[c], preferred_element_type=F32)
    y = ALPHA * x_ref[...] + acc
    o_ref[...] = _layer_norm_rows(y, g_ref[...], b_ref[...])


def _ffn(x, w13c, w2c, g, b, tm=1024):
    t = x.shape[0]
    nf, tf, _ = w2c.shape
    assert nf % 2 == 1
    resident = pl.Buffered(1)
    return pl.pallas_call(
        _ffn_kernel,
        grid=(t // tm,),
        in_specs=[
            pl.BlockSpec((tm, D_MODEL), lambda i: (i, 0)),
            pl.BlockSpec((2 * nf, D_MODEL, tf), lambda i: (0, 0, 0), pipeline_mode=resident),
            pl.BlockSpec((nf, tf, D_MODEL), lambda i: (0, 0, 0), pipeline_mode=resident),
            pl.BlockSpec((1, D_MODEL), lambda i: (0, 0)),
            pl.BlockSpec((1, D_MODEL), lambda i: (0, 0)),
        ],
        out_specs=pl.BlockSpec((tm, D_MODEL), lambda i: (i, 0)),
        out_shape=jax.ShapeDtypeStruct((t, D_MODEL), F32),
        scratch_shapes=[pltpu.VMEM((tm, D_MODEL), BF16), pltpu.VMEM((nf, tm, tf), BF16)],
        compiler_params=_cparams(("parallel",)),
        name="ffn_ln",
    )(x, w13c, w2c, g, b)


def _ffn_weights(w13, w2, tf=256):
    d, two_f = w13.shape
    w13c = w13.astype(BF16).reshape(d, two_f // tf, tf).transpose(1, 0, 2)
    w2c = w2.astype(BF16).reshape(two_f // (2 * tf), tf, w2.shape[1])
    return w13c, w2c


def _tile_rows(n):
    return lax.broadcasted_iota(jnp.int32, (SUBLANES, n), 0)


def _bcast_row(tile, r):
    return jnp.broadcast_to(tile[r:r + 1, :], tile.shape)


def _segmented_scans(x):
    n = len(x)
    row = _tile_rows(x[0].shape[1])
    zero = jnp.zeros_like(x[0])
    r3 = row & 3
    up4 = (row & 4) != 0

    p = {1: list(x)}
    p[2] = [t + jnp.where((row & 1) == 1, pltpu.roll(t, 1, 0), 0.0) for t in x]
    p[4] = [t + jnp.where(r3 == 2, pltpu.roll(t, 1, 0),
                          jnp.where(r3 == 3, pltpu.roll(t, 2, 0), 0.0)) for t in p[2]]
    p[8] = [t + jnp.where(up4, _bcast_row(t, 3), 0.0) for t in p[4]]

    e = {1: [zero] * n}
    e[2] = [jnp.where((row & 1) == 0, pltpu.roll(t, SUBLANES - 1, 0), 0.0) for t in x]
    r2 = [a + b for a, b in zip(e[2], x)]
    e[4] = [t + jnp.where(r3 == 1, pltpu.roll(r, SUBLANES - 1, 0),
                          jnp.where(r3 == 0, pltpu.roll(r, SUBLANES - 2, 0), 0.0))
            for t, r in zip(e[2], r2)]
    r4 = [a + b for a, b in zip(e[4], x)]
    e[8] = [t + jnp.where(up4, 0.0, _bcast_row(r, 4)) for t, r in zip(e[4], r4)]

    m = SUBLANES
    while m < SUBLANES * n:
        nt = m // SUBLANES
        pn, en = [], []
        for g in range(n // (2 * nt)):
            lo = slice(2 * nt * g, 2 * nt * g + nt)
            hi = slice(2 * nt * g + nt, 2 * nt * (g + 1))
            tot = _bcast_row(p[m][lo][-1], SUBLANES - 1)
            pn += p[m][lo] + [t + tot for t in p[m][hi]]
            first = e[m][hi][0][0:1, :] + x[hi][0][0:1, :]
            tot_hi = jnp.broadcast_to(first, zero.shape)
            en += [t + tot_hi for t in e[m][lo]] + e[m][hi]
        p[2 * m], e[2 * m] = pn, en
        m *= 2
    return p, e


def _level_map(c):
    t = lax.broadcasted_iota(jnp.int32, (c, c), 0)
    s = lax.broadcasted_iota(jnp.int32, (c, c), 1)
    x = t ^ s
    lv = jnp.full((c, c), -1, jnp.int32)
    m = 1
    while m < c:
        lv = lv + (x >= m).astype(jnp.int32)
        m *= 2
    return jnp.where(t < s, -2, lv)


def _hgrn_kernel(q_ref, f_ref, g_ref, v_ref, lb_ref, ng_ref, o_ref, state_ref):
    c = HGRN_CHUNK
    n_tiles = c // SUBLANES
    n_chunks = q_ref.shape[0] // c
    lb = lb_ref[...]
    one_m_lb = 1.0 - lb
    ng = ng_ref[...]
    lv = _level_map(c)
    state_ref[...] = jnp.zeros_like(state_ref)

    def chunk(ci, state):
        r0 = pl.multiple_of(ci * c, c)
        z = f_ref[pl.ds(r0, c), :]
        q = q_ref[pl.ds(r0, c), :]
        v = v_ref[pl.ds(r0, c), :]
        gg = g_ref[pl.ds(r0, c), :]
        ez = jnp.exp(-jnp.abs(z))
        rz = 1.0 / (1.0 + ez)
        pos = z >= 0.0
        sig = jnp.where(pos, rz, ez * rz)
        nsig = jnp.where(pos, ez * rz, rz)
        lf = jnp.log2(lb + one_m_lb * sig)
        kk = one_m_lb * nsig

        lf_tiles = [lf[SUBLANES * i:SUBLANES * (i + 1), :] for i in range(n_tiles)]
        p, e = _segmented_scans(lf_tiles)

        def cat(ts):
            return jnp.concatenate(ts, axis=0)

        nt_dims = (((1,), (1,)), ((), ()))
        qb, kb = q.astype(BF16), kk.astype(BF16)
        yield
        scores = jnp.where(
            lv == -1, lax.dot_general(qb, kb, nt_dims, preferred_element_type=F32), 0.0)
        m, idx = 1, 0
        while m < c:
            qm = (q * jnp.exp2(cat(p[m]))).astype(BF16)
            km = kb if m == 1 else (kk * jnp.exp2(cat(e[m]))).astype(BF16)
            yield
            sm = lax.dot_general(qm, km, nt_dims, preferred_element_type=F32)
            scores = jnp.where(lv == idx, sm, scores)
            m *= 2
            idx += 1

        b = cat(p[c])
        qc = (q * jnp.exp2(b)).astype(BF16)
        kc = (kk * jnp.exp2(cat(e[c]))).astype(BF16)
        decay = jnp.exp2(b[c - 1:c, :])
        sb = scores.astype(BF16)
        yield
        kv = lax.dot_general(kc, v, (((0,), (0,)), ((), ())), preferred_element_type=F32)
        o = jnp.dot(sb, v, preferred_element_type=F32)
        yield
        st = state[0]
        o = o + jnp.dot(qc, st.astype(BF16), preferred_element_type=F32)
        state[0] = st * jnp.broadcast_to(decay, st.shape).T + kv
        yield
        ms = jnp.mean(o * o, axis=-1, keepdims=True)
        on = o * lax.rsqrt(ms + EPS) * ng
        o_ref[pl.ds(r0, c), :] = (on * (gg * jax.nn.sigmoid(gg))).astype(o_ref.dtype)

    def body(it, _):
        state = [state_ref[...]]
        _lockstep([chunk(it * HGRN_UNROLL + k, state) for k in range(HGRN_UNROLL)])
        state_ref[...] = state[0]
        return 0

    lax.fori_loop(0, n_chunks // HGRN_UNROLL, body, 0)


def _hgrn(pa, pb, lb, ng, batch, seq):
    t = pa.shape[0]
    nh = HALF // HGRN_DIM
    return pl.pallas_call(
        _hgrn_kernel,
        grid=(batch, nh),
        in_specs=[
            pl.BlockSpec((seq, HGRN_DIM), lambda b, h: (b, h)),
            pl.BlockSpec((seq, HGRN_DIM), lambda b, h: (b, nh + h)),
            pl.BlockSpec((seq, HGRN_DIM), lambda b, h: (b, 2 * nh + h)),
            pl.BlockSpec((seq, HGRN_DIM), lambda b, h: (b, h)),
            pl.BlockSpec((1, HGRN_DIM), lambda b, h: (0, h)),
            pl.BlockSpec((1, HGRN_DIM), lambda b, h: (0, 0)),
        ],
        out_specs=pl.BlockSpec((seq, HGRN_DIM), lambda b, h: (b, h)),
        out_shape=jax.ShapeDtypeStruct((t, HALF), BF16),
        scratch_shapes=[pltpu.VMEM((HGRN_DIM, HGRN_DIM), F32)],
        compiler_params=_cparams(("parallel", "parallel")),
        name="hgrn2",
    )(pa, pa, pa, pb, lb, ng)


def _fox_gate_kernel(f_ref, bf_ref, c_ref):
    s = f_ref.shape[0]
    ft = f_ref[...].T[0:SUBLANES, :]
    x = ft + bf_ref[...][:, 0:1]
    lf = jnp.minimum(x, 0.0) - jnp.log1p(jnp.exp(-jnp.abs(x)))
    lane = lax.broadcasted_iota(jnp.int32, lf.shape, 1)
    d = 1
    while d < s:
        lf = lf + jnp.where(lane >= d, pltpu.roll(lf, d, 1), 0.0)
        d *= 2
    c_ref[0] = lf


def _fox_gate(pa, col_block, bf_pad, batch, seq):
    return pl.pallas_call(
        _fox_gate_kernel,
        grid=(batch,),
        in_specs=[
            pl.BlockSpec((seq, LANES), lambda b: (b, col_block)),
            pl.BlockSpec((SUBLANES, LANES), lambda b: (0, 0)),
        ],
        out_specs=pl.BlockSpec((1, SUBLANES, seq), lambda b: (b, 0, 0)),
        out_shape=jax.ShapeDtypeStruct((batch, SUBLANES, seq), F32),
        compiler_params=_cparams(("parallel",)),
        name="fox_gate",
    )(pa, bf_pad)


def _head_lane_masks(shape):
    lane = lax.broadcasted_iota(jnp.int32, shape, len(shape) - 1)
    first = lane < HEAD_DIM
    return first, jnp.logical_not(first)


def _causal_tile_order(nq):
    u = ATT_UNROLL
    assert nq % u == 0
    pairs = [(i, i) for i in range(nq)]
    left = {i: i for i in range(nq)}
    while any(left.values()):
        pick = sorted((i for i in left if left[i]), key=lambda i: -left[i])[:u]
        assert len(pick) == u, "tile list does not split into groups of distinct query blocks"
        for i in pick:
            pairs.append((i, left[i] - 1))
            left[i] -= 1
    qi = jnp.asarray([p[0] for p in pairs], jnp.int32)
    kj = jnp.asarray([p[1] for p in pairs], jnp.int32)
    return qi, kj


def _split_heads(q_ref, v_ref, qh_ref, vh_ref):
    scale = HEAD_DIM ** -0.5
    t = ATT_BLOCK
    h0, h1 = _head_lane_masks((t, LANES))
    zero = jnp.zeros((t, LANES), BF16)
    for i in range(q_ref.shape[0] // t):
        q = q_ref[i * t:(i + 1) * t, :] * jnp.asarray(scale, BF16)
        v = v_ref[i * t:(i + 1) * t, :]
        qh_ref[i, 0:t, :] = jnp.where(h0, q, zero)
        qh_ref[i, t:2 * t, :] = jnp.where(h1, q, zero)
        vh_ref[i, 0:t, :] = jnp.where(h0, v, zero)
        vh_ref[i, t:2 * t, :] = jnp.where(h1, v, zero)


def _tile_loops(n_diag, n_tiles, load_fn, compute_fn, store_fn):
    u = ATT_UNROLL

    def body(it, diag):
        loaded = [load_fn(it * u + k, diag) for k in range(u)]
        results = _lockstep([compute_fn(x, diag) for x in loaded])
        for x, r in zip(loaded, results):
            store_fn(x, r)
        return 0

    lax.fori_loop(0, n_diag // u, lambda it, _: body(it, True), 0)
    lax.fori_loop(n_diag // u, n_tiles // u, lambda it, _: body(it, False), 0)


def _fox_kernel(qi_ref, kj_ref, q_ref, k_ref, v_ref, c_ref, o_ref,
                qh_ref, vh_ref, m_ref, l_ref, acc_ref):
    tq = tk = ATT_BLOCK
    nq = q_ref.shape[0] // tq
    nt_dims = (((1,), (1,)), ((), ()))
    o0m, _ = _head_lane_masks((tq, LANES))

    _split_heads(q_ref, v_ref, qh_ref, vh_ref)

    def load(n, diag):
        i = qi_ref[n]
        j = kj_ref[n]
        k0 = pl.multiple_of(j * tk, tk)
        state = None if diag else (m_ref[i], l_ref[i], acc_ref[i])
        return i, qh_ref[i], k_ref[pl.ds(k0, tk), :], vh_ref[j], c_ref[0, 0, :, pl.ds(k0, tk)], state

    def compute(loaded, diag):
        _, qq, k, vv, cs, state = loaded
        s = lax.dot_general(qq, k, nt_dims, preferred_element_type=F32)
        yield
        s = jnp.concatenate([s[0:tq] - cs[0:1], s[tq:2 * tq] - cs[1:2]], axis=0)
        if diag:
            qpos = lax.broadcasted_iota(jnp.int32, s.shape, 0) & (tq - 1)
            kpos = lax.broadcasted_iota(jnp.int32, s.shape, 1)
            s = jnp.where(kpos <= qpos, s, -jnp.inf)
        row_max = jnp.max(s, axis=-1, keepdims=True)
        if diag:
            m_new = jnp.broadcast_to(row_max, (2 * tq, LANES))
        else:
            m_old, l_old, acc_old = state
            m_new = jnp.maximum(m_old, row_max)
            alpha = jnp.exp(m_old - m_new)
        p = jnp.exp(s - jnp.concatenate([m_new] * (tk // LANES), axis=1))
        psum = p[:, 0:LANES]
        for c in range(1, tk // LANES):
            psum = psum + p[:, c * LANES:(c + 1) * LANES]
        pb = p.astype(BF16)
        yield
        pv = jnp.dot(jnp.concatenate([pb[0:tq], pb[tq:2 * tq]], axis=1), vv,
                     preferred_element_type=F32)
        yield
        if diag:
            return m_new, psum, pv
        return (m_new, alpha * l_old + psum,
                acc_old * jnp.where(o0m, alpha[0:tq], alpha[tq:2 * tq]) + pv)

    def store(loaded, result):
        i = loaded[0]
        m_ref[i], l_ref[i], acc_ref[i] = result

    _tile_loops(nq, qi_ref.shape[0], load, compute, store)

    for i in range(nq):
        l0 = jnp.sum(l_ref[i, 0:tq], axis=-1, keepdims=True)
        l1 = jnp.sum(l_ref[i, tq:2 * tq], axis=-1, keepdims=True)
        o_ref[i * tq:(i + 1) * tq, :] = (acc_ref[i] / jnp.where(o0m, l0, l1)).astype(o_ref.dtype)


def _att_scratch(nq):
    return [
        pltpu.VMEM((nq, 2 * ATT_BLOCK, LANES), BF16),
        pltpu.VMEM((nq, 2 * ATT_BLOCK, LANES), BF16),
    ]


def _fox(pb, col0, cpair, batch, seq):
    t = pb.shape[0]
    npair = HALF // LANES
    nq = seq // ATT_BLOCK
    qi, kj = _causal_tile_order(nq)
    grid_spec = pltpu.PrefetchScalarGridSpec(
        num_scalar_prefetch=2,
        grid=(batch, npair),
        in_specs=[
            pl.BlockSpec((seq, LANES), lambda b, p, qi, kj: (b, col0 + p)),
            pl.BlockSpec((seq, LANES), lambda b, p, qi, kj: (b, col0 + npair + p)),
            pl.BlockSpec((seq, LANES), lambda b, p, qi, kj: (b, col0 + 2 * npair + p)),
            pl.BlockSpec((1, 1, 2, seq), lambda b, p, qi, kj: (b, p, 0, 0)),
        ],
        out_specs=pl.BlockSpec((seq, LANES), lambda b, p, qi, kj: (b, p)),
        scratch_shapes=_att_scratch(nq) + [
            pltpu.VMEM((nq, 2 * ATT_BLOCK, LANES), F32),
            pltpu.VMEM((nq, 2 * ATT_BLOCK, LANES), F32),
            pltpu.VMEM((nq, ATT_BLOCK, LANES), F32),
        ],
    )
    return pl.pallas_call(
        _fox_kernel,
        grid_spec=grid_spec,
        out_shape=jax.ShapeDtypeStruct((t, HALF), BF16),
        compiler_params=_cparams(("parallel", "parallel")),
        name="fox_attention",
    )(qi, kj, pb, pb, pb, cpair)


def _suffix_matrix(tk):
    r = lax.broadcasted_iota(jnp.int32, (2 * tk, 2 * tk), 0) % tk
    cidx = lax.broadcasted_iota(jnp.int32, (2 * tk, 2 * tk), 1)
    keep = jnp.logical_or(cidx >= tk, r > cidx)
    return jnp.where(keep, 1.0, 0.0).astype(BF16)


def _sb_kernel(qi_ref, kj_ref, q_ref, k_ref, v_ref, o_ref, qh_ref, vh_ref, run_ref, acc_ref):
    tq = tk = ATT_BLOCK
    nq = q_ref.shape[0] // tq
    nsub = tk // LANES
    nt_dims = (((1,), (1,)), ((), ()))
    umat = _suffix_matrix(LANES)

    _split_heads(q_ref, v_ref, qh_ref, vh_ref)

    def load(n, diag):
        i = qi_ref[n]
        j = kj_ref[n]
        k0 = pl.multiple_of(j * tk, tk)
        state = None if diag else (run_ref[i], acc_ref[i])
        return i, qh_ref[i], k_ref[pl.ds(k0, tk), :], vh_ref[j], state

    def compute(loaded, diag):
        _, qq, k, vv, state = loaded
        z = lax.dot_general(qq, k, nt_dims, preferred_element_type=F32)
        yield
        logb = jnp.minimum(z, 0.0) - jnp.log(1.0 + jnp.exp2(jnp.abs(z) * -LOG2_E))
        lom = logb - z
        if diag:
            qpos = lax.broadcasted_iota(jnp.int32, z.shape, 0) & (tq - 1)
            kpos = lax.broadcasted_iota(jnp.int32, z.shape, 1)
            mask = kpos < qpos
            lom = jnp.where(mask, lom, 0.0)
        parts = []
        for c in reversed(range(nsub)):
            part = lom[:, c * LANES:(c + 1) * LANES]
            hi = part.astype(BF16)
            lo = (part - hi.astype(F32)).astype(BF16)
            parts.append(jnp.concatenate([hi, lo], axis=1))
        yield
        r = jnp.dot(jnp.concatenate(parts, axis=0), umat, preferred_element_type=F32)
        yield
        carry = jnp.zeros((2 * tq, LANES), F32) if diag else state[0]
        suffix = [None] * nsub
        for n_c, c in enumerate(reversed(range(nsub))):
            rc = r[n_c * 2 * tq:(n_c + 1) * 2 * tq]
            suffix[c] = rc[:, 0:LANES] + carry
            carry = carry + rc[:, LANES:2 * LANES]
        w = jnp.exp(logb + jnp.concatenate(suffix, axis=1))
        if diag:
            w = jnp.where(mask, w, 0.0)
        wb = w.astype(BF16)
        yield
        wv = jnp.dot(jnp.concatenate([wb[0:tq], wb[tq:2 * tq]], axis=1), vv,
                     preferred_element_type=F32)
        yield
        return carry, (wv if diag else state[1] + wv)

    def store(loaded, result):
        i = loaded[0]
        run_ref[i], acc_ref[i] = result

    _tile_loops(nq, qi_ref.shape[0], load, compute, store)

    for i in range(nq):
        o_ref[i * tq:(i + 1) * tq, :] = acc_ref[i].astype(o_ref.dtype)


def _sb(pb, col0, batch, seq):
    t = pb.shape[0]
    npair = HALF // LANES
    nq = seq // ATT_BLOCK
    qi, kj = _causal_tile_order(nq)
    grid_spec = pltpu.PrefetchScalarGridSpec(
        num_scalar_prefetch=2,
        grid=(batch, npair),
        in_specs=[
            pl.BlockSpec((seq, LANES), lambda b, p, qi, kj: (b, col0 + p)),
            pl.BlockSpec((seq, LANES), lambda b, p, qi, kj: (b, col0 + npair + p)),
            pl.BlockSpec((seq, LANES), lambda b, p, qi, kj: (b, col0 + 2 * npair + p)),
        ],
        out_specs=pl.BlockSpec((seq, LANES), lambda b, p, qi, kj: (b, p)),
        scratch_shapes=_att_scratch(nq) + [
            pltpu.VMEM((nq, 2 * ATT_BLOCK, LANES), F32),
            pltpu.VMEM((nq, ATT_BLOCK, LANES), F32),
        ],
    )
    return pl.pallas_call(
        _sb_kernel,
        grid_spec=grid_spec,
        out_shape=jax.ShapeDtypeStruct((t, HALF), BF16),
        compiler_params=_cparams(("parallel", "parallel")),
        name="sb_attention",
    )(qi, kj, pb, pb, pb)


def _lru_kernel(x_ref, gate_ref, cw_ref, cb_ref, wg_ref, bg_ref, lam_ref, o_ref,
                xe_ref, a_ref, u_ref, h_ref):
    sblk = pl.program_id(1)
    ts, w = x_ref.shape
    pad = SUBLANES

    @pl.when(sblk == 0)
    def _():
        xe_ref[0:pad, :] = jnp.zeros((pad, w), F32)
        h_ref[...] = jnp.zeros_like(h_ref)

    @pl.when(sblk > 0)
    def _():
        xe_ref[0:pad, :] = xe_ref[ts:ts + pad, :]

    x = x_ref[...]
    xe_ref[pad:pad + ts, :] = x
    xc = cb_ref[...] + cw_ref[CONV_WIDTH - 1:CONV_WIDTH, :] * x
    for d in range(1, CONV_WIDTH):
        xc = xc + cw_ref[CONV_WIDTH - 1 - d:CONV_WIDTH - d, :] * xe_ref[pad - d:pad - d + ts, :]

    gates = jnp.dot(xc.astype(BF16), wg_ref[...], preferred_element_type=F32) + bg_ref[...]
    r = jax.nn.sigmoid(gates[:, 0:w])
    ig = jax.nn.sigmoid(gates[:, w:2 * w])
    lam = lam_ref[...]
    softplus_neg_lam = jnp.maximum(-lam, 0.0) + jnp.log1p(jnp.exp(-jnp.abs(lam)))
    log_a = -LRU_C * r * softplus_neg_lam
    a_ref[...] = jnp.exp(log_a)
    th = jnp.tanh(-log_a)
    u_ref[...] = jnp.sqrt(2.0 * th / (1.0 + th)) * (ig * xc)

    row = _tile_rows(w)

    def tile(ti, h):
        r0 = pl.multiple_of(ti * SUBLANES, SUBLANES)
        a = a_ref[pl.ds(r0, SUBLANES), :]
        u = u_ref[pl.ds(r0, SUBLANES), :]
        for d in (1, 2, 4):
            sel = row >= d
            a_s = jnp.where(sel, pltpu.roll(a, d, 0), 1.0)
            u_s = jnp.where(sel, pltpu.roll(u, d, 0), 0.0)
            u = a * u_s + u
            a = a * a_s
        hh = a * h + u
        gt = gate_ref[pl.ds(r0, SUBLANES), :]
        o_ref[pl.ds(r0, SUBLANES), :] = (hh * jax.nn.gelu(gt, approximate=True)).astype(o_ref.dtype)
        return hh[SUBLANES - 1:SUBLANES, :]

    h_ref[...] = lax.fori_loop(0, ts // SUBLANES, tile, h_ref[...])


def _lru(pa, cw, cb, wg, bg, lam, batch, seq, ts=256):
    t = pa.shape[0]
    w = HALF
    ns = seq // ts
    return pl.pallas_call(
        _lru_kernel,
        grid=(batch, ns),
        in_specs=[
            pl.BlockSpec((ts, w), lambda b, s: (b * ns + s, 0)),
            pl.BlockSpec((ts, w), lambda b, s: (b * ns + s, 1)),
            pl.BlockSpec((CONV_WIDTH, w), lambda b, s: (0, 0)),
            pl.BlockSpec((1, w), lambda b, s: (0, 0)),
            pl.BlockSpec((w, 2 * w), lambda b, s: (0, 0)),
            pl.BlockSpec((1, 2 * w), lambda b, s: (0, 0)),
            pl.BlockSpec((1, w), lambda b, s: (0, 0)),
        ],
        out_specs=pl.BlockSpec((ts, w), lambda b, s: (b * ns + s, 0)),
        out_shape=jax.ShapeDtypeStruct((t, w), BF16),
        scratch_shapes=[
            pltpu.VMEM((ts + SUBLANES, w), F32),
            pltpu.VMEM((ts, w), F32),
            pltpu.VMEM((ts, w), F32),
            pltpu.VMEM((1, w), F32),
        ],
        compiler_params=_cparams(("parallel", "arbitrary")),
        name="rg_lru",
    )(pa, pa, cw, cb, wg, bg, lam)


def _block_diag(wb):
    n, bd, _ = wb.shape
    eye = jnp.eye(n, dtype=wb.dtype)
    return (eye[:, None, :, None] * wb[:, :, None, :]).reshape(n * bd, n * bd)


def kernel(x, ev_w_in, ev_fox_bf, hgrn_lb, ev_hgrn_norm_g, ev_w_out, od_w_in, od_conv_w, od_conv_b,
           od_gate_a_w, od_gate_a_b, od_gate_x_w, od_gate_x_b, od_lru_lambda, od_w_out, ffn_w13,
           ffn_w2, ln_g, ln_b):
    batch, seq, d = x.shape
    t = batch * seq
    xf = x.reshape(t, d)
    lb_all = jnp.cumsum(jax.nn.softmax(hgrn_lb.astype(F32), axis=0), axis=0)
    h = HALF

    w = ev_w_in[0]
    nfox = ev_fox_bf.shape[1]
    w_fb = jnp.pad(w[:, 7 * h:7 * h + nfox], ((0, 0), (0, LANES - nfox)))
    wa = jnp.concatenate([w[:, 0:2 * h], w[:, 3 * h:4 * h], w_fb], axis=1).astype(BF16)
    wb = jnp.concatenate([w[:, 2 * h:3 * h], w[:, 4 * h:7 * h]], axis=1).astype(BF16)
    pa, pb = _proj(xf, wa, wb)
    ya = _hgrn(pa, pb, lb_all[0].reshape(1, h), ev_hgrn_norm_g[0].reshape(1, HGRN_DIM), batch, seq)
    bf_pad = jnp.broadcast_to(ev_fox_bf[0].astype(F32).reshape(nfox, 1), (nfox, LANES))
    cum = _fox_gate(pa, 3 * h // LANES, bf_pad, batch, seq)
    yb = _fox(pb, h // LANES, cum.reshape(batch, nfox // 2, 2, seq), batch, seq)
    x1 = _outproj(ya, yb, xf, ev_w_out[0].astype(BF16), ln_g[0, 0].reshape(1, d), ln_b[0, 0].reshape(1, d))
    x2 = _ffn(x1, *_ffn_weights(ffn_w13[0], ffn_w2[0]),
              ln_g[0, 1].reshape(1, d), ln_b[0, 1].reshape(1, d))

    w = od_w_in[0]
    wa = w[:, 0:2 * h].astype(BF16)
    wb = w[:, 2 * h:5 * h].astype(BF16)
    pa, pb = _proj(x2, wa, wb)
    wg = jnp.concatenate([_block_diag(od_gate_a_w[0]), _block_diag(od_gate_x_w[0])], axis=1).astype(BF16)
    bg = jnp.concatenate([od_gate_a_b[0], od_gate_x_b[0]]).reshape(1, 2 * h).astype(F32)
    yc = _lru(pa, od_conv_w[0], od_conv_b[0].reshape(1, h), wg, bg, od_lru_lambda[0].reshape(1, h),
              batch, seq)
    yd = _sb(pb, 0, batch, seq)
    x3 = _outproj(yc, yd, x2, od_w_out[0].astype(BF16), ln_g[1, 0].reshape(1, d), ln_b[1, 0].reshape(1, d))
    x4 = _ffn(x3, *_ffn_weights(ffn_w13[1], ffn_w2[1]),
              ln_g[1, 1].reshape(1, d), ln_b[1, 1].reshape(1, d))
    return x4.reshape(batch, seq, d)
```

```python
import functools

import jax
import jax.numpy as jnp
from jax import lax
from jax.experimental import pallas as pl
from jax.experimental.pallas import tpu as pltpu

F32 = jnp.float32
BF16 = jnp.bfloat16

SUBLANES = 8
LANES = 128

D_MODEL = 1024
HALF = D_MODEL // 2
HEAD_DIM = 64
HGRN_DIM = 128
LRU_BLOCKS = 8
LRU_C = 8.0
CONV_WIDTH = 4
DEPTH = 2
ALPHA = (2 * DEPTH) ** 0.25
EPS = 1e-5
D_FF = 2816
LOG2_E = 1.4426950408889634

HGRN_CHUNK = 128
HGRN_UNROLL = 4
ATT_BLOCK = 256
ATT_UNROLL = 4
VMEM_LIMIT = 56 * 1024 * 1024


def _cparams(sem):
    return pltpu.CompilerParams(dimension_semantics=sem, vmem_limit_bytes=VMEM_LIMIT)


def _proj_kernel(cols_a, cols_b, x_ref, w_ref, oa_ref, ob_ref, wa_ref, wb_ref):
    @pl.when(pl.program_id(0) == 0)
    def _():
        for dst_ref, cols in ((wa_ref, cols_a), (wb_ref, cols_b)):
            for dst, src, n in cols:
                if n % LANES:
                    slot = -(-n // LANES) * LANES
                    dst_ref[:, dst:dst + slot] = jnp.zeros((dst_ref.shape[0], slot), BF16)
                dst_ref[:, dst:dst + n] = w_ref[:, src:src + n].astype(BF16)

    xb = x_ref[...].astype(BF16)
    oa_ref[...] = jnp.dot(xb, wa_ref[...], preferred_element_type=F32)
    ob_ref[...] = jnp.dot(xb, wb_ref[...], preferred_element_type=F32).astype(BF16)


def _proj(x, w, cols_a, cols_b, tm=512):
    t, d = x.shape

    def width(cols):
        return max(dst + -(-n // LANES) * LANES for dst, _, n in cols)

    na, nb = width(cols_a), width(cols_b)
    return pl.pallas_call(
        functools.partial(_proj_kernel, cols_a, cols_b),
        grid=(t // tm,),
        in_specs=[
            pl.BlockSpec((tm, d), lambda i: (i, 0)),
            pl.BlockSpec(w.shape, lambda i: (0, 0), pipeline_mode=pl.Buffered(1)),
        ],
        out_specs=[
            pl.BlockSpec((tm, na), lambda i: (i, 0)),
            pl.BlockSpec((tm, nb), lambda i: (i, 0)),
        ],
        out_shape=[jax.ShapeDtypeStruct((t, na), F32), jax.ShapeDtypeStruct((t, nb), BF16)],
        scratch_shapes=[pltpu.VMEM((d, na), BF16), pltpu.VMEM((d, nb), BF16)],
        compiler_params=_cparams(("arbitrary",)),
        name="in_proj",
    )(x, w)


def _lockstep(gens):
    results = [None] * len(gens)
    live = list(range(len(gens)))
    while live:
        for k in list(live):
            try:
                next(gens[k])
            except StopIteration as done:
                results[k] = done.value
                live.remove(k)
    return results


def _layer_norm_rows(y, g, b):
    mu = jnp.mean(y, axis=-1, keepdims=True)
    yc = y - mu
    var = jnp.mean(yc * yc, axis=-1, keepdims=True)
    return yc * lax.rsqrt(var + EPS) * g + b


def _layer_tail_kernel(ya_ref, yb_ref, x_ref, wo_ref, g1_ref, b1_ref, w13_ref, w2_ref, g2_ref, b2_ref,
                       o_ref, xb_ref, h_ref):
    nf, tf, _ = w2_ref.shape
    mix = jnp.dot(ya_ref[...], wo_ref[0:HALF, :], preferred_element_type=F32)
    mix = mix + jnp.dot(yb_ref[...], wo_ref[HALF:D_MODEL, :], preferred_element_type=F32)
    x1 = _layer_norm_rows(ALPHA * x_ref[...] + mix, g1_ref[...], b1_ref[...])
    o_ref[...] = x1
    xb_ref[...] = x1.astype(BF16)

    def hidden(c):
        xb = xb_ref[...]
        ca = pl.multiple_of(c * tf, tf)
        cb = pl.multiple_of((nf + c) * tf, tf)
        a = jnp.dot(xb, w13_ref[:, pl.ds(ca, tf)], preferred_element_type=F32)
        b = jnp.dot(xb, w13_ref[:, pl.ds(cb, tf)], preferred_element_type=F32)
        h_ref[c] = (a * jax.nn.sigmoid(a) * b).astype(BF16)

    hidden(0)

    def pair(n, _):
        hidden(2 * n + 1)
        hidden(2 * n + 2)
        return 0

    lax.fori_loop(0, (nf - 1) // 2, pair, 0)

    acc = jnp.dot(h_ref[0], w2_ref[0], preferred_element_type=F32)
    for c in range(1, nf):
        acc = acc + jnp.dot(h_ref[c], w2_ref[c], preferred_element_type=F32)
    y = ALPHA * o_ref[...] + acc
    o_ref[...] = _layer_norm_rows(y, g2_ref[...], b2_ref[...])


def _layer_tail(ya, yb, x, w_out, ln_g, ln_b, w13, w2, tm=1024, tf=256):
    t = x.shape[0]
    nf = D_FF // tf
    assert nf % 2 == 1 and nf * tf == D_FF
    d = D_MODEL
    resident = pl.Buffered(1)
    row = lambda i: (i, 0)
    fixed = lambda i: (0, 0)
    vec = pl.BlockSpec((1, d), fixed)
    return pl.pallas_call(
        _layer_tail_kernel,
        grid=(t // tm,),
        in_specs=[
            pl.BlockSpec((tm, HALF), row),
            pl.BlockSpec((tm, HALF), row),
            pl.BlockSpec((tm, d), row),
            pl.BlockSpec((d, d), fixed, pipeline_mode=resident),
            vec, vec,
            pl.BlockSpec((d, 2 * D_FF), fixed, pipeline_mode=resident),
            pl.BlockSpec((nf, tf, d), lambda i: (0, 0, 0), pipeline_mode=resident),
            vec, vec,
        ],
        out_specs=pl.BlockSpec((tm, d), row),
        out_shape=jax.ShapeDtypeStruct((t, d), F32),
        scratch_shapes=[pltpu.VMEM((tm, d), BF16), pltpu.VMEM((nf, tm, tf), BF16)],
        compiler_params=_cparams(("parallel",)),
        name="layer_tail",
    )(ya, yb, x, w_out.astype(BF16), ln_g[0:1], ln_b[0:1],
      w13.astype(BF16), w2.astype(BF16).reshape(nf, tf, d), ln_g[1:2], ln_b[1:2])


def _tile_rows(n):
    return lax.broadcasted_iota(jnp.int32, (SUBLANES, n), 0)


def _bcast_row(tile, r):
    return jnp.broadcast_to(tile[r:r + 1, :], tile.shape)


def _segmented_scans(x):
    n = len(x)
    row = _tile_rows(x[0].shape[1])
    zero = jnp.zeros_like(x[0])
    r3 = row & 3
    up4 = (row & 4) != 0

    p = {1: list(x)}
    p[2] = [t + jnp.where((row & 1) == 1, pltpu.roll(t, 1, 0), 0.0) for t in x]
    p[4] = [t + jnp.where(r3 == 2, pltpu.roll(t, 1, 0),
                          jnp.where(r3 == 3, pltpu.roll(t, 2, 0), 0.0)) for t in p[2]]
    p[8] = [t + jnp.where(up4, _bcast_row(t, 3), 0.0) for t in p[4]]

    e = {1: [zero] * n}
    e[2] = [jnp.where((row & 1) == 0, pltpu.roll(t, SUBLANES - 1, 0), 0.0) for t in x]
    r2 = [a + b for a, b in zip(e[2], x)]
    e[4] = [t + jnp.where(r3 == 1, pltpu.roll(r, SUBLANES - 1, 0),
                          jnp.where(r3 == 0, pltpu.roll(r, SUBLANES - 2, 0), 0.0))
            for t, r in zip(e[2], r2)]
    r4 = [a + b for a, b in zip(e[4], x)]
    e[8] = [t + jnp.where(up4, 0.0, _bcast_row(r, 4)) for t, r in zip(e[4], r4)]

    m = SUBLANES
    while m < SUBLANES * n:
        nt = m // SUBLANES
        pn, en = [], []
        for g in range(n // (2 * nt)):
            lo = slice(2 * nt * g, 2 * nt * g + nt)
            hi = slice(2 * nt * g + nt, 2 * nt * (g + 1))
            tot = _bcast_row(p[m][lo][-1], SUBLANES - 1)
            pn += p[m][lo] + [t + tot for t in p[m][hi]]
            first = e[m][hi][0][0:1, :] + x[hi][0][0:1, :]
            tot_hi = jnp.broadcast_to(first, zero.shape)
            en += [t + tot_hi for t in e[m][lo]] + e[m][hi]
        p[2 * m], e[2 * m] = pn, en
        m *= 2
    return p, e


def _level_map(c):
    t = lax.broadcasted_iota(jnp.int32, (c, c), 0)
    s = lax.broadcasted_iota(jnp.int32, (c, c), 1)
    x = t ^ s
    lv = jnp.full((c, c), -1, jnp.int32)
    m = 1
    while m < c:
        lv = lv + (x >= m).astype(jnp.int32)
        m *= 2
    return jnp.where(t < s, -2, lv)


def _hgrn_kernel(q_ref, f_ref, g_ref, v_ref, lb_ref, ng_ref, o_ref, state_ref):
    c = HGRN_CHUNK
    n_tiles = c // SUBLANES
    n_chunks = q_ref.shape[0] // c
    lb = lb_ref[...]
    one_m_lb = 1.0 - lb
    ng = ng_ref[...]
    lv = _level_map(c)
    state_ref[...] = jnp.zeros_like(state_ref)

    def chunk(ci, state):
        r0 = pl.multiple_of(ci * c, c)
        z = f_ref[pl.ds(r0, c), :]
        q = q_ref[pl.ds(r0, c), :]
        v = v_ref[pl.ds(r0, c), :]
        gg = g_ref[pl.ds(r0, c), :]
        ez = jnp.exp(-jnp.abs(z))
        rz = 1.0 / (1.0 + ez)
        pos = z >= 0.0
        sig = jnp.where(pos, rz, ez * rz)
        nsig = jnp.where(pos, ez * rz, rz)
        lf = jnp.log2(lb + one_m_lb * sig)
        kk = one_m_lb * nsig

        lf_tiles = [lf[SUBLANES * i:SUBLANES * (i + 1), :] for i in range(n_tiles)]
        p, e = _segmented_scans(lf_tiles)

        def cat(ts):
            return jnp.concatenate(ts, axis=0)

        nt_dims = (((1,), (1,)), ((), ()))
        qb, kb = q.astype(BF16), kk.astype(BF16)
        yield
        scores = jnp.where(
            lv == -1, lax.dot_general(qb, kb, nt_dims, preferred_element_type=F32), 0.0)
        m, idx = 1, 0
        while m < c:
            qm = (q * jnp.exp2(cat(p[m]))).astype(BF16)
            km = kb if m == 1 else (kk * jnp.exp2(cat(e[m]))).astype(BF16)
            yield
            sm = lax.dot_general(qm, km, nt_dims, preferred_element_type=F32)
            scores = jnp.where(lv == idx, sm, scores)
            m *= 2
            idx += 1

        b = cat(p[c])
        qc = (q * jnp.exp2(b)).astype(BF16)
        kc = (kk * jnp.exp2(cat(e[c]))).astype(BF16)
        decay = jnp.exp2(b[c - 1:c, :])
        sb = scores.astype(BF16)
        yield
        kv = lax.dot_general(kc, v, (((0,), (0,)), ((), ())), preferred_element_type=F32)
        o = jnp.dot(sb, v, preferred_element_type=F32)
        yield
        st = state[0]
        o = o + jnp.dot(qc, st.astype(BF16), preferred_element_type=F32)
        state[0] = st * jnp.broadcast_to(decay, st.shape).T + kv
        yield
        ms = jnp.mean(o * o, axis=-1, keepdims=True)
        on = o * lax.rsqrt(ms + EPS) * ng
        o_ref[pl.ds(r0, c), :] = (on * (gg * jax.nn.sigmoid(gg))).astype(o_ref.dtype)

    def body(it, _):
        state = [state_ref[...]]
        _lockstep([chunk(it * HGRN_UNROLL + k, state) for k in range(HGRN_UNROLL)])
        state_ref[...] = state[0]
        return 0

    lax.fori_loop(0, n_chunks // HGRN_UNROLL, body, 0)


def _hgrn(pa, pb, lb, ng, batch, seq):
    t = pa.shape[0]
    nh = HALF // HGRN_DIM
    return pl.pallas_call(
        _hgrn_kernel,
        grid=(batch, nh),
        in_specs=[
            pl.BlockSpec((seq, HGRN_DIM), lambda b, h: (b, h)),
            pl.BlockSpec((seq, HGRN_DIM), lambda b, h: (b, nh + h)),
            pl.BlockSpec((seq, HGRN_DIM), lambda b, h: (b, 2 * nh + h)),
            pl.BlockSpec((seq, HGRN_DIM), lambda b, h: (b, h)),
            pl.BlockSpec((1, HGRN_DIM), lambda b, h: (0, h)),
            pl.BlockSpec((1, HGRN_DIM), lambda b, h: (0, 0)),
        ],
        out_specs=pl.BlockSpec((seq, HGRN_DIM), lambda b, h: (b, h)),
        out_shape=jax.ShapeDtypeStruct((t, HALF), BF16),
        scratch_shapes=[pltpu.VMEM((HGRN_DIM, HGRN_DIM), F32)],
        compiler_params=_cparams(("parallel", "parallel")),
        name="hgrn2",
    )(pa, pa, pa, pb, lb, ng)


def _fox_gate_kernel(f_ref, bf_ref, c_ref):
    s = f_ref.shape[0]
    ft = f_ref[...].T[0:SUBLANES, :]
    x = ft + bf_ref[...][:, 0:1]
    lf = jnp.minimum(x, 0.0) - jnp.log1p(jnp.exp(-jnp.abs(x)))
    lane = lax.broadcasted_iota(jnp.int32, lf.shape, 1)
    d = 1
    while d < s:
        lf = lf + jnp.where(lane >= d, pltpu.roll(lf, d, 1), 0.0)
        d *= 2
    c_ref[0] = lf


def _fox_gate(pa, col_block, bf_pad, batch, seq):
    return pl.pallas_call(
        _fox_gate_kernel,
        grid=(batch,),
        in_specs=[
            pl.BlockSpec((seq, LANES), lambda b: (b, col_block)),
            pl.BlockSpec((SUBLANES, LANES), lambda b: (0, 0)),
        ],
        out_specs=pl.BlockSpec((1, SUBLANES, seq), lambda b: (b, 0, 0)),
        out_shape=jax.ShapeDtypeStruct((batch, SUBLANES, seq), F32),
        compiler_params=_cparams(("parallel",)),
        name="fox_gate",
    )(pa, bf_pad)


def _head_lane_masks(shape):
    lane = lax.broadcasted_iota(jnp.int32, shape, len(shape) - 1)
    first = lane < HEAD_DIM
    return first, jnp.logical_not(first)


def _causal_tile_order(nq):
    u = ATT_UNROLL
    assert nq % u == 0
    pairs = [(i, i) for i in range(nq)]
    left = {i: i for i in range(nq)}
    while any(left.values()):
        pick = sorted((i for i in left if left[i]), key=lambda i: -left[i])[:u]
        assert len(pick) == u, "tile list does not split into groups of distinct query blocks"
        for i in pick:
            pairs.append((i, left[i] - 1))
            left[i] -= 1
    qi = jnp.asarray([p[0] for p in pairs], jnp.int32)
    kj = jnp.asarray([p[1] for p in pairs], jnp.int32)
    return qi, kj


def _split_heads(q_ref, v_ref, qh_ref, vh_ref):
    scale = HEAD_DIM ** -0.5
    t = ATT_BLOCK
    h0, h1 = _head_lane_masks((t, LANES))
    zero = jnp.zeros((t, LANES), BF16)
    for i in range(q_ref.shape[0] // t):
        q = q_ref[i * t:(i + 1) * t, :] * jnp.asarray(scale, BF16)
        v = v_ref[i * t:(i + 1) * t, :]
        qh_ref[i, 0:t, :] = jnp.where(h0, q, zero)
        qh_ref[i, t:2 * t, :] = jnp.where(h1, q, zero)
        vh_ref[i, 0:t, :] = jnp.where(h0, v, zero)
        vh_ref[i, t:2 * t, :] = jnp.where(h1, v, zero)


def _tile_loops(n_diag, n_tiles, load_fn, compute_fn, store_fn):
    u = ATT_UNROLL

    def body(it, diag):
        loaded = [load_fn(it * u + k, diag) for k in range(u)]
        results = _lockstep([compute_fn(x, diag) for x in loaded])
        for x, r in zip(loaded, results):
            store_fn(x, r)
        return 0

    lax.fori_loop(0, n_diag // u, lambda it, _: body(it, True), 0)
    lax.fori_loop(n_diag // u, n_tiles // u, lambda it, _: body(it, False), 0)


def _fox_kernel(qi_ref, kj_ref, q_ref, k_ref, v_ref, c_ref, o_ref,
                qh_ref, vh_ref, m_ref, l_ref, acc_ref):
    tq = tk = ATT_BLOCK
    nq = q_ref.shape[0] // tq
    nt_dims = (((1,), (1,)), ((), ()))
    o0m, _ = _head_lane_masks((tq, LANES))

    _split_heads(q_ref, v_ref, qh_ref, vh_ref)

    def load(n, diag):
        i = qi_ref[n]
        j = kj_ref[n]
        k0 = pl.multiple_of(j * tk, tk)
        state = None if diag else (m_ref[i], l_ref[i], acc_ref[i])
        return i, qh_ref[i], k_ref[pl.ds(k0, tk), :], vh_ref[j], c_ref[0, 0, :, pl.ds(k0, tk)], state

    def compute(loaded, diag):
        _, qq, k, vv, cs, state = loaded
        s = lax.dot_general(qq, k, nt_dims, preferred_element_type=F32)
        yield
        s = jnp.concatenate([s[0:tq] - cs[0:1], s[tq:2 * tq] - cs[1:2]], axis=0)
        if diag:
            qpos = lax.broadcasted_iota(jnp.int32, s.shape, 0) & (tq - 1)
            kpos = lax.broadcasted_iota(jnp.int32, s.shape, 1)
            s = jnp.where(kpos <= qpos, s, -jnp.inf)
        row_max = jnp.max(s, axis=-1, keepdims=True)
        if diag:
            m_new = jnp.broadcast_to(row_max, (2 * tq, LANES))
        else:
            m_old, l_old, acc_old = state
            m_new = jnp.maximum(m_old, row_max)
            alpha = jnp.exp(m_old - m_new)
        p = jnp.exp(s - jnp.concatenate([m_new] * (tk // LANES), axis=1))
        psum = p[:, 0:LANES]
        for c in range(1, tk // LANES):
            psum = psum + p[:, c * LANES:(c + 1) * LANES]
        pb = p.astype(BF16)
        yield
        pv = jnp.dot(jnp.concatenate([pb[0:tq], pb[tq:2 * tq]], axis=1), vv,
                     preferred_element_type=F32)
        yield
        if diag:
            return m_new, psum, pv
        return (m_new, alpha * l_old + psum,
                acc_old * jnp.where(o0m, alpha[0:tq], alpha[tq:2 * tq]) + pv)

    def store(loaded, result):
        i = loaded[0]
        m_ref[i], l_ref[i], acc_ref[i] = result

    _tile_loops(nq, qi_ref.shape[0], load, compute, store)

    for i in range(nq):
        l0 = jnp.sum(l_ref[i, 0:tq], axis=-1, keepdims=True)
        l1 = jnp.sum(l_ref[i, tq:2 * tq], axis=-1, keepdims=True)
        o_ref[i * tq:(i + 1) * tq, :] = (acc_ref[i] / jnp.where(o0m, l0, l1)).astype(o_ref.dtype)


def _att_scratch(nq):
    return [
        pltpu.VMEM((nq, 2 * ATT_BLOCK, LANES), BF16),
        pltpu.VMEM((nq, 2 * ATT_BLOCK, LANES), BF16),
    ]


def _fox(pb, col0, cpair, batch, seq):
    t = pb.shape[0]
    npair = HALF // LANES
    nq = seq // ATT_BLOCK
    qi, kj = _causal_tile_order(nq)
    grid_spec = pltpu.PrefetchScalarGridSpec(
        num_scalar_prefetch=2,
        grid=(batch, npair),
        in_specs=[
            pl.BlockSpec((seq, LANES), lambda b, p, qi, kj: (b, col0 + p)),
            pl.BlockSpec((seq, LANES), lambda b, p, qi, kj: (b, col0 + npair + p)),
            pl.BlockSpec((seq, LANES), lambda b, p, qi, kj: (b, col0 + 2 * npair + p)),
            pl.BlockSpec((1, 1, 2, seq), lambda b, p, qi, kj: (b, p, 0, 0)),
        ],
        out_specs=pl.BlockSpec((seq, LANES), lambda b, p, qi, kj: (b, p)),
        scratch_shapes=_att_scratch(nq) + [
            pltpu.VMEM((nq, 2 * ATT_BLOCK, LANES), F32),
            pltpu.VMEM((nq, 2 * ATT_BLOCK, LANES), F32),
            pltpu.VMEM((nq, ATT_BLOCK, LANES), F32),
        ],
    )
    return pl.pallas_call(
        _fox_kernel,
        grid_spec=grid_spec,
        out_shape=jax.ShapeDtypeStruct((t, HALF), BF16),
        compiler_params=_cparams(("parallel", "parallel")),
        name="fox_attention",
    )(qi, kj, pb, pb, pb, cpair)


def _suffix_matrix(tk):
    r = lax.broadcasted_iota(jnp.int32, (2 * tk, 2 * tk), 0) % tk
    cidx = lax.broadcasted_iota(jnp.int32, (2 * tk, 2 * tk), 1)
    keep = jnp.logical_or(cidx >= tk, r > cidx)
    return jnp.where(keep, 1.0, 0.0).astype(BF16)


def _sb_kernel(qi_ref, kj_ref, q_ref, k_ref, v_ref, o_ref, qh_ref, vh_ref, run_ref, acc_ref):
    tq = tk = ATT_BLOCK
    nq = q_ref.shape[0] // tq
    nsub = tk // LANES
    nt_dims = (((1,), (1,)), ((), ()))
    umat = _suffix_matrix(LANES)

    _split_heads(q_ref, v_ref, qh_ref, vh_ref)

    def load(n, diag):
        i = qi_ref[n]
        j = kj_ref[n]
        k0 = pl.multiple_of(j * tk, tk)
        state = None if diag else (run_ref[i], acc_ref[i])
        return i, qh_ref[i], k_ref[pl.ds(k0, tk), :], vh_ref[j], state

    def compute(loaded, diag):
        _, qq, k, vv, state = loaded
        z = lax.dot_general(qq, k, nt_dims, preferred_element_type=F32)
        yield
        logb = jnp.minimum(z, 0.0) - jnp.log(1.0 + jnp.exp2(jnp.abs(z) * -LOG2_E))
        lom = logb - z
        if diag:
            qpos = lax.broadcasted_iota(jnp.int32, z.shape, 0) & (tq - 1)
            kpos = lax.broadcasted_iota(jnp.int32, z.shape, 1)
            mask = kpos < qpos
            lom = jnp.where(mask, lom, 0.0)
        parts = []
        for c in reversed(range(nsub)):
            part = lom[:, c * LANES:(c + 1) * LANES]
            hi = part.astype(BF16)
            lo = (part - hi.astype(F32)).astype(BF16)
            parts.append(jnp.concatenate([hi, lo], axis=1))
        yield
        r = jnp.dot(jnp.concatenate(parts, axis=0), umat, preferred_element_type=F32)
        yield
        carry = jnp.zeros((2 * tq, LANES), F32) if diag else state[0]
        suffix = [None] * nsub
        for n_c, c in enumerate(reversed(range(nsub))):
            rc = r[n_c * 2 * tq:(n_c + 1) * 2 * tq]
            suffix[c] = rc[:, 0:LANES] + carry
            carry = carry + rc[:, LANES:2 * LANES]
        w = jnp.exp(logb + jnp.concatenate(suffix, axis=1))
        if diag:
            w = jnp.where(mask, w, 0.0)
        wb = w.astype(BF16)
        yield
        wv = jnp.dot(jnp.concatenate([wb[0:tq], wb[tq:2 * tq]], axis=1), vv,
                     preferred_element_type=F32)
        yield
        return carry, (wv if diag else state[1] + wv)

    def store(loaded, result):
        i = loaded[0]
        run_ref[i], acc_ref[i] = result

    _tile_loops(nq, qi_ref.shape[0], load, compute, store)

    for i in range(nq):
        o_ref[i * tq:(i + 1) * tq, :] = acc_ref[i].astype(o_ref.dtype)


def _sb(pb, col0, batch, seq):
    t = pb.shape[0]
    npair = HALF // LANES
    nq = seq // ATT_BLOCK
    qi, kj = _causal_tile_order(nq)
    grid_spec = pltpu.PrefetchScalarGridSpec(
        num_scalar_prefetch=2,
        grid=(batch, npair),
        in_specs=[
            pl.BlockSpec((seq, LANES), lambda b, p, qi, kj: (b, col0 + p)),
            pl.BlockSpec((seq, LANES), lambda b, p, qi, kj: (b, col0 + npair + p)),
            pl.BlockSpec((seq, LANES), lambda b, p, qi, kj: (b, col0 + 2 * npair + p)),
        ],
        out_specs=pl.BlockSpec((seq, LANES), lambda b, p, qi, kj: (b, p)),
        scratch_shapes=_att_scratch(nq) + [
            pltpu.VMEM((nq, 2 * ATT_BLOCK, LANES), F32),
            pltpu.VMEM((nq, ATT_BLOCK, LANES), F32),
        ],
    )
    return pl.pallas_call(
        _sb_kernel,
        grid_spec=grid_spec,
        out_shape=jax.ShapeDtypeStruct((t, HALF), BF16),
        compiler_params=_cparams(("parallel", "parallel")),
        name="sb_attention",
    )(qi, kj, pb, pb, pb)


def _lru_kernel(x_ref, gate_ref, cw_ref, cb_ref, wg_ref, bg_ref, lam_ref, o_ref,
                xe_ref, a_ref, u_ref, h_ref):
    sblk = pl.program_id(1)
    ts, w = x_ref.shape
    pad = SUBLANES

    @pl.when(sblk == 0)
    def _():
        xe_ref[0:pad, :] = jnp.zeros((pad, w), F32)
        h_ref[...] = jnp.zeros_like(h_ref)

    @pl.when(sblk > 0)
    def _():
        xe_ref[0:pad, :] = xe_ref[ts:ts + pad, :]

    x = x_ref[...]
    xe_ref[pad:pad + ts, :] = x
    xc = cb_ref[...] + cw_ref[CONV_WIDTH - 1:CONV_WIDTH, :] * x
    for d in range(1, CONV_WIDTH):
        xc = xc + cw_ref[CONV_WIDTH - 1 - d:CONV_WIDTH - d, :] * xe_ref[pad - d:pad - d + ts, :]

    gates = jnp.dot(xc.astype(BF16), wg_ref[...], preferred_element_type=F32) + bg_ref[...]
    r = jax.nn.sigmoid(gates[:, 0:w])
    ig = jax.nn.sigmoid(gates[:, w:2 * w])
    lam = lam_ref[...]
    softplus_neg_lam = jnp.maximum(-lam, 0.0) + jnp.log1p(jnp.exp(-jnp.abs(lam)))
    log_a = -LRU_C * r * softplus_neg_lam
    a_ref[...] = jnp.exp(log_a)
    th = jnp.tanh(-log_a)
    u_ref[...] = jnp.sqrt(2.0 * th / (1.0 + th)) * (ig * xc)

    row = _tile_rows(w)

    def tile(ti, h):
        r0 = pl.multiple_of(ti * SUBLANES, SUBLANES)
        a = a_ref[pl.ds(r0, SUBLANES), :]
        u = u_ref[pl.ds(r0, SUBLANES), :]
        for d in (1, 2, 4):
            sel = row >= d
            a_s = jnp.where(sel, pltpu.roll(a, d, 0), 1.0)
            u_s = jnp.where(sel, pltpu.roll(u, d, 0), 0.0)
            u = a * u_s + u
            a = a * a_s
        hh = a * h + u
        gt = gate_ref[pl.ds(r0, SUBLANES), :]
        o_ref[pl.ds(r0, SUBLANES), :] = (hh * jax.nn.gelu(gt, approximate=True)).astype(o_ref.dtype)
        return hh[SUBLANES - 1:SUBLANES, :]

    h_ref[...] = lax.fori_loop(0, ts // SUBLANES, tile, h_ref[...])


def _lru(pa, cw, cb, wg, bg, lam, batch, seq, ts=256):
    t = pa.shape[0]
    w = HALF
    ns = seq // ts
    return pl.pallas_call(
        _lru_kernel,
        grid=(batch, ns),
        in_specs=[
            pl.BlockSpec((ts, w), lambda b, s: (b * ns + s, 0)),
            pl.BlockSpec((ts, w), lambda b, s: (b * ns + s, 1)),
            pl.BlockSpec((CONV_WIDTH, w), lambda b, s: (0, 0)),
            pl.BlockSpec((1, w), lambda b, s: (0, 0)),
            pl.BlockSpec((w, 2 * w), lambda b, s: (0, 0)),
            pl.BlockSpec((1, 2 * w), lambda b, s: (0, 0)),
            pl.BlockSpec((1, w), lambda b, s: (0, 0)),
        ],
        out_specs=pl.BlockSpec((ts, w), lambda b, s: (b * ns + s, 0)),
        out_shape=jax.ShapeDtypeStruct((t, w), BF16),
        scratch_shapes=[
            pltpu.VMEM((ts + SUBLANES, w), F32),
            pltpu.VMEM((ts, w), F32),
            pltpu.VMEM((ts, w), F32),
            pltpu.VMEM((1, w), F32),
        ],
        compiler_params=_cparams(("parallel", "arbitrary")),
        name="rg_lru",
    )(pa, pa, cw, cb, wg, bg, lam)


def _block_diag(wb):
    n, bd, _ = wb.shape
    eye = jnp.eye(n, dtype=wb.dtype)
    return (eye[:, None, :, None] * wb[:, :, None, :]).reshape(n * bd, n * bd)


def kernel(x, ev_w_in, ev_fox_bf, hgrn_lb, ev_hgrn_norm_g, ev_w_out, od_w_in, od_conv_w, od_conv_b,
           od_gate_a_w, od_gate_a_b, od_gate_x_w, od_gate_x_b, od_lru_lambda, od_w_out, ffn_w13,
           ffn_w2, ln_g, ln_b):
    batch, seq, d = x.shape
    t = batch * seq
    xf = x.reshape(t, d)
    lb_all = jnp.cumsum(jax.nn.softmax(hgrn_lb.astype(F32), axis=0), axis=0)
    h = HALF

    nfox = ev_fox_bf.shape[1]
    cols_a = [(0, 0, 2 * h), (2 * h, 3 * h, h), (3 * h, 7 * h, nfox)]
    cols_b = [(0, 2 * h, h), (h, 4 * h, 3 * h)]
    pa, pb = _proj(xf, ev_w_in[0], cols_a, cols_b)
    ya = _hgrn(pa, pb, lb_all[0].reshape(1, h), ev_hgrn_norm_g[0].reshape(1, HGRN_DIM), batch, seq)
    bf_pad = jnp.broadcast_to(ev_fox_bf[0].astype(F32).reshape(nfox, 1), (nfox, LANES))
    cum = _fox_gate(pa, 3 * h // LANES, bf_pad, batch, seq)
    yb = _fox(pb, h // LANES, cum.reshape(batch, nfox // 2, 2, seq), batch, seq)
    x2 = _layer_tail(ya, yb, xf, ev_w_out[0], ln_g[0], ln_b[0], ffn_w13[0], ffn_w2[0])

    pa, pb = _proj(x2, od_w_in[0], [(0, 0, 2 * h)], [(0, 2 * h, 3 * h)])
    wg = jnp.concatenate([_block_diag(od_gate_a_w[0]), _block_diag(od_gate_x_w[0])], axis=1).astype(BF16)
    bg = jnp.concatenate([od_gate_a_b[0], od_gate_x_b[0]]).reshape(1, 2 * h).astype(F32)
    yc = _lru(pa, od_conv_w[0], od_conv_b[0].reshape(1, h), wg, bg, od_lru_lambda[0].reshape(1, h),
              batch, seq)
    yd = _sb(pb, 0, batch, seq)
    x4 = _layer_tail(yc, yd, x2, od_w_out[0], ln_g[1], ln_b[1], ffn_w13[1], ffn_w2[1])
    return x4.reshape(batch, seq, d)
```

```python
import functools

import jax
import jax.numpy as jnp
from jax import lax
from jax.experimental import pallas as pl
from jax.experimental.pallas import tpu as pltpu

F32 = jnp.float32
BF16 = jnp.bfloat16

SUBLANES = 8
LANES = 128

D_MODEL = 1024
HALF = D_MODEL // 2
HEAD_DIM = 64
HGRN_DIM = 128
LRU_BLOCKS = 8
LRU_C = 8.0
CONV_WIDTH = 4
DEPTH = 2
ALPHA = (2 * DEPTH) ** 0.25
EPS = 1e-5
D_FF = 2816
LOG2_E = 1.4426950408889634

HGRN_CHUNK = 128
HGRN_UNROLL = 4
ATT_BLOCK = 256
ATT_UNROLL = 4
VMEM_LIMIT = 56 * 1024 * 1024


def _cparams(sem):
    return pltpu.CompilerParams(dimension_semantics=sem, vmem_limit_bytes=VMEM_LIMIT)


def _proj_kernel(cols_a, cols_b, x_ref, w_ref, oa_ref, ob_ref, wa_ref, wb_ref):
    @pl.when(pl.program_id(0) == 0)
    def _():
        for dst_ref, cols in ((wa_ref, cols_a), (wb_ref, cols_b)):
            for dst, src, n in cols:
                if n % LANES:
                    slot = -(-n // LANES) * LANES
                    dst_ref[:, dst:dst + slot] = jnp.zeros((dst_ref.shape[0], slot), BF16)
                dst_ref[:, dst:dst + n] = w_ref[:, src:src + n].astype(BF16)

    xb = x_ref[...].astype(BF16)
    oa_ref[...] = jnp.dot(xb, wa_ref[...], preferred_element_type=F32)
    ob_ref[...] = jnp.dot(xb, wb_ref[...], preferred_element_type=F32).astype(BF16)


def _proj(x, w, cols_a, cols_b, tm=512):
    t, d = x.shape

    def width(cols):
        return max(dst + -(-n // LANES) * LANES for dst, _, n in cols)

    na, nb = width(cols_a), width(cols_b)
    return pl.pallas_call(
        functools.partial(_proj_kernel, cols_a, cols_b),
        grid=(t // tm,),
        in_specs=[
            pl.BlockSpec((tm, d), lambda i: (i, 0)),
            pl.BlockSpec(w.shape, lambda i: (0, 0), pipeline_mode=pl.Buffered(1)),
        ],
        out_specs=[
            pl.BlockSpec((tm, na), lambda i: (i, 0)),
            pl.BlockSpec((tm, nb), lambda i: (i, 0)),
        ],
        out_shape=[jax.ShapeDtypeStruct((t, na), F32), jax.ShapeDtypeStruct((t, nb), BF16)],
        scratch_shapes=[pltpu.VMEM((d, na), BF16), pltpu.VMEM((d, nb), BF16)],
        compiler_params=_cparams(("arbitrary",)),
        name="in_proj",
    )(x, w)


def _lockstep(gens):
    results = [None] * len(gens)
    live = list(range(len(gens)))
    while live:
        for k in list(live):
            try:
                next(gens[k])
            except StopIteration as done:
                results[k] = done.value
                live.remove(k)
    return results


def _layer_norm_rows(y, g, b):
    mu = jnp.mean(y, axis=-1, keepdims=True)
    yc = y - mu
    var = jnp.mean(yc * yc, axis=-1, keepdims=True)
    return yc * lax.rsqrt(var + EPS) * g + b


def _layer_tail_kernel(ya_ref, yb_ref, x_ref, wo_ref, g1_ref, b1_ref, w13_ref, w2_ref, g2_ref, b2_ref,
                       o_ref, xb_ref, h_ref):
    nf, tf, _ = w2_ref.shape
    mix = jnp.dot(ya_ref[...], wo_ref[0:HALF, :], preferred_element_type=F32)
    mix = mix + jnp.dot(yb_ref[...], wo_ref[HALF:D_MODEL, :], preferred_element_type=F32)
    x1 = _layer_norm_rows(ALPHA * x_ref[...] + mix, g1_ref[...], b1_ref[...])
    o_ref[...] = x1
    xb_ref[...] = x1.astype(BF16)

    def hidden(c):
        xb = xb_ref[...]
        ca = pl.multiple_of(c * tf, tf)
        cb = pl.multiple_of((nf + c) * tf, tf)
        a = jnp.dot(xb, w13_ref[:, pl.ds(ca, tf)], preferred_element_type=F32)
        b = jnp.dot(xb, w13_ref[:, pl.ds(cb, tf)], preferred_element_type=F32)
        h_ref[c] = (a * jax.nn.sigmoid(a) * b).astype(BF16)

    hidden(0)

    def pair(n, _):
        hidden(2 * n + 1)
        hidden(2 * n + 2)
        return 0

    lax.fori_loop(0, (nf - 1) // 2, pair, 0)

    acc = jnp.dot(h_ref[0], w2_ref[0], preferred_element_type=F32)
    for c in range(1, nf):
        acc = acc + jnp.dot(h_ref[c], w2_ref[c], preferred_element_type=F32)
    y = ALPHA * o_ref[...] + acc
    o_ref[...] = _layer_norm_rows(y, g2_ref[...], b2_ref[...])


def _layer_tail(ya, yb, x, w_out, ln_g, ln_b, w13, w2, layer, slot, tm=1024, tf=256):
    t = x.shape[0]
    nf = D_FF // tf
    assert nf % 2 == 1 and nf * tf == D_FF
    d = D_MODEL
    resident = pl.Buffered(1)
    row = lambda i: (i, 0)
    return pl.pallas_call(
        _layer_tail_kernel,
        grid=(t // tm,),
        in_specs=[
            pl.BlockSpec((tm, HALF), row),
            pl.BlockSpec((tm, HALF), row),
            pl.BlockSpec((tm, d), row),
            pl.BlockSpec((None, d, d), lambda i: (slot, 0, 0), pipeline_mode=resident),
            pl.BlockSpec((None, 1, d), lambda i: (2 * layer, 0, 0)),
            pl.BlockSpec((None, 1, d), lambda i: (2 * layer, 0, 0)),
            pl.BlockSpec((None, d, 2 * D_FF), lambda i: (layer, 0, 0), pipeline_mode=resident),
            pl.BlockSpec((None, nf, tf, d), lambda i: (layer, 0, 0, 0), pipeline_mode=resident),
            pl.BlockSpec((None, 1, d), lambda i: (2 * layer + 1, 0, 0)),
            pl.BlockSpec((None, 1, d), lambda i: (2 * layer + 1, 0, 0)),
        ],
        out_specs=pl.BlockSpec((tm, d), row),
        out_shape=jax.ShapeDtypeStruct((t, d), F32),
        scratch_shapes=[pltpu.VMEM((tm, d), BF16), pltpu.VMEM((nf, tm, tf), BF16)],
        compiler_params=_cparams(("parallel",)),
        name="layer_tail",
    )(ya, yb, x, w_out, ln_g, ln_b, w13, w2.reshape(w2.shape[0], nf, tf, d), ln_g, ln_b)


def _tile_rows(n):
    return lax.broadcasted_iota(jnp.int32, (SUBLANES, n), 0)


def _bcast_row(tile, r):
    return jnp.broadcast_to(tile[r:r + 1, :], tile.shape)


def _segmented_scans(x):
    n = len(x)
    row = _tile_rows(x[0].shape[1])
    zero = jnp.zeros_like(x[0])
    r3 = row & 3
    up4 = (row & 4) != 0

    p = {1: list(x)}
    p[2] = [t + jnp.where((row & 1) == 1, pltpu.roll(t, 1, 0), 0.0) for t in x]
    p[4] = [t + jnp.where(r3 == 2, pltpu.roll(t, 1, 0),
                          jnp.where(r3 == 3, pltpu.roll(t, 2, 0), 0.0)) for t in p[2]]
    p[8] = [t + jnp.where(up4, _bcast_row(t, 3), 0.0) for t in p[4]]

    e = {1: [zero] * n}
    e[2] = [jnp.where((row & 1) == 0, pltpu.roll(t, SUBLANES - 1, 0), 0.0) for t in x]
    r2 = [a + b for a, b in zip(e[2], x)]
    e[4] = [t + jnp.where(r3 == 1, pltpu.roll(r, SUBLANES - 1, 0),
                          jnp.where(r3 == 0, pltpu.roll(r, SUBLANES - 2, 0), 0.0))
            for t, r in zip(e[2], r2)]
    r4 = [a + b for a, b in zip(e[4], x)]
    e[8] = [t + jnp.where(up4, 0.0, _bcast_row(r, 4)) for t, r in zip(e[4], r4)]

    m = SUBLANES
    while m < SUBLANES * n:
        nt = m // SUBLANES
        pn, en = [], []
        for g in range(n // (2 * nt)):
            lo = slice(2 * nt * g, 2 * nt * g + nt)
            hi = slice(2 * nt * g + nt, 2 * nt * (g + 1))
            tot = _bcast_row(p[m][lo][-1], SUBLANES - 1)
            pn += p[m][lo] + [t + tot for t in p[m][hi]]
            first = e[m][hi][0][0:1, :] + x[hi][0][0:1, :]
            tot_hi = jnp.broadcast_to(first, zero.shape)
            en += [t + tot_hi for t in e[m][lo]] + e[m][hi]
        p[2 * m], e[2 * m] = pn, en
        m *= 2
    return p, e


def _level_map(c):
    t = lax.broadcasted_iota(jnp.int32, (c, c), 0)
    s = lax.broadcasted_iota(jnp.int32, (c, c), 1)
    x = t ^ s
    lv = jnp.full((c, c), -1, jnp.int32)
    m = 1
    while m < c:
        lv = lv + (x >= m).astype(jnp.int32)
        m *= 2
    return jnp.where(t < s, -2, lv)


def _hgrn_kernel(q_ref, f_ref, g_ref, v_ref, lb_ref, ng_ref, o_ref, state_ref):
    c = HGRN_CHUNK
    n_tiles = c // SUBLANES
    n_chunks = q_ref.shape[0] // c
    lb = lb_ref[...]
    one_m_lb = 1.0 - lb
    ng = ng_ref[...]
    lv = _level_map(c)
    state_ref[...] = jnp.zeros_like(state_ref)

    def chunk(ci, state):
        r0 = pl.multiple_of(ci * c, c)
        z = f_ref[pl.ds(r0, c), :]
        q = q_ref[pl.ds(r0, c), :]
        v = v_ref[pl.ds(r0, c), :]
        gg = g_ref[pl.ds(r0, c), :]
        ez = jnp.exp(-jnp.abs(z))
        rz = 1.0 / (1.0 + ez)
        pos = z >= 0.0
        sig = jnp.where(pos, rz, ez * rz)
        nsig = jnp.where(pos, ez * rz, rz)
        lf = jnp.log2(lb + one_m_lb * sig)
        kk = one_m_lb * nsig

        lf_tiles = [lf[SUBLANES * i:SUBLANES * (i + 1), :] for i in range(n_tiles)]
        p, e = _segmented_scans(lf_tiles)

        def cat(ts):
            return jnp.concatenate(ts, axis=0)

        nt_dims = (((1,), (1,)), ((), ()))
        qb, kb = q.astype(BF16), kk.astype(BF16)
        yield
        scores = jnp.where(
            lv == -1, lax.dot_general(qb, kb, nt_dims, preferred_element_type=F32), 0.0)
        m, idx = 1, 0
        while m < c:
            qm = (q * jnp.exp2(cat(p[m]))).astype(BF16)
            km = kb if m == 1 else (kk * jnp.exp2(cat(e[m]))).astype(BF16)
            yield
            sm = lax.dot_general(qm, km, nt_dims, preferred_element_type=F32)
            scores = jnp.where(lv == idx, sm, scores)
            m *= 2
            idx += 1

        b = cat(p[c])
        qc = (q * jnp.exp2(b)).astype(BF16)
        kc = (kk * jnp.exp2(cat(e[c]))).astype(BF16)
        decay = jnp.exp2(b[c - 1:c, :])
        sb = scores.astype(BF16)
        yield
        kv = lax.dot_general(kc, v, (((0,), (0,)), ((), ())), preferred_element_type=F32)
        o = jnp.dot(sb, v, preferred_element_type=F32)
        yield
        st = state[0]
        o = o + jnp.dot(qc, st.astype(BF16), preferred_element_type=F32)
        state[0] = st * jnp.broadcast_to(decay, st.shape).T + kv
        yield
        ms = jnp.mean(o * o, axis=-1, keepdims=True)
        on = o * lax.rsqrt(ms + EPS) * ng
        o_ref[pl.ds(r0, c), :] = (on * (gg * jax.nn.sigmoid(gg))).astype(o_ref.dtype)

    def body(it, _):
        state = [state_ref[...]]
        _lockstep([chunk(it * HGRN_UNROLL + k, state) for k in range(HGRN_UNROLL)])
        state_ref[...] = state[0]
        return 0

    lax.fori_loop(0, n_chunks // HGRN_UNROLL, body, 0)


def _hgrn(pa, pb, lb, ng, batch, seq):
    t = pa.shape[0]
    nh = HALF // HGRN_DIM
    return pl.pallas_call(
        _hgrn_kernel,
        grid=(batch, nh),
        in_specs=[
            pl.BlockSpec((seq, HGRN_DIM), lambda b, h: (b, h)),
            pl.BlockSpec((seq, HGRN_DIM), lambda b, h: (b, nh + h)),
            pl.BlockSpec((seq, HGRN_DIM), lambda b, h: (b, 2 * nh + h)),
            pl.BlockSpec((seq, HGRN_DIM), lambda b, h: (b, h)),
            pl.BlockSpec((1, HGRN_DIM), lambda b, h: (0, h)),
            pl.BlockSpec((1, HGRN_DIM), lambda b, h: (0, 0)),
        ],
        out_specs=pl.BlockSpec((seq, HGRN_DIM), lambda b, h: (b, h)),
        out_shape=jax.ShapeDtypeStruct((t, HALF), BF16),
        scratch_shapes=[pltpu.VMEM((HGRN_DIM, HGRN_DIM), F32)],
        compiler_params=_cparams(("parallel", "parallel")),
        name="hgrn2",
    )(pa, pa, pa, pb, lb, ng)


def _fox_gate_kernel(f_ref, bf_ref, c_ref):
    s = f_ref.shape[0]
    ft = f_ref[...].T[0:SUBLANES, :]
    x = ft + bf_ref[...][:, 0:1]
    lf = jnp.minimum(x, 0.0) - jnp.log1p(jnp.exp(-jnp.abs(x)))
    lane = lax.broadcasted_iota(jnp.int32, lf.shape, 1)
    d = 1
    while d < s:
        lf = lf + jnp.where(lane >= d, pltpu.roll(lf, d, 1), 0.0)
        d *= 2
    c_ref[0] = lf


def _fox_gate(pa, col_block, bf_pad, batch, seq):
    return pl.pallas_call(
        _fox_gate_kernel,
        grid=(batch,),
        in_specs=[
            pl.BlockSpec((seq, LANES), lambda b: (b, col_block)),
            pl.BlockSpec((SUBLANES, LANES), lambda b: (0, 0)),
        ],
        out_specs=pl.BlockSpec((1, SUBLANES, seq), lambda b: (b, 0, 0)),
        out_shape=jax.ShapeDtypeStruct((batch, SUBLANES, seq), F32),
        compiler_params=_cparams(("parallel",)),
        name="fox_gate",
    )(pa, bf_pad)


def _head_lane_masks(shape):
    lane = lax.broadcasted_iota(jnp.int32, shape, len(shape) - 1)
    first = lane < HEAD_DIM
    return first, jnp.logical_not(first)


def _causal_tile_order(nq):
    u = ATT_UNROLL
    assert nq % u == 0
    pairs = [(i, i) for i in range(nq)]
    left = {i: i for i in range(nq)}
    while any(left.values()):
        pick = sorted((i for i in left if left[i]), key=lambda i: -left[i])[:u]
        assert len(pick) == u, "tile list does not split into groups of distinct query blocks"
        for i in pick:
            pairs.append((i, left[i] - 1))
            left[i] -= 1
    return [p[0] for p in pairs], [p[1] for p in pairs]


def _split_heads(q_ref, v_ref, qh_ref, vh_ref):
    scale = HEAD_DIM ** -0.5
    t = ATT_BLOCK
    h0, h1 = _head_lane_masks((t, LANES))
    zero = jnp.zeros((t, LANES), BF16)
    for i in range(q_ref.shape[0] // t):
        q = q_ref[i * t:(i + 1) * t, :] * jnp.asarray(scale, BF16)
        v = v_ref[i * t:(i + 1) * t, :]
        qh_ref[i, 0:t, :] = jnp.where(h0, q, zero)
        qh_ref[i, t:2 * t, :] = jnp.where(h1, q, zero)
        vh_ref[i, 0:t, :] = jnp.where(h0, v, zero)
        vh_ref[i, t:2 * t, :] = jnp.where(h1, v, zero)


def _tile_waves(nq, tile_fn):
    qi, kj = _causal_tile_order(nq)
    state = [None] * nq
    for w0 in range(0, len(qi), ATT_UNROLL):
        wave = list(zip(qi[w0:w0 + ATT_UNROLL], kj[w0:w0 + ATT_UNROLL]))
        results = _lockstep([tile_fn(i, j, state[i], i == j) for i, j in wave])
        for (i, _), r in zip(wave, results):
            state[i] = r
    return state


def _foxt_kernel(q_ref, k_ref, v_ref, c_ref, o_ref, qt_ref, vt_ref, cb_ref):
    tq = tk = ATT_BLOCK
    seq = q_ref.shape[0]
    nq = seq // tq
    scale = HEAD_DIM ** -0.5
    lane0, lane1 = _head_lane_masks((tq, LANES))
    dim_head0 = lax.broadcasted_iota(jnp.int32, (LANES, tq), 0) < HEAD_DIM

    for i in range(nq):
        q = (q_ref[i * tq:(i + 1) * tq, :] * jnp.asarray(scale, BF16)).astype(F32)
        v = v_ref[i * tq:(i + 1) * tq, :].astype(F32)
        for h, lanes in enumerate((lane0, lane1)):
            qt_ref[i, :, h * tq:(h + 1) * tq] = jnp.where(lanes, q, 0.0).T.astype(BF16)
            vt_ref[i, :, h * tq:(h + 1) * tq] = jnp.where(lanes, v, 0.0).T.astype(BF16)
    for h in range(2):
        cb_ref[h] = jnp.broadcast_to(c_ref[0, 0, h:h + 1, :], (LANES, seq)).T

    def tile(i, j, state, diag):
        keys = slice(j * tk, (j + 1) * tk)
        s = jnp.dot(k_ref[keys, :], qt_ref[i], preferred_element_type=F32)
        yield
        wide = tq // LANES
        s = jnp.concatenate([s[:, 0:tq] - jnp.concatenate([cb_ref[0, keys, :]] * wide, axis=1),
                             s[:, tq:2 * tq] - jnp.concatenate([cb_ref[1, keys, :]] * wide, axis=1)],
                            axis=1)
        if diag:
            kpos = lax.broadcasted_iota(jnp.int32, s.shape, 0)
            qpos = lax.broadcasted_iota(jnp.int32, s.shape, 1) & (tq - 1)
            s = jnp.where(kpos <= qpos, s, -jnp.inf)
        col_max = jnp.max(s, axis=0, keepdims=True)
        if diag:
            m_new = col_max
        else:
            m_old, l_old, acc_old = state
            m_new = jnp.maximum(m_old, col_max)
            alpha = jnp.exp(m_old - m_new)
        p = jnp.exp(s - m_new)
        psum = jnp.sum(p, axis=0, keepdims=True)
        pb = p.astype(BF16)
        yield
        pv = jnp.dot(vt_ref[j], jnp.concatenate([pb[:, 0:tq], pb[:, tq:2 * tq]], axis=0),
                     preferred_element_type=F32)
        yield
        if diag:
            return m_new, psum, pv
        a_rows = jnp.where(dim_head0, alpha[:, 0:tq], alpha[:, tq:2 * tq])
        return m_new, alpha * l_old + psum, acc_old * a_rows + pv

    for i, (_, l, acc) in enumerate(_tile_waves(nq, tile)):
        l_rows = jnp.where(dim_head0, l[:, 0:tq], l[:, tq:2 * tq])
        o_ref[i * tq:(i + 1) * tq, :] = (acc / l_rows).T.astype(o_ref.dtype)


def _foxt(pb, col0, cpair, batch, seq):
    t = pb.shape[0]
    npair = HALF // LANES
    nq = seq // ATT_BLOCK
    return pl.pallas_call(
        _foxt_kernel,
        grid=(batch, npair),
        in_specs=[
            pl.BlockSpec((seq, LANES), lambda b, p: (b, col0 + p)),
            pl.BlockSpec((seq, LANES), lambda b, p: (b, col0 + npair + p)),
            pl.BlockSpec((seq, LANES), lambda b, p: (b, col0 + 2 * npair + p)),
            pl.BlockSpec((1, 1, 2, seq), lambda b, p: (b, p, 0, 0)),
        ],
        out_specs=pl.BlockSpec((seq, LANES), lambda b, p: (b, p)),
        out_shape=jax.ShapeDtypeStruct((t, HALF), BF16),
        scratch_shapes=[
            pltpu.VMEM((nq, LANES, 2 * ATT_BLOCK), BF16),
            pltpu.VMEM((nq, LANES, 2 * ATT_BLOCK), BF16),
            pltpu.VMEM((2, seq, LANES), F32),
        ],
        compiler_params=_cparams(("parallel", "parallel")),
        name="fox_attention",
    )(pb, pb, pb, cpair)


def _suffix_matrix(tk):
    r = lax.broadcasted_iota(jnp.int32, (2 * tk, 2 * tk), 0) % tk
    cidx = lax.broadcasted_iota(jnp.int32, (2 * tk, 2 * tk), 1)
    keep = jnp.logical_or(cidx >= tk, r > cidx)
    return jnp.where(keep, 1.0, 0.0).astype(BF16)


def _sb_kernel(q_ref, k_ref, v_ref, o_ref, qh_ref, vh_ref):
    tq = tk = ATT_BLOCK
    nq = q_ref.shape[0] // tq
    nsub = tk // LANES
    nt_dims = (((1,), (1,)), ((), ()))
    umat = _suffix_matrix(LANES)

    _split_heads(q_ref, v_ref, qh_ref, vh_ref)

    def tile(i, j, state, diag):
        z = lax.dot_general(qh_ref[i], k_ref[j * tk:(j + 1) * tk, :], nt_dims,
                            preferred_element_type=F32)
        yield
        logb = jnp.minimum(z, 0.0) - jnp.log(1.0 + jnp.exp2(jnp.abs(z) * -LOG2_E))
        lom = logb - z
        if diag:
            qpos = lax.broadcasted_iota(jnp.int32, z.shape, 0) & (tq - 1)
            kpos = lax.broadcasted_iota(jnp.int32, z.shape, 1)
            mask = kpos < qpos
            lom = jnp.where(mask, lom, 0.0)
        parts = []
        for c in reversed(range(nsub)):
            part = lom[:, c * LANES:(c + 1) * LANES]
            hi = part.astype(BF16)
            lo = (part - hi.astype(F32)).astype(BF16)
            parts.append(jnp.concatenate([hi, lo], axis=1))
        yield
        r = jnp.dot(jnp.concatenate(parts, axis=0), umat, preferred_element_type=F32)
        yield
        carry = jnp.zeros((2 * tq, LANES), F32) if diag else state[0]
        suffix = [None] * nsub
        for n_c, c in enumerate(reversed(range(nsub))):
            rc = r[n_c * 2 * tq:(n_c + 1) * 2 * tq]
            suffix[c] = rc[:, 0:LANES] + carry
            carry = carry + rc[:, LANES:2 * LANES]
        w = jnp.exp(logb + jnp.concatenate(suffix, axis=1))
        if diag:
            w = jnp.where(mask, w, 0.0)
        wb = w.astype(BF16)
        yield
        wv = jnp.dot(jnp.concatenate([wb[0:tq], wb[tq:2 * tq]], axis=1), vh_ref[j],
                     preferred_element_type=F32)
        yield
        return carry, (wv if diag else state[1] + wv)

    for i, (_, acc) in enumerate(_tile_waves(nq, tile)):
        o_ref[i * tq:(i + 1) * tq, :] = acc.astype(o_ref.dtype)


def _sb(pb, col0, batch, seq):
    t = pb.shape[0]
    npair = HALF // LANES
    nq = seq // ATT_BLOCK
    return pl.pallas_call(
        _sb_kernel,
        grid=(batch, npair),
        in_specs=[
            pl.BlockSpec((seq, LANES), lambda b, p: (b, col0 + p)),
            pl.BlockSpec((seq, LANES), lambda b, p: (b, col0 + npair + p)),
            pl.BlockSpec((seq, LANES), lambda b, p: (b, col0 + 2 * npair + p)),
        ],
        out_specs=pl.BlockSpec((seq, LANES), lambda b, p: (b, p)),
        out_shape=jax.ShapeDtypeStruct((t, HALF), BF16),
        scratch_shapes=[
            pltpu.VMEM((nq, 2 * ATT_BLOCK, LANES), BF16),
            pltpu.VMEM((nq, 2 * ATT_BLOCK, LANES), BF16),
        ],
        compiler_params=_cparams(("parallel", "parallel")),
        name="sb_attention",
    )(pb, pb, pb)


def _lru_kernel(x_ref, gate_ref, cw_ref, cb_ref, wg_ref, bg_ref, lam_ref, o_ref,
                xe_ref, a_ref, u_ref, h_ref):
    sblk = pl.program_id(1)
    ts, w = x_ref.shape
    pad = SUBLANES

    @pl.when(sblk == 0)
    def _():
        xe_ref[0:pad, :] = jnp.zeros((pad, w), F32)
        h_ref[...] = jnp.zeros_like(h_ref)

    @pl.when(sblk > 0)
    def _():
        xe_ref[0:pad, :] = xe_ref[ts:ts + pad, :]

    x = x_ref[...]
    xe_ref[pad:pad + ts, :] = x
    xc = cb_ref[...] + cw_ref[CONV_WIDTH - 1:CONV_WIDTH, :] * x
    for d in range(1, CONV_WIDTH):
        xc = xc + cw_ref[CONV_WIDTH - 1 - d:CONV_WIDTH - d, :] * xe_ref[pad - d:pad - d + ts, :]

    gates = jnp.dot(xc.astype(BF16), wg_ref[...], preferred_element_type=F32) + bg_ref[...]
    r = jax.nn.sigmoid(gates[:, 0:w])
    ig = jax.nn.sigmoid(gates[:, w:2 * w])
    lam = lam_ref[...]
    softplus_neg_lam = jnp.maximum(-lam, 0.0) + jnp.log1p(jnp.exp(-jnp.abs(lam)))
    log_a = -LRU_C * r * softplus_neg_lam
    a_ref[...] = jnp.exp(log_a)
    th = jnp.tanh(-log_a)
    u_ref[...] = jnp.sqrt(2.0 * th / (1.0 + th)) * (ig * xc)

    row = _tile_rows(w)

    def tile(ti, h):
        r0 = pl.multiple_of(ti * SUBLANES, SUBLANES)
        a = a_ref[pl.ds(r0, SUBLANES), :]
        u = u_ref[pl.ds(r0, SUBLANES), :]
        for d in (1, 2, 4):
            sel = row >= d
            a_s = jnp.where(sel, pltpu.roll(a, d, 0), 1.0)
            u_s = jnp.where(sel, pltpu.roll(u, d, 0), 0.0)
            u = a * u_s + u
            a = a * a_s
        hh = a * h + u
        gt = gate_ref[pl.ds(r0, SUBLANES), :]
        o_ref[pl.ds(r0, SUBLANES), :] = (hh * jax.nn.gelu(gt, approximate=True)).astype(o_ref.dtype)
        return hh[SUBLANES - 1:SUBLANES, :]

    h_ref[...] = lax.fori_loop(0, ts // SUBLANES, tile, h_ref[...])


def _lru(pa, cw, cb, wg, bg, lam, batch, seq, ts=256):
    t = pa.shape[0]
    w = HALF
    ns = seq // ts
    return pl.pallas_call(
        _lru_kernel,
        grid=(batch, ns),
        in_specs=[
            pl.BlockSpec((ts, w), lambda b, s: (b * ns + s, 0)),
            pl.BlockSpec((ts, w), lambda b, s: (b * ns + s, 1)),
            pl.BlockSpec((CONV_WIDTH, w), lambda b, s: (0, 0)),
            pl.BlockSpec((1, w), lambda b, s: (0, 0)),
            pl.BlockSpec((w, 2 * w), lambda b, s: (0, 0)),
            pl.BlockSpec((1, 2 * w), lambda b, s: (0, 0)),
            pl.BlockSpec((1, w), lambda b, s: (0, 0)),
        ],
        out_specs=pl.BlockSpec((ts, w), lambda b, s: (b * ns + s, 0)),
        out_shape=jax.ShapeDtypeStruct((t, w), BF16),
        scratch_shapes=[
            pltpu.VMEM((ts + SUBLANES, w), F32),
            pltpu.VMEM((ts, w), F32),
            pltpu.VMEM((ts, w), F32),
            pltpu.VMEM((1, w), F32),
        ],
        compiler_params=_cparams(("parallel", "arbitrary")),
        name="rg_lru",
    )(pa, pa, cw, cb, wg, bg, lam)


def _block_diag(wb):
    n, bd, _ = wb.shape
    eye = jnp.eye(n, dtype=wb.dtype)
    return (eye[:, None, :, None] * wb[:, :, None, :]).reshape(n * bd, n * bd)


def kernel(x, ev_w_in, ev_fox_bf, hgrn_lb, ev_hgrn_norm_g, ev_w_out, od_w_in, od_conv_w, od_conv_b,
           od_gate_a_w, od_gate_a_b, od_gate_x_w, od_gate_x_b, od_lru_lambda, od_w_out, ffn_w13,
           ffn_w2, ln_g, ln_b):
    batch, seq, d = x.shape
    t = batch * seq
    xf = x.reshape(t, d)
    lb_all = jnp.cumsum(jax.nn.softmax(hgrn_lb.astype(F32), axis=0), axis=0)
    h = HALF

    nfox = ev_fox_bf.shape[1]
    cols_a = [(0, 0, 2 * h), (2 * h, 3 * h, h), (3 * h, 7 * h, nfox)]
    cols_b = [(0, 2 * h, h), (h, 4 * h, 3 * h)]
    pa, pb = _proj(xf, ev_w_in[0], cols_a, cols_b)
    ya = _hgrn(pa, pb, lb_all[0].reshape(1, h), ev_hgrn_norm_g[0].reshape(1, HGRN_DIM), batch, seq)
    bf_pad = jnp.broadcast_to(ev_fox_bf[0].astype(F32).reshape(nfox, 1), (nfox, LANES))
    cum = _fox_gate(pa, 3 * h // LANES, bf_pad, batch, seq)
    yb = _foxt(pb, h // LANES, cum.reshape(batch, nfox // 2, 2, seq), batch, seq)
    lng = ln_g.astype(F32).reshape(2 * DEPTH, 1, d)
    lnb = ln_b.astype(F32).reshape(2 * DEPTH, 1, d)
    w13_bf, w2_bf = ffn_w13.astype(BF16), ffn_w2.astype(BF16)
    x2 = _layer_tail(ya, yb, xf, ev_w_out.astype(BF16), lng, lnb, w13_bf, w2_bf, layer=0, slot=0)

    pa, pb = _proj(x2, od_w_in[0], [(0, 0, 2 * h)], [(0, 2 * h, 3 * h)])
    wg = jnp.concatenate([_block_diag(od_gate_a_w[0]), _block_diag(od_gate_x_w[0])], axis=1).astype(BF16)
    bg = jnp.concatenate([od_gate_a_b[0], od_gate_x_b[0]]).reshape(1, 2 * h).astype(F32)
    yc = _lru(pa, od_conv_w[0], od_conv_b[0].reshape(1, h), wg, bg, od_lru_lambda[0].reshape(1, h),
              batch, seq)
    yd = _sb(pb, 0, batch, seq)
    x4 = _layer_tail(yc, yd, x2, od_w_out.astype(BF16), lng, lnb, w13_bf, w2_bf, layer=1, slot=0)
    return x4.reshape(batch, seq, d)
```

```python
import functools

import jax
import jax.numpy as jnp
from jax import lax
from jax.experimental import pallas as pl
from jax.experimental.pallas import tpu as pltpu

F32 = jnp.float32
BF16 = jnp.bfloat16

SUBLANES = 8
LANES = 128

D_MODEL = 1024
HALF = D_MODEL // 2
HEAD_DIM = 64
HGRN_DIM = 128
LRU_BLOCKS = 8
LRU_C = 8.0
CONV_WIDTH = 4
DEPTH = 2
ALPHA = (2 * DEPTH) ** 0.25
EPS = 1e-5
D_FF = 2816
LOG2_E = 1.4426950408889634

FFN_CHUNK = 256
HGRN_CHUNK = 128
HGRN_UNROLL = 4
ATT_BLOCK = 256
ATT_UNROLL = 4
VMEM_LIMIT = 56 * 1024 * 1024


def _cparams(sem):
    return pltpu.CompilerParams(dimension_semantics=sem, vmem_limit_bytes=VMEM_LIMIT)


def _proj_kernel(cols_a, cols_b, x_ref, w_ref, oa_ref, ob_ref, wa_ref, wb_ref):
    @pl.when(pl.program_id(0) == 0)
    def _():
        for dst_ref, cols in ((wa_ref, cols_a), (wb_ref, cols_b)):
            for dst, src, n in cols:
                if n % LANES:
                    slot = -(-n // LANES) * LANES
                    dst_ref[:, dst:dst + slot] = jnp.zeros((dst_ref.shape[0], slot), BF16)
                dst_ref[:, dst:dst + n] = w_ref[:, src:src + n].astype(BF16)

    xb = x_ref[...].astype(BF16)
    oa_ref[...] = jnp.dot(xb, wa_ref[...], preferred_element_type=F32)
    ob_ref[...] = jnp.dot(xb, wb_ref[...], preferred_element_type=F32).astype(BF16)


def _time_major_spec(tm, n, batch, seq):
    per_seq = seq // tm
    return pl.BlockSpec((tm, n), lambda i: (i % per_seq, i // per_seq))


def _proj(x, w, cols_a, cols_b, a_time_major=None, tm=512):
    t, d = x.shape

    def width(cols):
        return max(dst + -(-n // LANES) * LANES for dst, _, n in cols)

    na, nb = width(cols_a), width(cols_b)
    if a_time_major is None:
        a_spec, a_shape = pl.BlockSpec((tm, na), lambda i: (i, 0)), (t, na)
    else:
        batch, seq = a_time_major
        a_spec, a_shape = _time_major_spec(tm, na, batch, seq), (seq, batch * na)
    oa, ob = pl.pallas_call(
        functools.partial(_proj_kernel, cols_a, cols_b),
        grid=(t // tm,),
        in_specs=[
            pl.BlockSpec((tm, d), lambda i: (i, 0)),
            pl.BlockSpec(w.shape, lambda i: (0, 0), pipeline_mode=pl.Buffered(1)),
        ],
        out_specs=[a_spec, pl.BlockSpec((tm, nb), lambda i: (i, 0))],
        out_shape=[jax.ShapeDtypeStruct(a_shape, F32), jax.ShapeDtypeStruct((t, nb), BF16)],
        scratch_shapes=[pltpu.VMEM((d, na), BF16), pltpu.VMEM((d, nb), BF16)],
        compiler_params=_cparams(("arbitrary",)),
        name="in_proj",
    )(x, w)
    return oa.reshape(t, na), ob


def _lockstep(gens):
    results = [None] * len(gens)
    live = list(range(len(gens)))
    while live:
        for k in list(live):
            try:
                next(gens[k])
            except StopIteration as done:
                results[k] = done.value
                live.remove(k)
    return results


def _layer_norm_rows(y, g, b):
    mu = jnp.mean(y, axis=-1, keepdims=True)
    yc = y - mu
    var = jnp.mean(yc * yc, axis=-1, keepdims=True)
    return yc * lax.rsqrt(var + EPS) * g + b


def _layer_tail_kernel(ya_ref, yb_ref, x_ref, wo_ref, g1_ref, b1_ref, w13_ref, w2_ref, g2_ref, b2_ref,
                       o_ref, xb_ref, h_ref):
    tf = FFN_CHUNK
    nf = w2_ref.shape[0] // tf
    mix = jnp.dot(ya_ref[...], wo_ref[0:HALF, :], preferred_element_type=F32)
    mix = mix + jnp.dot(yb_ref[...], wo_ref[HALF:D_MODEL, :], preferred_element_type=F32)
    x1 = _layer_norm_rows(ALPHA * x_ref[...] + mix, g1_ref[...], b1_ref[...])
    o_ref[...] = x1
    xb_ref[...] = x1.astype(BF16)

    def hidden(c):
        xb = xb_ref[...]
        ca = pl.multiple_of(c * tf, tf)
        cb = pl.multiple_of((nf + c) * tf, tf)
        a = jnp.dot(xb, w13_ref[:, pl.ds(ca, tf)], preferred_element_type=F32)
        b = jnp.dot(xb, w13_ref[:, pl.ds(cb, tf)], preferred_element_type=F32)
        h_ref[:, pl.ds(ca, tf)] = (a * jax.nn.sigmoid(a) * b).astype(BF16)

    hidden(0)

    def pair(n, _):
        hidden(2 * n + 1)
        hidden(2 * n + 2)
        return 0

    lax.fori_loop(0, (nf - 1) // 2, pair, 0)

    for n0 in range(0, D_MODEL, tf):
        cols = slice(n0, n0 + tf)
        down = jnp.dot(h_ref[...], w2_ref[:, cols], preferred_element_type=F32)
        o_ref[:, cols] = ALPHA * o_ref[:, cols] + down
    o_ref[...] = _layer_norm_rows(o_ref[...], g2_ref[...], b2_ref[...])


def _layer_tail(ya, yb, x, w_out, ln_g, ln_b, w13, w2, layer, slot, ya_time_major=None, tm=1024):
    t = x.shape[0]
    nf = D_FF // FFN_CHUNK
    assert nf % 2 == 1 and nf * FFN_CHUNK == D_FF and D_MODEL % FFN_CHUNK == 0
    d = D_MODEL
    resident = pl.Buffered(1)
    row = lambda i: (i, 0)
    ya_spec = pl.BlockSpec((tm, HALF), row)
    if ya_time_major is not None:
        batch, seq = ya_time_major
        ya = ya.reshape(seq, batch * HALF)
        ya_spec = _time_major_spec(tm, HALF, batch, seq)
    return pl.pallas_call(
        _layer_tail_kernel,
        grid=(t // tm,),
        in_specs=[
            ya_spec,
            pl.BlockSpec((tm, HALF), row),
            pl.BlockSpec((tm, d), row),
            pl.BlockSpec((None, d, d), lambda i: (slot, 0, 0), pipeline_mode=resident),
            pl.BlockSpec((None, 1, d), lambda i: (2 * layer, 0, 0)),
            pl.BlockSpec((None, 1, d), lambda i: (2 * layer, 0, 0)),
            pl.BlockSpec((None, d, 2 * D_FF), lambda i: (layer, 0, 0), pipeline_mode=resident),
            pl.BlockSpec((None, D_FF, d), lambda i: (layer, 0, 0), pipeline_mode=resident),
            pl.BlockSpec((None, 1, d), lambda i: (2 * layer + 1, 0, 0)),
            pl.BlockSpec((None, 1, d), lambda i: (2 * layer + 1, 0, 0)),
        ],
        out_specs=pl.BlockSpec((tm, d), row),
        out_shape=jax.ShapeDtypeStruct((t, d), F32),
        scratch_shapes=[pltpu.VMEM((tm, d), BF16), pltpu.VMEM((tm, D_FF), BF16)],
        compiler_params=_cparams(("parallel",)),
        name="layer_tail",
    )(ya, yb, x, w_out, ln_g, ln_b, w13, w2, ln_g, ln_b)


def _tile_rows(n):
    return lax.broadcasted_iota(jnp.int32, (SUBLANES, n), 0)


def _bcast_row(tile, r):
    return jnp.broadcast_to(tile[r:r + 1, :], tile.shape)


def _segmented_scans(x):
    n = len(x)
    row = _tile_rows(x[0].shape[1])
    zero = jnp.zeros_like(x[0])
    r3 = row & 3
    up4 = (row & 4) != 0

    p = {1: list(x)}
    p[2] = [t + jnp.where((row & 1) == 1, pltpu.roll(t, 1, 0), 0.0) for t in x]
    p[4] = [t + jnp.where(r3 == 2, pltpu.roll(t, 1, 0),
                          jnp.where(r3 == 3, pltpu.roll(t, 2, 0), 0.0)) for t in p[2]]
    p[8] = [t + jnp.where(up4, _bcast_row(t, 3), 0.0) for t in p[4]]

    e = {1: [zero] * n}
    e[2] = [jnp.where((row & 1) == 0, pltpu.roll(t, SUBLANES - 1, 0), 0.0) for t in x]
    r2 = [a + b for a, b in zip(e[2], x)]
    e[4] = [t + jnp.where(r3 == 1, pltpu.roll(r, SUBLANES - 1, 0),
                          jnp.where(r3 == 0, pltpu.roll(r, SUBLANES - 2, 0), 0.0))
            for t, r in zip(e[2], r2)]
    r4 = [a + b for a, b in zip(e[4], x)]
    e[8] = [t + jnp.where(up4, 0.0, _bcast_row(r, 4)) for t, r in zip(e[4], r4)]

    m = SUBLANES
    while m < SUBLANES * n:
        nt = m // SUBLANES
        pn, en = [], []
        for g in range(n // (2 * nt)):
            lo = slice(2 * nt * g, 2 * nt * g + nt)
            hi = slice(2 * nt * g + nt, 2 * nt * (g + 1))
            tot = _bcast_row(p[m][lo][-1], SUBLANES - 1)
            pn += p[m][lo] + [t + tot for t in p[m][hi]]
            first = e[m][hi][0][0:1, :] + x[hi][0][0:1, :]
            tot_hi = jnp.broadcast_to(first, zero.shape)
            en += [t + tot_hi for t in e[m][lo]] + e[m][hi]
        p[2 * m], e[2 * m] = pn, en
        m *= 2
    return p, e


def _level_map(c):
    t = lax.broadcasted_iota(jnp.int32, (c, c), 0)
    s = lax.broadcasted_iota(jnp.int32, (c, c), 1)
    x = t ^ s
    lv = jnp.full((c, c), -1, jnp.int32)
    m = 1
    while m < c:
        lv = lv + (x >= m).astype(jnp.int32)
        m *= 2
    return jnp.where(t < s, -2, lv)


def _hgrn_kernel(q_ref, f_ref, g_ref, v_ref, lb_ref, ng_ref, o_ref):
    c = HGRN_CHUNK
    n_tiles = c // SUBLANES
    n_chunks = q_ref.shape[0] // c
    lb = lb_ref[...]
    one_m_lb = 1.0 - lb
    ng = ng_ref[...]
    lv = _level_map(c)

    def chunk(ci, state):
        rows = slice(ci * c, (ci + 1) * c)
        z = f_ref[rows, :]
        q = q_ref[rows, :]
        v = v_ref[rows, :]
        gg = g_ref[rows, :]
        ez = jnp.exp(-jnp.abs(z))
        rz = 1.0 / (1.0 + ez)
        pos = z >= 0.0
        sig = jnp.where(pos, rz, ez * rz)
        nsig = jnp.where(pos, ez * rz, rz)
        lf = jnp.log2(lb + one_m_lb * sig)
        kk = one_m_lb * nsig

        lf_tiles = [lf[SUBLANES * i:SUBLANES * (i + 1), :] for i in range(n_tiles)]
        p, e = _segmented_scans(lf_tiles)

        def cat(ts):
            return jnp.concatenate(ts, axis=0)

        nt_dims = (((1,), (1,)), ((), ()))
        qb, kb = q.astype(BF16), kk.astype(BF16)
        yield
        scores = jnp.where(
            lv == -1, lax.dot_general(qb, kb, nt_dims, preferred_element_type=F32), 0.0)
        m, idx = 1, 0
        while m < c:
            qm = (q * jnp.exp2(cat(p[m]))).astype(BF16)
            km = kb if m == 1 else (kk * jnp.exp2(cat(e[m]))).astype(BF16)
            yield
            sm = lax.dot_general(qm, km, nt_dims, preferred_element_type=F32)
            scores = jnp.where(lv == idx, sm, scores)
            m *= 2
            idx += 1

        b = cat(p[c])
        qc = (q * jnp.exp2(b)).astype(BF16)
        kc = (kk * jnp.exp2(cat(e[c]))).astype(BF16)
        decay = jnp.exp2(b[c - 1:c, :])
        sb = scores.astype(BF16)
        yield
        kv = lax.dot_general(kc, v, (((0,), (0,)), ((), ())), preferred_element_type=F32)
        o = jnp.dot(sb, v, preferred_element_type=F32)
        yield
        st = state[0]
        o = o + jnp.dot(qc, st.astype(BF16), preferred_element_type=F32)
        state[0] = st * jnp.broadcast_to(decay, st.shape).T + kv
        yield
        ms = jnp.mean(o * o, axis=-1, keepdims=True)
        on = o * lax.rsqrt(ms + EPS) * ng
        o_ref[rows, :] = (on * (gg * jax.nn.sigmoid(gg))).astype(o_ref.dtype)

    state = [jnp.zeros((HGRN_DIM, HGRN_DIM), F32)]
    for c0 in range(0, n_chunks, HGRN_UNROLL):
        _lockstep([chunk(c0 + k, state) for k in range(HGRN_UNROLL)])


def _hgrn(pa, pb, lb, ng, batch, seq):
    t = pa.shape[0]
    nh = HALF // HGRN_DIM
    return pl.pallas_call(
        _hgrn_kernel,
        grid=(batch, nh),
        in_specs=[
            pl.BlockSpec((seq, HGRN_DIM), lambda b, h: (b, h)),
            pl.BlockSpec((seq, HGRN_DIM), lambda b, h: (b, nh + h)),
            pl.BlockSpec((seq, HGRN_DIM), lambda b, h: (b, 2 * nh + h)),
            pl.BlockSpec((seq, HGRN_DIM), lambda b, h: (b, h)),
            pl.BlockSpec((1, HGRN_DIM), lambda b, h: (0, h)),
            pl.BlockSpec((1, HGRN_DIM), lambda b, h: (0, 0)),
        ],
        out_specs=pl.BlockSpec((seq, HGRN_DIM), lambda b, h: (b, h)),
        out_shape=jax.ShapeDtypeStruct((t, HALF), BF16),
        compiler_params=_cparams(("parallel", "parallel")),
        name="hgrn2",
    )(pa, pa, pa, pb, lb, ng)


def _fox_gate_kernel(f_ref, bf_ref, c_ref):
    s = f_ref.shape[0]
    ft = f_ref[...].T[0:SUBLANES, :]
    x = ft + bf_ref[...][:, 0:1]
    lf = jnp.minimum(x, 0.0) - jnp.log1p(jnp.exp(-jnp.abs(x)))
    lane = lax.broadcasted_iota(jnp.int32, lf.shape, 1)
    d = 1
    while d < s:
        lf = lf + jnp.where(lane >= d, pltpu.roll(lf, d, 1), 0.0)
        d *= 2
    c_ref[0] = lf


def _fox_gate(pa, col_block, bf_pad, batch, seq):
    return pl.pallas_call(
        _fox_gate_kernel,
        grid=(batch,),
        in_specs=[
            pl.BlockSpec((seq, LANES), lambda b: (b, col_block)),
            pl.BlockSpec((SUBLANES, LANES), lambda b: (0, 0)),
        ],
        out_specs=pl.BlockSpec((1, SUBLANES, seq), lambda b: (b, 0, 0)),
        out_shape=jax.ShapeDtypeStruct((batch, SUBLANES, seq), F32),
        compiler_params=_cparams(("parallel",)),
        name="fox_gate",
    )(pa, bf_pad)


def _head_lane_masks(shape):
    lane = lax.broadcasted_iota(jnp.int32, shape, len(shape) - 1)
    first = lane < HEAD_DIM
    return first, jnp.logical_not(first)


def _causal_tile_order(nq):
    u = ATT_UNROLL
    assert nq % u == 0
    pairs = [(i, i) for i in range(nq)]
    left = {i: i for i in range(nq)}
    while any(left.values()):
        pick = sorted((i for i in left if left[i]), key=lambda i: -left[i])[:u]
        assert len(pick) == u, "tile list does not split into groups of distinct query blocks"
        for i in pick:
            pairs.append((i, left[i] - 1))
            left[i] -= 1
    return [p[0] for p in pairs], [p[1] for p in pairs]


def _split_heads(q_ref, v_ref, qh_ref, vh_ref):
    scale = HEAD_DIM ** -0.5
    t = ATT_BLOCK
    h0, h1 = _head_lane_masks((t, LANES))
    zero = jnp.zeros((t, LANES), BF16)
    for i in range(q_ref.shape[0] // t):
        q = q_ref[i * t:(i + 1) * t, :] * jnp.asarray(scale, BF16)
        v = v_ref[i * t:(i + 1) * t, :]
        qh_ref[i, 0:t, :] = jnp.where(h0, q, zero)
        qh_ref[i, t:2 * t, :] = jnp.where(h1, q, zero)
        vh_ref[i, 0:t, :] = jnp.where(h0, v, zero)
        vh_ref[i, t:2 * t, :] = jnp.where(h1, v, zero)


def _tile_waves(nq, tile_fn):
    qi, kj = _causal_tile_order(nq)
    state = [None] * nq
    for w0 in range(0, len(qi), ATT_UNROLL):
        wave = list(zip(qi[w0:w0 + ATT_UNROLL], kj[w0:w0 + ATT_UNROLL]))
        results = _lockstep([tile_fn(i, j, state[i], i == j) for i, j in wave])
        for (i, _), r in zip(wave, results):
            state[i] = r
    return state


def _foxt_kernel(q_ref, k_ref, v_ref, c_ref, o_ref, qt_ref, vt_ref, cb_ref):
    tq = tk = ATT_BLOCK
    seq = q_ref.shape[0]
    nq = seq // tq
    scale = HEAD_DIM ** -0.5
    lane0, lane1 = _head_lane_masks((tq, LANES))
    dim_head0 = lax.broadcasted_iota(jnp.int32, (LANES, tq), 0) < HEAD_DIM

    for i in range(nq):
        q = (q_ref[i * tq:(i + 1) * tq, :] * jnp.asarray(scale, BF16)).astype(F32)
        v = v_ref[i * tq:(i + 1) * tq, :].astype(F32)
        for h, lanes in enumerate((lane0, lane1)):
            qt_ref[i, :, h * tq:(h + 1) * tq] = jnp.where(lanes, q, 0.0).T.astype(BF16)
            vt_ref[i, :, h * tq:(h + 1) * tq] = jnp.where(lanes, v, 0.0).T.astype(BF16)
    for h in range(2):
        cb_ref[h] = jnp.broadcast_to(c_ref[0, 0, h:h + 1, :], (LANES, seq)).T

    def tile(i, j, state, diag):
        keys = slice(j * tk, (j + 1) * tk)
        s = jnp.dot(k_ref[keys, :], qt_ref[i], preferred_element_type=F32)
        yield
        wide = tq // LANES
        s = jnp.concatenate([s[:, 0:tq] - jnp.concatenate([cb_ref[0, keys, :]] * wide, axis=1),
                             s[:, tq:2 * tq] - jnp.concatenate([cb_ref[1, keys, :]] * wide, axis=1)],
                            axis=1)
        if diag:
            kpos = lax.broadcasted_iota(jnp.int32, s.shape, 0)
            qpos = lax.broadcasted_iota(jnp.int32, s.shape, 1) & (tq - 1)
            s = jnp.where(kpos <= qpos, s, -jnp.inf)
        col_max = jnp.max(s, axis=0, keepdims=True)
        if diag:
            m_new = col_max
        else:
            m_old, l_old, acc_old = state
            m_new = jnp.maximum(m_old, col_max)
            alpha = jnp.exp(m_old - m_new)
        p = jnp.exp(s - m_new)
        psum = jnp.sum(p, axis=0, keepdims=True)
        pb = p.astype(BF16)
        yield
        pv = jnp.dot(vt_ref[j], jnp.concatenate([pb[:, 0:tq], pb[:, tq:2 * tq]], axis=0),
                     preferred_element_type=F32)
        yield
        if diag:
            return m_new, psum, pv
        a_rows = jnp.where(dim_head0, alpha[:, 0:tq], alpha[:, tq:2 * tq])
        return m_new, alpha * l_old + psum, acc_old * a_rows + pv

    for i, (_, l, acc) in enumerate(_tile_waves(nq, tile)):
        l_rows = jnp.where(dim_head0, l[:, 0:tq], l[:, tq:2 * tq])
        o_ref[i * tq:(i + 1) * tq, :] = (acc / l_rows).T.astype(o_ref.dtype)


def _foxt(pb, col0, cpair, batch, seq):
    t = pb.shape[0]
    npair = HALF // LANES
    nq = seq // ATT_BLOCK
    return pl.pallas_call(
        _foxt_kernel,
        grid=(batch, npair),
        in_specs=[
            pl.BlockSpec((seq, LANES), lambda b, p: (b, col0 + p)),
            pl.BlockSpec((seq, LANES), lambda b, p: (b, col0 + npair + p)),
            pl.BlockSpec((seq, LANES), lambda b, p: (b, col0 + 2 * npair + p)),
            pl.BlockSpec((1, 1, 2, seq), lambda b, p: (b, p, 0, 0)),
        ],
        out_specs=pl.BlockSpec((seq, LANES), lambda b, p: (b, p)),
        out_shape=jax.ShapeDtypeStruct((t, HALF), BF16),
        scratch_shapes=[
            pltpu.VMEM((nq, LANES, 2 * ATT_BLOCK), BF16),
            pltpu.VMEM((nq, LANES, 2 * ATT_BLOCK), BF16),
            pltpu.VMEM((2, seq, LANES), F32),
        ],
        compiler_params=_cparams(("parallel", "parallel")),
        name="fox_attention",
    )(pb, pb, pb, cpair)


def _suffix_matrix(tk):
    r = lax.broadcasted_iota(jnp.int32, (2 * tk, 2 * tk), 0) % tk
    cidx = lax.broadcasted_iota(jnp.int32, (2 * tk, 2 * tk), 1)
    keep = jnp.logical_or(cidx >= tk, r > cidx)
    return jnp.where(keep, 1.0, 0.0).astype(BF16)


def _sb_kernel(q_ref, k_ref, v_ref, o_ref, qh_ref, vh_ref):
    tq = tk = ATT_BLOCK
    nq = q_ref.shape[0] // tq
    nsub = tk // LANES
    nt_dims = (((1,), (1,)), ((), ()))
    umat = _suffix_matrix(LANES)

    _split_heads(q_ref, v_ref, qh_ref, vh_ref)

    def tile(i, j, state, diag):
        z = lax.dot_general(qh_ref[i], k_ref[j * tk:(j + 1) * tk, :], nt_dims,
                            preferred_element_type=F32)
        yield
        logb = jnp.minimum(z, 0.0) - jnp.log(1.0 + jnp.exp2(jnp.abs(z) * -LOG2_E))
        lom = logb - z
        if diag:
            qpos = lax.broadcasted_iota(jnp.int32, z.shape, 0) & (tq - 1)
            kpos = lax.broadcasted_iota(jnp.int32, z.shape, 1)
            mask = kpos < qpos
            lom = jnp.where(mask, lom, 0.0)
        parts = []
        for c in reversed(range(nsub)):
            part = lom[:, c * LANES:(c + 1) * LANES]
            hi = part.astype(BF16)
            lo = (part - hi.astype(F32)).astype(BF16)
            parts.append(jnp.concatenate([hi, lo], axis=1))
        yield
        r = jnp.dot(jnp.concatenate(parts, axis=0), umat, preferred_element_type=F32)
        yield
        carry = jnp.zeros((2 * tq, LANES), F32) if diag else state[0]
        suffix = [None] * nsub
        for n_c, c in enumerate(reversed(range(nsub))):
            rc = r[n_c * 2 * tq:(n_c + 1) * 2 * tq]
            suffix[c] = rc[:, 0:LANES] + carry
            carry = carry + rc[:, LANES:2 * LANES]
        w = jnp.exp(logb + jnp.concatenate(suffix, axis=1))
        if diag:
            w = jnp.where(mask, w, 0.0)
        wb = w.astype(BF16)
        yield
        wv = jnp.dot(jnp.concatenate([wb[0:tq], wb[tq:2 * tq]], axis=1), vh_ref[j],
                     preferred_element_type=F32)
        yield
        return carry, (wv if diag else state[1] + wv)

    for i, (_, acc) in enumerate(_tile_waves(nq, tile)):
        o_ref[i * tq:(i + 1) * tq, :] = acc.astype(o_ref.dtype)


def _sb(pb, col0, batch, seq):
    t = pb.shape[0]
    npair = HALF // LANES
    nq = seq // ATT_BLOCK
    return pl.pallas_call(
        _sb_kernel,
        grid=(batch, npair),
        in_specs=[
            pl.BlockSpec((seq, LANES), lambda b, p: (b, col0 + p)),
            pl.BlockSpec((seq, LANES), lambda b, p: (b, col0 + npair + p)),
            pl.BlockSpec((seq, LANES), lambda b, p: (b, col0 + 2 * npair + p)),
        ],
        out_specs=pl.BlockSpec((seq, LANES), lambda b, p: (b, p)),
        out_shape=jax.ShapeDtypeStruct((t, HALF), BF16),
        scratch_shapes=[
            pltpu.VMEM((nq, 2 * ATT_BLOCK, LANES), BF16),
            pltpu.VMEM((nq, 2 * ATT_BLOCK, LANES), BF16),
        ],
        compiler_params=_cparams(("parallel", "parallel")),
        name="sb_attention",
    )(pb, pb, pb)


def _lru_kernel(x_ref, gate_ref, cw_ref, cb_ref, wg_ref, bg_ref, lam_ref, o_ref,
                xe_ref, a_ref, u_ref, h_ref):
    step = pl.program_id(0)
    rows, w = x_ref.shape
    hist = (CONV_WIDTH - 1) * SUBLANES

    @pl.when(step == 0)
    def _():
        xe_ref[0:hist, :] = jnp.zeros((hist, w), F32)
        h_ref[...] = jnp.zeros_like(h_ref)

    @pl.when(step > 0)
    def _():
        xe_ref[0:hist, :] = xe_ref[rows:rows + hist, :]

    x = x_ref[...]
    xe_ref[hist:hist + rows, :] = x
    xc = cb_ref[...] + cw_ref[CONV_WIDTH - 1:CONV_WIDTH, :] * x
    for d in range(1, CONV_WIDTH):
        back = hist - d * SUBLANES
        xc = xc + cw_ref[CONV_WIDTH - 1 - d:CONV_WIDTH - d, :] * xe_ref[back:back + rows, :]

    gates = jnp.dot(xc.astype(BF16), wg_ref[...], preferred_element_type=F32) + bg_ref[...]
    r = jax.nn.sigmoid(gates[:, 0:w])
    ig = jax.nn.sigmoid(gates[:, w:2 * w])
    lam = lam_ref[...]
    softplus_neg_lam = jnp.maximum(-lam, 0.0) + jnp.log1p(jnp.exp(-jnp.abs(lam)))
    log_a = -LRU_C * r * softplus_neg_lam
    a_ref[...] = jnp.exp(log_a)
    th = jnp.tanh(-log_a)
    u_ref[...] = jnp.sqrt(2.0 * th / (1.0 + th)) * (ig * xc)

    h = h_ref[...]
    for t0 in range(0, rows, SUBLANES):
        h = a_ref[t0:t0 + SUBLANES, :] * h + u_ref[t0:t0 + SUBLANES, :]
        u_ref[t0:t0 + SUBLANES, :] = h
    h_ref[...] = h
    o_ref[...] = (u_ref[...] * jax.nn.gelu(gate_ref[...], approximate=True)).astype(o_ref.dtype)


def _lru(pa, cw, cb, wg, bg, lam, rows=512):
    t = pa.shape[0]
    w = HALF
    fixed = lambda s: (0, 0)
    return pl.pallas_call(
        _lru_kernel,
        grid=(t // rows,),
        in_specs=[
            pl.BlockSpec((rows, w), lambda s: (s, 0)),
            pl.BlockSpec((rows, w), lambda s: (s, 1)),
            pl.BlockSpec((CONV_WIDTH, w), fixed),
            pl.BlockSpec((1, w), fixed),
            pl.BlockSpec((w, 2 * w), fixed),
            pl.BlockSpec((1, 2 * w), fixed),
            pl.BlockSpec((1, w), fixed),
        ],
        out_specs=pl.BlockSpec((rows, w), lambda s: (s, 0)),
        out_shape=jax.ShapeDtypeStruct((t, w), BF16),
        scratch_shapes=[
            pltpu.VMEM((rows + (CONV_WIDTH - 1) * SUBLANES, w), F32),
            pltpu.VMEM((rows, w), F32),
            pltpu.VMEM((rows, w), F32),
            pltpu.VMEM((SUBLANES, w), F32),
        ],
        compiler_params=_cparams(("arbitrary",)),
        name="rg_lru",
    )(pa, pa, cw, cb, wg, bg, lam)


def _block_diag(wb):
    n, bd, _ = wb.shape
    eye = jnp.eye(n, dtype=wb.dtype)
    return (eye[:, None, :, None] * wb[:, :, None, :]).reshape(n * bd, n * bd)


def kernel(x, ev_w_in, ev_fox_bf, hgrn_lb, ev_hgrn_norm_g, ev_w_out, od_w_in, od_conv_w, od_conv_b,
           od_gate_a_w, od_gate_a_b, od_gate_x_w, od_gate_x_b, od_lru_lambda, od_w_out, ffn_w13,
           ffn_w2, ln_g, ln_b):
    batch, seq, d = x.shape
    t = batch * seq
    xf = x.reshape(t, d)
    lb_all = jnp.cumsum(jax.nn.softmax(hgrn_lb.astype(F32), axis=0), axis=0)
    h = HALF

    nfox = ev_fox_bf.shape[1]
    cols_a = [(0, 0, 2 * h), (2 * h, 3 * h, h), (3 * h, 7 * h, nfox)]
    cols_b = [(0, 2 * h, h), (h, 4 * h, 3 * h)]
    pa, pb = _proj(xf, ev_w_in[0], cols_a, cols_b)
    ya = _hgrn(pa, pb, lb_all[0].reshape(1, h), ev_hgrn_norm_g[0].reshape(1, HGRN_DIM), batch, seq)
    bf_pad = jnp.broadcast_to(ev_fox_bf[0].astype(F32).reshape(nfox, 1), (nfox, LANES))
    cum = _fox_gate(pa, 3 * h // LANES, bf_pad, batch, seq)
    yb = _foxt(pb, h // LANES, cum.reshape(batch, nfox // 2, 2, seq), batch, seq)
    lng = ln_g.astype(F32).reshape(2 * DEPTH, 1, d)
    lnb = ln_b.astype(F32).reshape(2 * DEPTH, 1, d)
    w13_bf, w2_bf = ffn_w13.astype(BF16), ffn_w2.astype(BF16)
    x2 = _layer_tail(ya, yb, xf, ev_w_out.astype(BF16), lng, lnb, w13_bf, w2_bf, layer=0, slot=0)

    assert batch == SUBLANES
    pa, pb = _proj(x2, od_w_in[0], [(0, 0, 2 * h)], [(0, 2 * h, 3 * h)], a_time_major=(batch, seq))
    wg = jnp.concatenate([_block_diag(od_gate_a_w[0]), _block_diag(od_gate_x_w[0])], axis=1).astype(BF16)
    bg = jnp.concatenate([od_gate_a_b[0], od_gate_x_b[0]]).reshape(1, 2 * h).astype(F32)
    yc = _lru(pa, od_conv_w[0], od_conv_b[0].reshape(1, h), wg, bg, od_lru_lambda[0].reshape(1, h))
    yd = _sb(pb, 0, batch, seq)
    x4 = _layer_tail(yc, yd, x2, od_w_out.astype(BF16), lng, lnb, w13_bf, w2_bf, layer=1, slot=0,
                     ya_time_major=(batch, seq))
    return x4.reshape(batch, seq, d)
```

```python
import functools

import jax
import jax.numpy as jnp
from jax import lax
from jax.experimental import pallas as pl
from jax.experimental.pallas import tpu as pltpu

F32 = jnp.float32
BF16 = jnp.bfloat16

SUBLANES = 8
LANES = 128

D_MODEL = 1024
HALF = D_MODEL // 2
HEAD_DIM = 64
HGRN_DIM = 128
LRU_BLOCKS = 8
LRU_C = 8.0
CONV_WIDTH = 4
DEPTH = 2
ALPHA = (2 * DEPTH) ** 0.25
EPS = 1e-5
D_FF = 2816
LOG2_E = 1.4426950408889634

FFN_CHUNK = 256
HGRN_CHUNK = 128
HGRN_UNROLL = 4
ATT_BLOCK = 256
ATT_UNROLL = 4
VMEM_LIMIT = 56 * 1024 * 1024


def _cparams(sem):
    return pltpu.CompilerParams(dimension_semantics=sem, vmem_limit_bytes=VMEM_LIMIT)


def _proj_kernel(cols_a, cols_b, x_ref, w_ref, oa_ref, ob_ref, wa_ref, wb_ref, *scratch):
    @pl.when(pl.program_id(0) == 0)
    def _():
        for dst_ref, cols in ((wa_ref, cols_a), (wb_ref, cols_b)):
            for dst, src, n in cols:
                if n % LANES:
                    slot = -(-n // LANES) * LANES
                    dst_ref[:, dst:dst + slot] = jnp.zeros((dst_ref.shape[0], slot), BF16)
                dst_ref[:, dst:dst + n] = w_ref[:, src:src + n].astype(BF16)

    if len(x_ref.shape) == 2:
        xb = x_ref[...].astype(BF16)
        oa_ref[...] = jnp.dot(xb, wa_ref[...], preferred_element_type=F32)
        ob_ref[...] = jnp.dot(xb, wb_ref[...], preferred_element_type=F32).astype(BF16)
    else:
        nseq, steps, d = x_ref.shape
        xt_ref = scratch[0]
        for b in range(nseq):
            for c in range(d // LANES):
                xt_ref[c, pl.ds(b, steps, stride=nseq), :] = x_ref[b, :, c * LANES:(c + 1) * LANES]
        xt = jnp.concatenate([xt_ref[c] for c in range(d // LANES)], axis=1)
        oa_ref[...] = jnp.dot(xt.astype(BF16), wa_ref[...], preferred_element_type=F32)
        xb = x_ref[...].reshape(nseq * steps, d).astype(BF16)
        ob = jnp.dot(xb, wb_ref[...], preferred_element_type=F32).astype(BF16)
        ob_ref[...] = ob.reshape(ob_ref.shape)


def _proj(x, w, cols_a, cols_b, a_time_major=None, tm=512):
    t, d = x.shape

    def width(cols):
        return max(dst + -(-n // LANES) * LANES for dst, _, n in cols)

    na, nb = width(cols_a), width(cols_b)
    row = lambda i: (i, 0)
    scratch = [pltpu.VMEM((d, na), BF16), pltpu.VMEM((d, nb), BF16)]
    if a_time_major is None:
        x_spec, b_spec, b_shape = pl.BlockSpec((tm, d), row), pl.BlockSpec((tm, nb), row), (t, nb)
    else:
        batch, seq = a_time_major
        steps = tm // batch
        x = x.reshape(batch, seq, d)
        x_spec = pl.BlockSpec((batch, steps, d), lambda i: (0, i, 0))
        b_spec, b_shape = pl.BlockSpec((batch, steps, nb), lambda i: (0, i, 0)), (batch, seq, nb)
        scratch.append(pltpu.VMEM((d // LANES, tm, LANES), F32))
    oa, ob = pl.pallas_call(
        functools.partial(_proj_kernel, cols_a, cols_b),
        grid=(t // tm,),
        in_specs=[x_spec, pl.BlockSpec(w.shape, lambda i: (0, 0), pipeline_mode=pl.Buffered(1))],
        out_specs=[pl.BlockSpec((tm, na), row), b_spec],
        out_shape=[jax.ShapeDtypeStruct((t, na), F32), jax.ShapeDtypeStruct(b_shape, BF16)],
        scratch_shapes=scratch,
        compiler_params=_cparams(("arbitrary",)),
        name="in_proj",
    )(x, w)
    return oa, ob.reshape(t, nb)


def _lockstep(gens):
    results = [None] * len(gens)
    live = list(range(len(gens)))
    while live:
        for k in list(live):
            try:
                next(gens[k])
            except StopIteration as done:
                results[k] = done.value
                live.remove(k)
    return results


def _layer_norm_rows(y, g, b):
    mu = jnp.mean(y, axis=-1, keepdims=True)
    yc = y - mu
    var = jnp.mean(yc * yc, axis=-1, keepdims=True)
    return yc * lax.rsqrt(var + EPS) * g + b


def _layer_tail_kernel(ya_ref, yb_ref, x_ref, wo_ref, g1_ref, b1_ref, w13_ref, w2_ref, g2_ref, b2_ref,
                       o_ref, xb_ref, h_ref):
    tf = FFN_CHUNK
    nf = w2_ref.shape[0] // tf
    mix = jnp.dot(ya_ref[...], wo_ref[0:HALF, :], preferred_element_type=F32)
    mix = mix + jnp.dot(yb_ref[...], wo_ref[HALF:D_MODEL, :], preferred_element_type=F32)
    x1 = _layer_norm_rows(ALPHA * x_ref[...] + mix, g1_ref[...], b1_ref[...])
    o_ref[...] = x1
    xb_ref[...] = x1.astype(BF16)

    def hidden(c):
        xb = xb_ref[...]
        ca = pl.multiple_of(c * tf, tf)
        cb = pl.multiple_of((nf + c) * tf, tf)
        a = jnp.dot(xb, w13_ref[:, pl.ds(ca, tf)], preferred_element_type=F32)
        b = jnp.dot(xb, w13_ref[:, pl.ds(cb, tf)], preferred_element_type=F32)
        h_ref[:, pl.ds(ca, tf)] = (a * jax.nn.sigmoid(a) * b).astype(BF16)

    hidden(0)

    def pair(n, _):
        hidden(2 * n + 1)
        hidden(2 * n + 2)
        return 0

    lax.fori_loop(0, (nf - 1) // 2, pair, 0)

    for n0 in range(0, D_MODEL, tf):
        cols = slice(n0, n0 + tf)
        down = jnp.dot(h_ref[...], w2_ref[:, cols], preferred_element_type=F32)
        o_ref[:, cols] = ALPHA * o_ref[:, cols] + down
    o_ref[...] = _layer_norm_rows(o_ref[...], g2_ref[...], b2_ref[...])


def _layer_tail(ya, yb, x, w_out, ln_g, ln_b, w13, w2, layer, slot, tm=1024):
    t = x.shape[0]
    nf = D_FF // FFN_CHUNK
    assert nf % 2 == 1 and nf * FFN_CHUNK == D_FF and D_MODEL % FFN_CHUNK == 0
    d = D_MODEL
    resident = pl.Buffered(1)
    row = lambda i: (i, 0)
    return pl.pallas_call(
        _layer_tail_kernel,
        grid=(t // tm,),
        in_specs=[
            pl.BlockSpec((tm, HALF), row),
            pl.BlockSpec((tm, HALF), row),
            pl.BlockSpec((tm, d), row),
            pl.BlockSpec((None, d, d), lambda i: (slot, 0, 0), pipeline_mode=resident),
            pl.BlockSpec((None, 1, d), lambda i: (2 * layer, 0, 0)),
            pl.BlockSpec((None, 1, d), lambda i: (2 * layer, 0, 0)),
            pl.BlockSpec((None, d, 2 * D_FF), lambda i: (layer, 0, 0), pipeline_mode=resident),
            pl.BlockSpec((None, D_FF, d), lambda i: (layer, 0, 0), pipeline_mode=resident),
            pl.BlockSpec((None, 1, d), lambda i: (2 * layer + 1, 0, 0)),
            pl.BlockSpec((None, 1, d), lambda i: (2 * layer + 1, 0, 0)),
        ],
        out_specs=pl.BlockSpec((tm, d), row),
        out_shape=jax.ShapeDtypeStruct((t, d), F32),
        scratch_shapes=[pltpu.VMEM((tm, d), BF16), pltpu.VMEM((tm, D_FF), BF16)],
        compiler_params=_cparams(("parallel",)),
        name="layer_tail",
    )(ya, yb, x, w_out, ln_g, ln_b, w13, w2, ln_g, ln_b)


def _tile_rows(n):
    return lax.broadcasted_iota(jnp.int32, (SUBLANES, n), 0)


def _bcast_row(tile, r):
    return jnp.broadcast_to(tile[r:r + 1, :], tile.shape)


def _segmented_scans(x):
    n = len(x)
    row = _tile_rows(x[0].shape[1])
    zero = jnp.zeros_like(x[0])
    r3 = row & 3
    up4 = (row & 4) != 0

    p = {1: list(x)}
    p[2] = [t + jnp.where((row & 1) == 1, pltpu.roll(t, 1, 0), 0.0) for t in x]
    p[4] = [t + jnp.where(r3 == 2, pltpu.roll(t, 1, 0),
                          jnp.where(r3 == 3, pltpu.roll(t, 2, 0), 0.0)) for t in p[2]]
    p[8] = [t + jnp.where(up4, _bcast_row(t, 3), 0.0) for t in p[4]]

    e = {1: [zero] * n}
    e[2] = [jnp.where((row & 1) == 0, pltpu.roll(t, SUBLANES - 1, 0), 0.0) for t in x]
    r2 = [a + b for a, b in zip(e[2], x)]
    e[4] = [t + jnp.where(r3 == 1, pltpu.roll(r, SUBLANES - 1, 0),
                          jnp.where(r3 == 0, pltpu.roll(r, SUBLANES - 2, 0), 0.0))
            for t, r in zip(e[2], r2)]
    r4 = [a + b for a, b in zip(e[4], x)]
    e[8] = [t + jnp.where(up4, 0.0, _bcast_row(r, 4)) for t, r in zip(e[4], r4)]

    m = SUBLANES
    while m < SUBLANES * n:
        nt = m // SUBLANES
        pn, en = [], []
        for g in range(n // (2 * nt)):
            lo = slice(2 * nt * g, 2 * nt * g + nt)
            hi = slice(2 * nt * g + nt, 2 * nt * (g + 1))
            tot = _bcast_row(p[m][lo][-1], SUBLANES - 1)
            pn += p[m][lo] + [t + tot for t in p[m][hi]]
            first = e[m][hi][0][0:1, :] + x[hi][0][0:1, :]
            tot_hi = jnp.broadcast_to(first, zero.shape)
            en += [t + tot_hi for t in e[m][lo]] + e[m][hi]
        p[2 * m], e[2 * m] = pn, en
        m *= 2
    return p, e


def _level_map(c):
    t = lax.broadcasted_iota(jnp.int32, (c, c), 0)
    s = lax.broadcasted_iota(jnp.int32, (c, c), 1)
    x = t ^ s
    lv = jnp.full((c, c), -1, jnp.int32)
    m = 1
    while m < c:
        lv = lv + (x >= m).astype(jnp.int32)
        m *= 2
    return jnp.where(t < s, -2, lv)


def _hgrn_kernel(q_ref, f_ref, g_ref, v_ref, lb_ref, ng_ref, o_ref):
    c = HGRN_CHUNK
    n_tiles = c // SUBLANES
    n_chunks = q_ref.shape[0] // c
    lb = lb_ref[...]
    one_m_lb = 1.0 - lb
    ng = ng_ref[...]
    lv = _level_map(c)

    def chunk(ci, state):
        rows = slice(ci * c, (ci + 1) * c)
        z = f_ref[rows, :]
        q = q_ref[rows, :]
        v = v_ref[rows, :]
        gg = g_ref[rows, :]
        ez = jnp.exp(-jnp.abs(z))
        rz = 1.0 / (1.0 + ez)
        pos = z >= 0.0
        sig = jnp.where(pos, rz, ez * rz)
        nsig = jnp.where(pos, ez * rz, rz)
        lf = jnp.log2(lb + one_m_lb * sig)
        kk = one_m_lb * nsig

        lf_tiles = [lf[SUBLANES * i:SUBLANES * (i + 1), :] for i in range(n_tiles)]
        p, e = _segmented_scans(lf_tiles)

        def cat(ts):
            return jnp.concatenate(ts, axis=0)

        nt_dims = (((1,), (1,)), ((), ()))
        qb, kb = q.astype(BF16), kk.astype(BF16)
        yield
        scores = jnp.where(
            lv == -1, lax.dot_general(qb, kb, nt_dims, preferred_element_type=F32), 0.0)
        m, idx = 1, 0
        while m < c:
            qm = (q * jnp.exp2(cat(p[m]))).astype(BF16)
            km = kb if m == 1 else (kk * jnp.exp2(cat(e[m]))).astype(BF16)
            yield
            sm = lax.dot_general(qm, km, nt_dims, preferred_element_type=F32)
            scores = jnp.where(lv == idx, sm, scores)
            m *= 2
            idx += 1

        b = cat(p[c])
        qc = (q * jnp.exp2(b)).astype(BF16)
        kc = (kk * jnp.exp2(cat(e[c]))).astype(BF16)
        decay = jnp.exp2(b[c - 1:c, :])
        sb = scores.astype(BF16)
        yield
        kv = lax.dot_general(kc, v, (((0,), (0,)), ((), ())), preferred_element_type=F32)
        o = jnp.dot(sb, v, preferred_element_type=F32)
        yield
        st = state[0]
        o = o + jnp.dot(qc, st.astype(BF16), preferred_element_type=F32)
        state[0] = st * jnp.broadcast_to(decay, st.shape).T + kv
        yield
        ms = jnp.mean(o * o, axis=-1, keepdims=True)
        on = o * lax.rsqrt(ms + EPS) * ng
        o_ref[rows, :] = (on * (gg * jax.nn.sigmoid(gg))).astype(o_ref.dtype)

    state = [jnp.zeros((HGRN_DIM, HGRN_DIM), F32)]
    for c0 in range(0, n_chunks, HGRN_UNROLL):
        _lockstep([chunk(c0 + k, state) for k in range(HGRN_UNROLL)])


def _hgrn(pa, pb, lb, ng, batch, seq):
    t = pa.shape[0]
    nh = HALF // HGRN_DIM
    return pl.pallas_call(
        _hgrn_kernel,
        grid=(batch, nh),
        in_specs=[
            pl.BlockSpec((seq, HGRN_DIM), lambda b, h: (b, h)),
            pl.BlockSpec((seq, HGRN_DIM), lambda b, h: (b, nh + h)),
            pl.BlockSpec((seq, HGRN_DIM), lambda b, h: (b, 2 * nh + h)),
            pl.BlockSpec((seq, HGRN_DIM), lambda b, h: (b, h)),
            pl.BlockSpec((1, HGRN_DIM), lambda b, h: (0, h)),
            pl.BlockSpec((1, HGRN_DIM), lambda b, h: (0, 0)),
        ],
        out_specs=pl.BlockSpec((seq, HGRN_DIM), lambda b, h: (b, h)),
        out_shape=jax.ShapeDtypeStruct((t, HALF), BF16),
        compiler_params=_cparams(("parallel", "parallel")),
        name="hgrn2",
    )(pa, pa, pa, pb, lb, ng)


def _fox_gate_kernel(f_ref, bf_ref, c_ref):
    s = f_ref.shape[0]
    ft = f_ref[...].T[0:SUBLANES, :]
    x = ft + bf_ref[...][:, 0:1]
    lf = jnp.minimum(x, 0.0) - jnp.log1p(jnp.exp(-jnp.abs(x)))
    lane = lax.broadcasted_iota(jnp.int32, lf.shape, 1)
    d = 1
    while d < s:
        lf = lf + jnp.where(lane >= d, pltpu.roll(lf, d, 1), 0.0)
        d *= 2
    c_ref[0] = lf


def _fox_gate(pa, col_block, bf_pad, batch, seq):
    return pl.pallas_call(
        _fox_gate_kernel,
        grid=(batch,),
        in_specs=[
            pl.BlockSpec((seq, LANES), lambda b: (b, col_block)),
            pl.BlockSpec((SUBLANES, LANES), lambda b: (0, 0)),
        ],
        out_specs=pl.BlockSpec((1, SUBLANES, seq), lambda b: (b, 0, 0)),
        out_shape=jax.ShapeDtypeStruct((batch, SUBLANES, seq), F32),
        compiler_params=_cparams(("parallel",)),
        name="fox_gate",
    )(pa, bf_pad)


def _head_lane_masks(shape):
    lane = lax.broadcasted_iota(jnp.int32, shape, len(shape) - 1)
    first = lane < HEAD_DIM
    return first, jnp.logical_not(first)


def _causal_tile_order(nq):
    u = ATT_UNROLL
    assert nq % u == 0
    pairs = [(i, i) for i in range(nq)]
    left = {i: i for i in range(nq)}
    while any(left.values()):
        pick = sorted((i for i in left if left[i]), key=lambda i: -left[i])[:u]
        assert len(pick) == u, "tile list does not split into groups of distinct query blocks"
        for i in pick:
            pairs.append((i, left[i] - 1))
            left[i] -= 1
    return [p[0] for p in pairs], [p[1] for p in pairs]


def _split_heads(q_ref, v_ref, qh_ref, vh_ref):
    scale = HEAD_DIM ** -0.5
    t = ATT_BLOCK
    h0, h1 = _head_lane_masks((t, LANES))
    zero = jnp.zeros((t, LANES), BF16)
    for i in range(q_ref.shape[0] // t):
        q = q_ref[i * t:(i + 1) * t, :] * jnp.asarray(scale, BF16)
        v = v_ref[i * t:(i + 1) * t, :]
        qh_ref[i, 0:t, :] = jnp.where(h0, q, zero)
        qh_ref[i, t:2 * t, :] = jnp.where(h1, q, zero)
        vh_ref[i, 0:t, :] = jnp.where(h0, v, zero)
        vh_ref[i, t:2 * t, :] = jnp.where(h1, v, zero)


def _tile_waves(nq, tile_fn):
    qi, kj = _causal_tile_order(nq)
    state = [None] * nq
    for w0 in range(0, len(qi), ATT_UNROLL):
        wave = list(zip(qi[w0:w0 + ATT_UNROLL], kj[w0:w0 + ATT_UNROLL]))
        results = _lockstep([tile_fn(i, j, state[i], i == j) for i, j in wave])
        for (i, _), r in zip(wave, results):
            state[i] = r
    return state


def _foxt_kernel(q_ref, k_ref, v_ref, c_ref, o_ref, qt_ref, vt_ref, cb_ref):
    tq = tk = ATT_BLOCK
    seq = q_ref.shape[0]
    nq = seq // tq
    scale = HEAD_DIM ** -0.5
    lane0, lane1 = _head_lane_masks((tq, LANES))
    dim_head0 = lax.broadcasted_iota(jnp.int32, (LANES, tq), 0) < HEAD_DIM

    for i in range(nq):
        q = (q_ref[i * tq:(i + 1) * tq, :] * jnp.asarray(scale, BF16)).astype(F32)
        v = v_ref[i * tq:(i + 1) * tq, :].astype(F32)
        for h, lanes in enumerate((lane0, lane1)):
            qt_ref[i, :, h * tq:(h + 1) * tq] = jnp.where(lanes, q, 0.0).T.astype(BF16)
            vt_ref[i, :, h * tq:(h + 1) * tq] = jnp.where(lanes, v, 0.0).T.astype(BF16)
    for h in range(2):
        cb_ref[h] = jnp.broadcast_to(c_ref[0, 0, h:h + 1, :], (LANES, seq)).T

    def tile(i, j, state, diag):
        keys = slice(j * tk, (j + 1) * tk)
        s = jnp.dot(k_ref[keys, :], qt_ref[i], preferred_element_type=F32)
        yield
        wide = tq // LANES
        s = jnp.concatenate([s[:, 0:tq] - jnp.concatenate([cb_ref[0, keys, :]] * wide, axis=1),
                             s[:, tq:2 * tq] - jnp.concatenate([cb_ref[1, keys, :]] * wide, axis=1)],
                            axis=1)
        if diag:
            kpos = lax.broadcasted_iota(jnp.int32, s.shape, 0)
            qpos = lax.broadcasted_iota(jnp.int32, s.shape, 1) & (tq - 1)
            s = jnp.where(kpos <= qpos, s, -jnp.inf)
        col_max = jnp.max(s, axis=0, keepdims=True)
        if diag:
            m_new = col_max
        else:
            m_old, l_old, acc_old = state
            m_new = jnp.maximum(m_old, col_max)
            alpha = jnp.exp(m_old - m_new)
        p = jnp.exp(s - m_new)
        psum = jnp.sum(p, axis=0, keepdims=True)
        pb = p.astype(BF16)
        yield
        pv = jnp.dot(vt_ref[j], jnp.concatenate([pb[:, 0:tq], pb[:, tq:2 * tq]], axis=0),
                     preferred_element_type=F32)
        yield
        if diag:
            return m_new, psum, pv
        a_rows = jnp.where(dim_head0, alpha[:, 0:tq], alpha[:, tq:2 * tq])
        return m_new, alpha * l_old + psum, acc_old * a_rows + pv

    for i, (_, l, acc) in enumerate(_tile_waves(nq, tile)):
        l_rows = jnp.where(dim_head0, l[:, 0:tq], l[:, tq:2 * tq])
        o_ref[i * tq:(i + 1) * tq, :] = (acc / l_rows).T.astype(o_ref.dtype)


def _foxt(pb, col0, cpair, batch, seq):
    t = pb.shape[0]
    npair = HALF // LANES
    nq = seq // ATT_BLOCK
    return pl.pallas_call(
        _foxt_kernel,
        grid=(batch, npair),
        in_specs=[
            pl.BlockSpec((seq, LANES), lambda b, p: (b, col0 + p)),
            pl.BlockSpec((seq, LANES), lambda b, p: (b, col0 + npair + p)),
            pl.BlockSpec((seq, LANES), lambda b, p: (b, col0 + 2 * npair + p)),
            pl.BlockSpec((1, 1, 2, seq), lambda b, p: (b, p, 0, 0)),
        ],
        out_specs=pl.BlockSpec((seq, LANES), lambda b, p: (b, p)),
        out_shape=jax.ShapeDtypeStruct((t, HALF), BF16),
        scratch_shapes=[
            pltpu.VMEM((nq, LANES, 2 * ATT_BLOCK), BF16),
            pltpu.VMEM((nq, LANES, 2 * ATT_BLOCK), BF16),
            pltpu.VMEM((2, seq, LANES), F32),
        ],
        compiler_params=_cparams(("parallel", "parallel")),
        name="fox_attention",
    )(pb, pb, pb, cpair)


def _suffix_matrix(tk):
    r = lax.broadcasted_iota(jnp.int32, (2 * tk, 2 * tk), 0) % tk
    cidx = lax.broadcasted_iota(jnp.int32, (2 * tk, 2 * tk), 1)
    keep = jnp.logical_or(cidx >= tk, r > cidx)
    return jnp.where(keep, 1.0, 0.0).astype(BF16)


def _sb_kernel(q_ref, k_ref, v_ref, o_ref, qh_ref, vh_ref):
    tq = tk = ATT_BLOCK
    nq = q_ref.shape[0] // tq
    nsub = tk // LANES
    nt_dims = (((1,), (1,)), ((), ()))
    umat = _suffix_matrix(LANES)

    _split_heads(q_ref, v_ref, qh_ref, vh_ref)

    def tile(i, j, state, diag):
        z = lax.dot_general(qh_ref[i], k_ref[j * tk:(j + 1) * tk, :], nt_dims,
                            preferred_element_type=F32)
        yield
        logb = jnp.minimum(z, 0.0) - jnp.log(1.0 + jnp.exp2(jnp.abs(z) * -LOG2_E))
        lom = logb - z
        if diag:
            qpos = lax.broadcasted_iota(jnp.int32, z.shape, 0) & (tq - 1)
            kpos = lax.broadcasted_iota(jnp.int32, z.shape, 1)
            mask = kpos < qpos
            lom = jnp.where(mask, lom, 0.0)
        parts = []
        for c in reversed(range(nsub)):
            part = lom[:, c * LANES:(c + 1) * LANES]
            hi = part.astype(BF16)
            lo = (part - hi.astype(F32)).astype(BF16)
            parts.append(jnp.concatenate([hi, lo], axis=1))
        yield
        r = jnp.dot(jnp.concatenate(parts, axis=0), umat, preferred_element_type=F32)
        yield
        carry = jnp.zeros((2 * tq, LANES), F32) if diag else state[0]
        suffix = [None] * nsub
        for n_c, c in enumerate(reversed(range(nsub))):
            rc = r[n_c * 2 * tq:(n_c + 1) * 2 * tq]
            suffix[c] = rc[:, 0:LANES] + carry
            carry = carry + rc[:, LANES:2 * LANES]
        w = jnp.exp(logb + jnp.concatenate(suffix, axis=1))
        if diag:
            w = jnp.where(mask, w, 0.0)
        wb = w.astype(BF16)
        yield
        wv = jnp.dot(jnp.concatenate([wb[0:tq], wb[tq:2 * tq]], axis=1), vh_ref[j],
                     preferred_element_type=F32)
        yield
        return carry, (wv if diag else state[1] + wv)

    for i, (_, acc) in enumerate(_tile_waves(nq, tile)):
        o_ref[i * tq:(i + 1) * tq, :] = acc.astype(o_ref.dtype)


def _sb(pb, col0, batch, seq):
    t = pb.shape[0]
    npair = HALF // LANES
    nq = seq // ATT_BLOCK
    return pl.pallas_call(
        _sb_kernel,
        grid=(batch, npair),
        in_specs=[
            pl.BlockSpec((seq, LANES), lambda b, p: (b, col0 + p)),
            pl.BlockSpec((seq, LANES), lambda b, p: (b, col0 + npair + p)),
            pl.BlockSpec((seq, LANES), lambda b, p: (b, col0 + 2 * npair + p)),
        ],
        out_specs=pl.BlockSpec((seq, LANES), lambda b, p: (b, p)),
        out_shape=jax.ShapeDtypeStruct((t, HALF), BF16),
        scratch_shapes=[
            pltpu.VMEM((nq, 2 * ATT_BLOCK, LANES), BF16),
            pltpu.VMEM((nq, 2 * ATT_BLOCK, LANES), BF16),
        ],
        compiler_params=_cparams(("parallel", "parallel")),
        name="sb_attention",
    )(pb, pb, pb)


def _lru_kernel(x_ref, gate_ref, cw_ref, cb_ref, wg_ref, bg_ref, lam_ref, o_ref,
                xe_ref, a_ref, u_ref, h_ref, y_ref):
    step = pl.program_id(0)
    rows, w = x_ref.shape
    hist = (CONV_WIDTH - 1) * SUBLANES

    @pl.when(step == 0)
    def _():
        xe_ref[0:hist, :] = jnp.zeros((hist, w), F32)
        h_ref[...] = jnp.zeros_like(h_ref)

    @pl.when(step > 0)
    def _():
        xe_ref[0:hist, :] = xe_ref[rows:rows + hist, :]

    x = x_ref[...]
    xe_ref[hist:hist + rows, :] = x
    xc = cb_ref[...] + cw_ref[CONV_WIDTH - 1:CONV_WIDTH, :] * x
    for d in range(1, CONV_WIDTH):
        back = hist - d * SUBLANES
        xc = xc + cw_ref[CONV_WIDTH - 1 - d:CONV_WIDTH - d, :] * xe_ref[back:back + rows, :]

    gates = jnp.dot(xc.astype(BF16), wg_ref[...], preferred_element_type=F32) + bg_ref[...]
    r = jax.nn.sigmoid(gates[:, 0:w])
    ig = jax.nn.sigmoid(gates[:, w:2 * w])
    lam = lam_ref[...]
    softplus_neg_lam = jnp.maximum(-lam, 0.0) + jnp.log1p(jnp.exp(-jnp.abs(lam)))
    log_a = -LRU_C * r * softplus_neg_lam
    a_ref[...] = jnp.exp(log_a)
    th = jnp.tanh(-log_a)
    u_ref[...] = jnp.sqrt(2.0 * th / (1.0 + th)) * (ig * xc)

    h = h_ref[...]
    for t0 in range(0, rows, SUBLANES):
        h = a_ref[t0:t0 + SUBLANES, :] * h + u_ref[t0:t0 + SUBLANES, :]
        u_ref[t0:t0 + SUBLANES, :] = h
    h_ref[...] = h
    y = u_ref[...] * jax.nn.gelu(gate_ref[...], approximate=True)
    nseq, steps, _ = o_ref.shape
    for c in range(w // LANES):
        y_ref[c] = y[:, c * LANES:(c + 1) * LANES]
    for b in range(nseq):
        for c in range(w // LANES):
            o_ref[b, :, c * LANES:(c + 1) * LANES] = (
                y_ref[c, pl.ds(b, steps, stride=nseq), :].astype(o_ref.dtype))


def _lru(pa, cw, cb, wg, bg, lam, batch, seq, rows=512):
    t = pa.shape[0]
    w = HALF
    steps = rows // batch
    fixed = lambda s: (0, 0)
    return pl.pallas_call(
        _lru_kernel,
        grid=(t // rows,),
        in_specs=[
            pl.BlockSpec((rows, w), lambda s: (s, 0)),
            pl.BlockSpec((rows, w), lambda s: (s, 1)),
            pl.BlockSpec((CONV_WIDTH, w), fixed),
            pl.BlockSpec((1, w), fixed),
            pl.BlockSpec((w, 2 * w), fixed),
            pl.BlockSpec((1, 2 * w), fixed),
            pl.BlockSpec((1, w), fixed),
        ],
        out_specs=pl.BlockSpec((batch, steps, w), lambda s: (0, s, 0)),
        out_shape=jax.ShapeDtypeStruct((batch, seq, w), BF16),
        scratch_shapes=[
            pltpu.VMEM((rows + (CONV_WIDTH - 1) * SUBLANES, w), F32),
            pltpu.VMEM((rows, w), F32),
            pltpu.VMEM((rows, w), F32),
            pltpu.VMEM((SUBLANES, w), F32),
            pltpu.VMEM((w // LANES, rows, LANES), F32),
        ],
        compiler_params=_cparams(("arbitrary",)),
        name="rg_lru",
    )(pa, pa, cw, cb, wg, bg, lam).reshape(t, w)


def _block_diag(wb):
    n, bd, _ = wb.shape
    eye = jnp.eye(n, dtype=wb.dtype)
    return (eye[:, None, :, None] * wb[:, :, None, :]).reshape(n * bd, n * bd)


def kernel(x, ev_w_in, ev_fox_bf, hgrn_lb, ev_hgrn_norm_g, ev_w_out, od_w_in, od_conv_w, od_conv_b,
           od_gate_a_w, od_gate_a_b, od_gate_x_w, od_gate_x_b, od_lru_lambda, od_w_out, ffn_w13,
           ffn_w2, ln_g, ln_b):
    batch, seq, d = x.shape
    t = batch * seq
    xf = x.reshape(t, d)
    lb_all = jnp.cumsum(jax.nn.softmax(hgrn_lb.astype(F32), axis=0), axis=0)
    h = HALF

    nfox = ev_fox_bf.shape[1]
    cols_a = [(0, 0, 2 * h), (2 * h, 3 * h, h), (3 * h, 7 * h, nfox)]
    cols_b = [(0, 2 * h, h), (h, 4 * h, 3 * h)]
    pa, pb = _proj(xf, ev_w_in[0], cols_a, cols_b)
    ya = _hgrn(pa, pb, lb_all[0].reshape(1, h), ev_hgrn_norm_g[0].reshape(1, HGRN_DIM), batch, seq)
    bf_pad = jnp.broadcast_to(ev_fox_bf[0].astype(F32).reshape(nfox, 1), (nfox, LANES))
    cum = _fox_gate(pa, 3 * h // LANES, bf_pad, batch, seq)
    yb = _foxt(pb, h // LANES, cum.reshape(batch, nfox // 2, 2, seq), batch, seq)
    lng = ln_g.astype(F32).reshape(2 * DEPTH, 1, d)
    lnb = ln_b.astype(F32).reshape(2 * DEPTH, 1, d)
    w13_bf, w2_bf = ffn_w13.astype(BF16), ffn_w2.astype(BF16)
    x2 = _layer_tail(ya, yb, xf, ev_w_out.astype(BF16), lng, lnb, w13_bf, w2_bf, layer=0, slot=0)

    assert batch == SUBLANES
    pa, pb = _proj(x2, od_w_in[0], [(0, 0, 2 * h)], [(0, 2 * h, 3 * h)], a_time_major=(batch, seq))
    wg = jnp.concatenate([_block_diag(od_gate_a_w[0]), _block_diag(od_gate_x_w[0])], axis=1).astype(BF16)
    bg = jnp.concatenate([od_gate_a_b[0], od_gate_x_b[0]]).reshape(1, 2 * h).astype(F32)
    yc = _lru(pa, od_conv_w[0], od_conv_b[0].reshape(1, h), wg, bg, od_lru_lambda[0].reshape(1, h),
              batch, seq)
    yd = _sb(pb, 0, batch, seq)
    x4 = _layer_tail(yc, yd, x2, od_w_out.astype(BF16), lng, lnb, w13_bf, w2_bf, layer=1, slot=0)
    return x4.reshape(batch, seq, d)
```

```python
import functools

import jax
import jax.numpy as jnp
from jax import lax
from jax.experimental import pallas as pl
from jax.experimental.pallas import tpu as pltpu

F32 = jnp.float32
BF16 = jnp.bfloat16

SUBLANES = 8
LANES = 128

D_MODEL = 1024
HALF = D_MODEL // 2
HEAD_DIM = 64
HGRN_DIM = 128
LRU_BLOCKS = 8
LRU_C = 8.0
CONV_WIDTH = 4
DEPTH = 2
ALPHA = (2 * DEPTH) ** 0.25
EPS = 1e-5
D_FF = 2816
LOG2_E = 1.4426950408889634

FFN_CHUNK = 256
TAIL_ROW_PARTS = 4
HGRN_CHUNK = 128
HGRN_UNROLL = 4
ATT_BLOCK = 256
ATT_UNROLL = 4
VMEM_LIMIT = 56 * 1024 * 1024


def _cparams(sem):
    return pltpu.CompilerParams(dimension_semantics=sem, vmem_limit_bytes=VMEM_LIMIT)


def _proj_kernel(cols_a, cols_b, x_ref, w_ref, oa_ref, ob_ref, wa_ref, wb_ref, *scratch):
    @pl.when(pl.program_id(0) == 0)
    def _():
        for dst_ref, cols in ((wa_ref, cols_a), (wb_ref, cols_b)):
            for dst, src, n in cols:
                if n % LANES:
                    slot = -(-n // LANES) * LANES
                    dst_ref[:, dst:dst + slot] = jnp.zeros((dst_ref.shape[0], slot), BF16)
                dst_ref[:, dst:dst + n] = w_ref[:, src:src + n].astype(BF16)

    if len(x_ref.shape) == 2:
        xb = x_ref[...].astype(BF16)
        oa_ref[...] = jnp.dot(xb, wa_ref[...], preferred_element_type=F32)
        ob_ref[...] = jnp.dot(xb, wb_ref[...], preferred_element_type=F32).astype(BF16)
    else:
        nseq, steps, d = x_ref.shape
        xt_ref = scratch[0]
        for b in range(nseq):
            for c in range(d // LANES):
                xt_ref[c, pl.ds(b, steps, stride=nseq), :] = x_ref[b, :, c * LANES:(c + 1) * LANES]
        xt = jnp.concatenate([xt_ref[c] for c in range(d // LANES)], axis=1)
        oa_ref[...] = jnp.dot(xt.astype(BF16), wa_ref[...], preferred_element_type=F32)
        xb = x_ref[...].reshape(nseq * steps, d).astype(BF16)
        ob = jnp.dot(xb, wb_ref[...], preferred_element_type=F32).astype(BF16)
        ob_ref[...] = ob.reshape(ob_ref.shape)


def _proj(x, w, cols_a, cols_b, a_time_major=None, tm=512):
    t, d = x.shape

    def width(cols):
        return max(dst + -(-n // LANES) * LANES for dst, _, n in cols)

    na, nb = width(cols_a), width(cols_b)
    row = lambda i: (i, 0)
    scratch = [pltpu.VMEM((d, na), BF16), pltpu.VMEM((d, nb), BF16)]
    if a_time_major is None:
        x_spec, b_spec, b_shape = pl.BlockSpec((tm, d), row), pl.BlockSpec((tm, nb), row), (t, nb)
    else:
        batch, seq = a_time_major
        steps = tm // batch
        x = x.reshape(batch, seq, d)
        x_spec = pl.BlockSpec((batch, steps, d), lambda i: (0, i, 0))
        b_spec, b_shape = pl.BlockSpec((batch, steps, nb), lambda i: (0, i, 0)), (batch, seq, nb)
        scratch.append(pltpu.VMEM((d // LANES, tm, LANES), F32))
    oa, ob = pl.pallas_call(
        functools.partial(_proj_kernel, cols_a, cols_b),
        grid=(t // tm,),
        in_specs=[x_spec, pl.BlockSpec(w.shape, lambda i: (0, 0), pipeline_mode=pl.Buffered(1))],
        out_specs=[pl.BlockSpec((tm, na), row), b_spec],
        out_shape=[jax.ShapeDtypeStruct((t, na), F32), jax.ShapeDtypeStruct(b_shape, BF16)],
        scratch_shapes=scratch,
        compiler_params=_cparams(("arbitrary",)),
        name="in_proj",
    )(x, w)
    return oa, ob.reshape(t, nb)


def _lockstep(gens):
    results = [None] * len(gens)
    live = list(range(len(gens)))
    while live:
        for k in list(live):
            try:
                next(gens[k])
            except StopIteration as done:
                results[k] = done.value
                live.remove(k)
    return results


def _layer_norm_rows(y, g, b):
    mu = jnp.mean(y, axis=-1, keepdims=True)
    yc = y - mu
    var = jnp.mean(yc * yc, axis=-1, keepdims=True)
    return yc * lax.rsqrt(var + EPS) * g + b


def _layer_tail_kernel(ya_ref, yb_ref, x_ref, wo_ref, g1_ref, b1_ref, w13_ref, w2_ref, g2_ref, b2_ref,
                       o_ref, xb_ref, h_ref):
    tf = FFN_CHUNK
    nf = w2_ref.shape[0] // tf
    tm = x_ref.shape[0]
    n_parts = TAIL_ROW_PARTS

    def part(p):
        rows = slice(p * tm // n_parts, (p + 1) * tm // n_parts)
        mix = jnp.dot(ya_ref[rows, :], wo_ref[0:HALF, :], preferred_element_type=F32)
        mix = mix + jnp.dot(yb_ref[rows, :], wo_ref[HALF:D_MODEL, :], preferred_element_type=F32)
        yield
        x1 = _layer_norm_rows(ALPHA * x_ref[rows, :] + mix, g1_ref[...], b1_ref[...])
        o_ref[rows, :] = x1
        xb_ref[rows, :] = x1.astype(BF16)
        for c in range(nf):
            xb = xb_ref[rows, :]
            a = jnp.dot(xb, w13_ref[:, c * tf:(c + 1) * tf], preferred_element_type=F32)
            b = jnp.dot(xb, w13_ref[:, (nf + c) * tf:(nf + c + 1) * tf], preferred_element_type=F32)
            yield
            h_ref[rows, c * tf:(c + 1) * tf] = (a * jax.nn.sigmoid(a) * b).astype(BF16)
        for n0 in range(0, D_MODEL, tf):
            cols = slice(n0, n0 + tf)
            down = jnp.dot(h_ref[rows, :], w2_ref[:, cols], preferred_element_type=F32)
            yield
            o_ref[rows, cols] = ALPHA * o_ref[rows, cols] + down
        o_ref[rows, :] = _layer_norm_rows(o_ref[rows, :], g2_ref[...], b2_ref[...])

    _lockstep([part(p) for p in range(n_parts)])


def _layer_tail(ya, yb, x, w_out, ln_g, ln_b, w13, w2, layer, slot, tm=1024):
    t = x.shape[0]
    nf = D_FF // FFN_CHUNK
    assert nf % 2 == 1 and nf * FFN_CHUNK == D_FF and D_MODEL % FFN_CHUNK == 0
    d = D_MODEL
    resident = pl.Buffered(1)
    row = lambda i: (i, 0)
    return pl.pallas_call(
        _layer_tail_kernel,
        grid=(t // tm,),
        in_specs=[
            pl.BlockSpec((tm, HALF), row),
            pl.BlockSpec((tm, HALF), row),
            pl.BlockSpec((tm, d), row),
            pl.BlockSpec((None, d, d), lambda i: (slot, 0, 0), pipeline_mode=resident),
            pl.BlockSpec((None, 1, d), lambda i: (2 * layer, 0, 0)),
            pl.BlockSpec((None, 1, d), lambda i: (2 * layer, 0, 0)),
            pl.BlockSpec((None, d, 2 * D_FF), lambda i: (layer, 0, 0), pipeline_mode=resident),
            pl.BlockSpec((None, D_FF, d), lambda i: (layer, 0, 0), pipeline_mode=resident),
            pl.BlockSpec((None, 1, d), lambda i: (2 * layer + 1, 0, 0)),
            pl.BlockSpec((None, 1, d), lambda i: (2 * layer + 1, 0, 0)),
        ],
        out_specs=pl.BlockSpec((tm, d), row),
        out_shape=jax.ShapeDtypeStruct((t, d), F32),
        scratch_shapes=[pltpu.VMEM((tm, d), BF16), pltpu.VMEM((tm, D_FF), BF16)],
        compiler_params=_cparams(("parallel",)),
        name="layer_tail",
    )(ya, yb, x, w_out, ln_g, ln_b, w13, w2, ln_g, ln_b)


def _tile_rows(n):
    return lax.broadcasted_iota(jnp.int32, (SUBLANES, n), 0)


def _bcast_row(tile, r):
    return jnp.broadcast_to(tile[r:r + 1, :], tile.shape)


def _segmented_scans(x):
    n = len(x)
    row = _tile_rows(x[0].shape[1])
    zero = jnp.zeros_like(x[0])
    r3 = row & 3
    up4 = (row & 4) != 0

    p = {1: list(x)}
    p[2] = [t + jnp.where((row & 1) == 1, pltpu.roll(t, 1, 0), 0.0) for t in x]
    p[4] = [t + jnp.where(r3 == 2, pltpu.roll(t, 1, 0),
                          jnp.where(r3 == 3, pltpu.roll(t, 2, 0), 0.0)) for t in p[2]]
    p[8] = [t + jnp.where(up4, _bcast_row(t, 3), 0.0) for t in p[4]]

    e = {1: [zero] * n}
    e[2] = [jnp.where((row & 1) == 0, pltpu.roll(t, SUBLANES - 1, 0), 0.0) for t in x]
    r2 = [a + b for a, b in zip(e[2], x)]
    e[4] = [t + jnp.where(r3 == 1, pltpu.roll(r, SUBLANES - 1, 0),
                          jnp.where(r3 == 0, pltpu.roll(r, SUBLANES - 2, 0), 0.0))
            for t, r in zip(e[2], r2)]
    r4 = [a + b for a, b in zip(e[4], x)]
    e[8] = [t + jnp.where(up4, 0.0, _bcast_row(r, 4)) for t, r in zip(e[4], r4)]

    m = SUBLANES
    while m < SUBLANES * n:
        nt = m // SUBLANES
        pn, en = [], []
        for g in range(n // (2 * nt)):
            lo = slice(2 * nt * g, 2 * nt * g + nt)
            hi = slice(2 * nt * g + nt, 2 * nt * (g + 1))
            tot = _bcast_row(p[m][lo][-1], SUBLANES - 1)
            pn += p[m][lo] + [t + tot for t in p[m][hi]]
            first = e[m][hi][0][0:1, :] + x[hi][0][0:1, :]
            tot_hi = jnp.broadcast_to(first, zero.shape)
            en += [t + tot_hi for t in e[m][lo]] + e[m][hi]
        p[2 * m], e[2 * m] = pn, en
        m *= 2
    return p, e


def _level_map(c):
    t = lax.broadcasted_iota(jnp.int32, (c, c), 0)
    s = lax.broadcasted_iota(jnp.int32, (c, c), 1)
    x = t ^ s
    lv = jnp.full((c, c), -1, jnp.int32)
    m = 1
    while m < c:
        lv = lv + (x >= m).astype(jnp.int32)
        m *= 2
    return jnp.where(t < s, -2, lv)


def _hgrn_kernel(q_ref, f_ref, g_ref, v_ref, lb_ref, ng_ref, o_ref):
    c = HGRN_CHUNK
    n_tiles = c // SUBLANES
    n_chunks = q_ref.shape[0] // c
    lb = lb_ref[...]
    one_m_lb = 1.0 - lb
    ng = ng_ref[...]
    lv = _level_map(c)

    def chunk(ci, state):
        rows = slice(ci * c, (ci + 1) * c)
        z = f_ref[rows, :]
        q = q_ref[rows, :]
        v = v_ref[rows, :]
        gg = g_ref[rows, :]
        ez = jnp.exp(-jnp.abs(z))
        rz = 1.0 / (1.0 + ez)
        pos = z >= 0.0
        sig = jnp.where(pos, rz, ez * rz)
        nsig = jnp.where(pos, ez * rz, rz)
        lf = jnp.log2(lb + one_m_lb * sig)
        kk = one_m_lb * nsig

        lf_tiles = [lf[SUBLANES * i:SUBLANES * (i + 1), :] for i in range(n_tiles)]
        p, e = _segmented_scans(lf_tiles)

        def cat(ts):
            return jnp.concatenate(ts, axis=0)

        nt_dims = (((1,), (1,)), ((), ()))
        qb, kb = q.astype(BF16), kk.astype(BF16)
        yield
        scores = jnp.where(
            lv == -1, lax.dot_general(qb, kb, nt_dims, preferred_element_type=F32), 0.0)
        m, idx = 1, 0
        while m < c:
            qm = (q * jnp.exp2(cat(p[m]))).astype(BF16)
            km = kb if m == 1 else (kk * jnp.exp2(cat(e[m]))).astype(BF16)
            yield
            sm = lax.dot_general(qm, km, nt_dims, preferred_element_type=F32)
            scores = jnp.where(lv == idx, sm, scores)
            m *= 2
            idx += 1

        b = cat(p[c])
        qc = (q * jnp.exp2(b)).astype(BF16)
        kc = (kk * jnp.exp2(cat(e[c]))).astype(BF16)
        decay = jnp.exp2(b[c - 1:c, :])
        sb = scores.astype(BF16)
        yield
        kv = lax.dot_general(kc, v, (((0,), (0,)), ((), ())), preferred_element_type=F32)
        o = jnp.dot(sb, v, preferred_element_type=F32)
        yield
        st = state[0]
        o = o + jnp.dot(qc, st.astype(BF16), preferred_element_type=F32)
        state[0] = st * jnp.broadcast_to(decay, st.shape).T + kv
        yield
        ms = jnp.mean(o * o, axis=-1, keepdims=True)
        on = o * lax.rsqrt(ms + EPS) * ng
        o_ref[rows, :] = (on * (gg * jax.nn.sigmoid(gg))).astype(o_ref.dtype)

    state = [jnp.zeros((HGRN_DIM, HGRN_DIM), F32)]
    for c0 in range(0, n_chunks, HGRN_UNROLL):
        _lockstep([chunk(c0 + k, state) for k in range(HGRN_UNROLL)])


def _hgrn(pa, pb, lb, ng, batch, seq):
    t = pa.shape[0]
    nh = HALF // HGRN_DIM
    return pl.pallas_call(
        _hgrn_kernel,
        grid=(batch, nh),
        in_specs=[
            pl.BlockSpec((seq, HGRN_DIM), lambda b, h: (b, h)),
            pl.BlockSpec((seq, HGRN_DIM), lambda b, h: (b, nh + h)),
            pl.BlockSpec((seq, HGRN_DIM), lambda b, h: (b, 2 * nh + h)),
            pl.BlockSpec((seq, HGRN_DIM), lambda b, h: (b, h)),
            pl.BlockSpec((1, HGRN_DIM), lambda b, h: (0, h)),
            pl.BlockSpec((1, HGRN_DIM), lambda b, h: (0, 0)),
        ],
        out_specs=pl.BlockSpec((seq, HGRN_DIM), lambda b, h: (b, h)),
        out_shape=jax.ShapeDtypeStruct((t, HALF), BF16),
        compiler_params=_cparams(("parallel", "parallel")),
        name="hgrn2",
    )(pa, pa, pa, pb, lb, ng)


def _fox_gate_kernel(f_ref, bf_ref, c_ref):
    s = f_ref.shape[0]
    ft = f_ref[...].T[0:SUBLANES, :]
    x = ft + bf_ref[...][:, 0:1]
    lf = jnp.minimum(x, 0.0) - jnp.log1p(jnp.exp(-jnp.abs(x)))
    lane = lax.broadcasted_iota(jnp.int32, lf.shape, 1)
    d = 1
    while d < s:
        lf = lf + jnp.where(lane >= d, pltpu.roll(lf, d, 1), 0.0)
        d *= 2
    c_ref[0] = lf


def _fox_gate(pa, col_block, bf_pad, batch, seq):
    return pl.pallas_call(
        _fox_gate_kernel,
        grid=(batch,),
        in_specs=[
            pl.BlockSpec((seq, LANES), lambda b: (b, col_block)),
            pl.BlockSpec((SUBLANES, LANES), lambda b: (0, 0)),
        ],
        out_specs=pl.BlockSpec((1, SUBLANES, seq), lambda b: (b, 0, 0)),
        out_shape=jax.ShapeDtypeStruct((batch, SUBLANES, seq), F32),
        compiler_params=_cparams(("parallel",)),
        name="fox_gate",
    )(pa, bf_pad)


def _head_lane_masks(shape):
    lane = lax.broadcasted_iota(jnp.int32, shape, len(shape) - 1)
    first = lane < HEAD_DIM
    return first, jnp.logical_not(first)


def _causal_tile_order(nq):
    u = ATT_UNROLL
    assert nq % u == 0
    pairs = [(i, i) for i in range(nq)]
    left = {i: i for i in range(nq)}
    while any(left.values()):
        pick = sorted((i for i in left if left[i]), key=lambda i: -left[i])[:u]
        assert len(pick) == u, "tile list does not split into groups of distinct query blocks"
        for i in pick:
            pairs.append((i, left[i] - 1))
            left[i] -= 1
    return [p[0] for p in pairs], [p[1] for p in pairs]


def _split_heads(q_ref, v_ref, qh_ref, vh_ref):
    scale = HEAD_DIM ** -0.5
    t = ATT_BLOCK
    h0, h1 = _head_lane_masks((t, LANES))
    zero = jnp.zeros((t, LANES), BF16)
    for i in range(q_ref.shape[0] // t):
        q = q_ref[i * t:(i + 1) * t, :] * jnp.asarray(scale, BF16)
        v = v_ref[i * t:(i + 1) * t, :]
        qh_ref[i, 0:t, :] = jnp.where(h0, q, zero)
        qh_ref[i, t:2 * t, :] = jnp.where(h1, q, zero)
        vh_ref[i, 0:t, :] = jnp.where(h0, v, zero)
        vh_ref[i, t:2 * t, :] = jnp.where(h1, v, zero)


def _tile_waves(nq, tile_fn):
    qi, kj = _causal_tile_order(nq)
    state = [None] * nq
    for w0 in range(0, len(qi), ATT_UNROLL):
        wave = list(zip(qi[w0:w0 + ATT_UNROLL], kj[w0:w0 + ATT_UNROLL]))
        results = _lockstep([tile_fn(i, j, state[i], i == j) for i, j in wave])
        for (i, _), r in zip(wave, results):
            state[i] = r
    return state


def _foxt_kernel(q_ref, k_ref, v_ref, c_ref, o_ref, qt_ref, vt_ref, cb_ref):
    tq = tk = ATT_BLOCK
    seq = q_ref.shape[0]
    nq = seq // tq
    scale = HEAD_DIM ** -0.5
    lane0, lane1 = _head_lane_masks((tq, LANES))
    dim_head0 = lax.broadcasted_iota(jnp.int32, (LANES, tq), 0) < HEAD_DIM

    for i in range(nq):
        q = (q_ref[i * tq:(i + 1) * tq, :] * jnp.asarray(scale, BF16)).astype(F32)
        v = v_ref[i * tq:(i + 1) * tq, :].astype(F32)
        for h, lanes in enumerate((lane0, lane1)):
            qt_ref[i, :, h * tq:(h + 1) * tq] = jnp.where(lanes, q, 0.0).T.astype(BF16)
            vt_ref[i, :, h * tq:(h + 1) * tq] = jnp.where(lanes, v, 0.0).T.astype(BF16)
    for h in range(2):
        cb_ref[h] = jnp.broadcast_to(c_ref[0, 0, h:h + 1, :], (LANES, seq)).T

    def tile(i, j, state, diag):
        keys = slice(j * tk, (j + 1) * tk)
        s = jnp.dot(k_ref[keys, :], qt_ref[i], preferred_element_type=F32)
        yield
        wide = tq // LANES
        s = jnp.concatenate([s[:, 0:tq] - jnp.concatenate([cb_ref[0, keys, :]] * wide, axis=1),
                             s[:, tq:2 * tq] - jnp.concatenate([cb_ref[1, keys, :]] * wide, axis=1)],
                            axis=1)
        if diag:
            kpos = lax.broadcasted_iota(jnp.int32, s.shape, 0)
            qpos = lax.broadcasted_iota(jnp.int32, s.shape, 1) & (tq - 1)
            s = jnp.where(kpos <= qpos, s, -jnp.inf)
        col_max = jnp.max(s, axis=0, keepdims=True)
        if diag:
            m_new = col_max
        else:
            m_old, l_old, acc_old = state
            m_new = jnp.maximum(m_old, col_max)
            alpha = jnp.exp(m_old - m_new)
        p = jnp.exp(s - m_new)
        psum = jnp.sum(p, axis=0, keepdims=True)
        pb = p.astype(BF16)
        yield
        pv = jnp.dot(vt_ref[j], jnp.concatenate([pb[:, 0:tq], pb[:, tq:2 * tq]], axis=0),
                     preferred_element_type=F32)
        yield
        if diag:
            return m_new, psum, pv
        a_rows = jnp.where(dim_head0, alpha[:, 0:tq], alpha[:, tq:2 * tq])
        return m_new, alpha * l_old + psum, acc_old * a_rows + pv

    for i, (_, l, acc) in enumerate(_tile_waves(nq, tile)):
        l_rows = jnp.where(dim_head0, l[:, 0:tq], l[:, tq:2 * tq])
        o_ref[i * tq:(i + 1) * tq, :] = (acc / l_rows).T.astype(o_ref.dtype)


def _foxt(pb, col0, cpair, batch, seq):
    t = pb.shape[0]
    npair = HALF // LANES
    nq = seq // ATT_BLOCK
    return pl.pallas_call(
        _foxt_kernel,
        grid=(batch, npair),
        in_specs=[
            pl.BlockSpec((seq, LANES), lambda b, p: (b, col0 + p)),
            pl.BlockSpec((seq, LANES), lambda b, p: (b, col0 + npair + p)),
            pl.BlockSpec((seq, LANES), lambda b, p: (b, col0 + 2 * npair + p)),
            pl.BlockSpec((1, 1, 2, seq), lambda b, p: (b, p, 0, 0)),
        ],
        out_specs=pl.BlockSpec((seq, LANES), lambda b, p: (b, p)),
        out_shape=jax.ShapeDtypeStruct((t, HALF), BF16),
        scratch_shapes=[
            pltpu.VMEM((nq, LANES, 2 * ATT_BLOCK), BF16),
            pltpu.VMEM((nq, LANES, 2 * ATT_BLOCK), BF16),
            pltpu.VMEM((2, seq, LANES), F32),
        ],
        compiler_params=_cparams(("parallel", "parallel")),
        name="fox_attention",
    )(pb, pb, pb, cpair)


def _suffix_matrix(tk):
    r = lax.broadcasted_iota(jnp.int32, (2 * tk, 2 * tk), 0) % tk
    cidx = lax.broadcasted_iota(jnp.int32, (2 * tk, 2 * tk), 1)
    keep = jnp.logical_or(cidx >= tk, r > cidx)
    return jnp.where(keep, 1.0, 0.0).astype(BF16)


def _sb_kernel(q_ref, k_ref, v_ref, o_ref, qh_ref, vh_ref):
    tq = tk = ATT_BLOCK
    nq = q_ref.shape[0] // tq
    nsub = tk // LANES
    nt_dims = (((1,), (1,)), ((), ()))
    umat = _suffix_matrix(LANES)

    _split_heads(q_ref, v_ref, qh_ref, vh_ref)

    def tile(i, j, state, diag):
        z = lax.dot_general(qh_ref[i], k_ref[j * tk:(j + 1) * tk, :], nt_dims,
                            preferred_element_type=F32)
        yield
        logb = jnp.minimum(z, 0.0) - jnp.log(1.0 + jnp.exp2(jnp.abs(z) * -LOG2_E))
        lom = logb - z
        if diag:
            qpos = lax.broadcasted_iota(jnp.int32, z.shape, 0) & (tq - 1)
            kpos = lax.broadcasted_iota(jnp.int32, z.shape, 1)
            mask = kpos < qpos
            lom = jnp.where(mask, lom, 0.0)
        parts = []
        for c in reversed(range(nsub)):
            part = lom[:, c * LANES:(c + 1) * LANES]
            hi = part.astype(BF16)
            lo = (part - hi.astype(F32)).astype(BF16)
            parts.append(jnp.concatenate([hi, lo], axis=1))
        yield
        r = jnp.dot(jnp.concatenate(parts, axis=0), umat, preferred_element_type=F32)
        yield
        carry = jnp.zeros((2 * tq, LANES), F32) if diag else state[0]
        suffix = [None] * nsub
        for n_c, c in enumerate(reversed(range(nsub))):
            rc = r[n_c * 2 * tq:(n_c + 1) * 2 * tq]
            suffix[c] = rc[:, 0:LANES] + carry
            carry = carry + rc[:, LANES:2 * LANES]
        w = jnp.exp(logb + jnp.concatenate(suffix, axis=1))
        if diag:
            w = jnp.where(mask, w, 0.0)
        wb = w.astype(BF16)
        yield
        wv = jnp.dot(jnp.concatenate([wb[0:tq], wb[tq:2 * tq]], axis=1), vh_ref[j],
                     preferred_element_type=F32)
        yield
        return carry, (wv if diag else state[1] + wv)

    for i, (_, acc) in enumerate(_tile_waves(nq, tile)):
        o_ref[i * tq:(i + 1) * tq, :] = acc.astype(o_ref.dtype)


def _sb(pb, col0, batch, seq):
    t = pb.shape[0]
    npair = HALF // LANES
    nq = seq // ATT_BLOCK
    return pl.pallas_call(
        _sb_kernel,
        grid=(batch, npair),
        in_specs=[
            pl.BlockSpec((seq, LANES), lambda b, p: (b, col0 + p)),
            pl.BlockSpec((seq, LANES), lambda b, p: (b, col0 + npair + p)),
            pl.BlockSpec((seq, LANES), lambda b, p: (b, col0 + 2 * npair + p)),
        ],
        out_specs=pl.BlockSpec((seq, LANES), lambda b, p: (b, p)),
        out_shape=jax.ShapeDtypeStruct((t, HALF), BF16),
        scratch_shapes=[
            pltpu.VMEM((nq, 2 * ATT_BLOCK, LANES), BF16),
            pltpu.VMEM((nq, 2 * ATT_BLOCK, LANES), BF16),
        ],
        compiler_params=_cparams(("parallel", "parallel")),
        name="sb_attention",
    )(pb, pb, pb)


def _lru_kernel(x_ref, gate_ref, cw_ref, cb_ref, wg_ref, bg_ref, lam_ref, o_ref,
                xe_ref, a_ref, u_ref, h_ref, y_ref):
    step = pl.program_id(0)
    rows, w = x_ref.shape
    hist = (CONV_WIDTH - 1) * SUBLANES

    @pl.when(step == 0)
    def _():
        xe_ref[0:hist, :] = jnp.zeros((hist, w), F32)
        h_ref[...] = jnp.zeros_like(h_ref)

    @pl.when(step > 0)
    def _():
        xe_ref[0:hist, :] = xe_ref[rows:rows + hist, :]

    x = x_ref[...]
    xe_ref[hist:hist + rows, :] = x
    xc = cb_ref[...] + cw_ref[CONV_WIDTH - 1:CONV_WIDTH, :] * x
    for d in range(1, CONV_WIDTH):
        back = hist - d * SUBLANES
        xc = xc + cw_ref[CONV_WIDTH - 1 - d:CONV_WIDTH - d, :] * xe_ref[back:back + rows, :]

    gates = jnp.dot(xc.astype(BF16), wg_ref[...], preferred_element_type=F32) + bg_ref[...]
    r = jax.nn.sigmoid(gates[:, 0:w])
    ig = jax.nn.sigmoid(gates[:, w:2 * w])
    lam = lam_ref[...]
    softplus_neg_lam = jnp.maximum(-lam, 0.0) + jnp.log1p(jnp.exp(-jnp.abs(lam)))
    log_a = -LRU_C * r * softplus_neg_lam
    a_ref[...] = jnp.exp(log_a)
    th = jnp.tanh(-log_a)
    u_ref[...] = jnp.sqrt(2.0 * th / (1.0 + th)) * (ig * xc)

    h = h_ref[...]
    for t0 in range(0, rows, SUBLANES):
        h = a_ref[t0:t0 + SUBLANES, :] * h + u_ref[t0:t0 + SUBLANES, :]
        u_ref[t0:t0 + SUBLANES, :] = h
    h_ref[...] = h
    y = u_ref[...] * jax.nn.gelu(gate_ref[...], approximate=True)
    nseq, steps, _ = o_ref.shape
    for c in range(w // LANES):
        y_ref[c] = y[:, c * LANES:(c + 1) * LANES]
    for b in range(nseq):
        for c in range(w // LANES):
            o_ref[b, :, c * LANES:(c + 1) * LANES] = (
                y_ref[c, pl.ds(b, steps, stride=nseq), :].astype(o_ref.dtype))


def _lru(pa, cw, cb, wg, bg, lam, batch, seq, rows=512):
    t = pa.shape[0]
    w = HALF
    steps = rows // batch
    fixed = lambda s: (0, 0)
    return pl.pallas_call(
        _lru_kernel,
        grid=(t // rows,),
        in_specs=[
            pl.BlockSpec((rows, w), lambda s: (s, 0)),
            pl.BlockSpec((rows, w), lambda s: (s, 1)),
            pl.BlockSpec((CONV_WIDTH, w), fixed),
            pl.BlockSpec((1, w), fixed),
            pl.BlockSpec((w, 2 * w), fixed),
            pl.BlockSpec((1, 2 * w), fixed),
            pl.BlockSpec((1, w), fixed),
        ],
        out_specs=pl.BlockSpec((batch, steps, w), lambda s: (0, s, 0)),
        out_shape=jax.ShapeDtypeStruct((batch, seq, w), BF16),
        scratch_shapes=[
            pltpu.VMEM((rows + (CONV_WIDTH - 1) * SUBLANES, w), F32),
            pltpu.VMEM((rows, w), F32),
            pltpu.VMEM((rows, w), F32),
            pltpu.VMEM((SUBLANES, w), F32),
            pltpu.VMEM((w // LANES, rows, LANES), F32),
        ],
        compiler_params=_cparams(("arbitrary",)),
        name="rg_lru",
    )(pa, pa, cw, cb, wg, bg, lam).reshape(t, w)


def _block_diag(wb):
    n, bd, _ = wb.shape
    eye = jnp.eye(n, dtype=wb.dtype)
    return (eye[:, None, :, None] * wb[:, :, None, :]).reshape(n * bd, n * bd)


def kernel(x, ev_w_in, ev_fox_bf, hgrn_lb, ev_hgrn_norm_g, ev_w_out, od_w_in, od_conv_w, od_conv_b,
           od_gate_a_w, od_gate_a_b, od_gate_x_w, od_gate_x_b, od_lru_lambda, od_w_out, ffn_w13,
           ffn_w2, ln_g, ln_b):
    batch, seq, d = x.shape
    t = batch * seq
    xf = x.reshape(t, d)
    lb_all = jnp.cumsum(jax.nn.softmax(hgrn_lb.astype(F32), axis=0), axis=0)
    h = HALF

    nfox = ev_fox_bf.shape[1]
    cols_a = [(0, 0, 2 * h), (2 * h, 3 * h, h), (3 * h, 7 * h, nfox)]
    cols_b = [(0, 2 * h, h), (h, 4 * h, 3 * h)]
    pa, pb = _proj(xf, ev_w_in[0], cols_a, cols_b)
    ya = _hgrn(pa, pb, lb_all[0].reshape(1, h), ev_hgrn_norm_g[0].reshape(1, HGRN_DIM), batch, seq)
    bf_pad = jnp.broadcast_to(ev_fox_bf[0].astype(F32).reshape(nfox, 1), (nfox, LANES))
    cum = _fox_gate(pa, 3 * h // LANES, bf_pad, batch, seq)
    yb = _foxt(pb, h // LANES, cum.reshape(batch, nfox // 2, 2, seq), batch, seq)
    lng = ln_g.astype(F32).reshape(2 * DEPTH, 1, d)
    lnb = ln_b.astype(F32).reshape(2 * DEPTH, 1, d)
    w13_bf, w2_bf = ffn_w13.astype(BF16), ffn_w2.astype(BF16)
    x2 = _layer_tail(ya, yb, xf, ev_w_out.astype(BF16), lng, lnb, w13_bf, w2_bf, layer=0, slot=0)

    assert batch == SUBLANES
    pa, pb = _proj(x2, od_w_in[0], [(0, 0, 2 * h)], [(0, 2 * h, 3 * h)], a_time_major=(batch, seq))
    wg = jnp.concatenate([_block_diag(od_gate_a_w[0]), _block_diag(od_gate_x_w[0])], axis=1).astype(BF16)
    bg = jnp.concatenate([od_gate_a_b[0], od_gate_x_b[0]]).reshape(1, 2 * h).astype(F32)
    yc = _lru(pa, od_conv_w[0], od_conv_b[0].reshape(1, h), wg, bg, od_lru_lambda[0].reshape(1, h),
              batch, seq)
    yd = _sb(pb, 0, batch, seq)
    x4 = _layer_tail(yc, yd, x2, od_w_out.astype(BF16), lng, lnb, w13_bf, w2_bf, layer=1, slot=0)
    return x4.reshape(batch, seq, d)
```

```python
import functools

import jax
import jax.numpy as jnp
from jax import lax
from jax.experimental import pallas as pl
from jax.experimental.pallas import tpu as pltpu

F32 = jnp.float32
BF16 = jnp.bfloat16

SUBLANES = 8
LANES = 128

D_MODEL = 1024
HALF = D_MODEL // 2
HEAD_DIM = 64
HGRN_DIM = 128
LRU_BLOCKS = 8
LRU_C = 8.0
CONV_WIDTH = 4
DEPTH = 2
ALPHA = (2 * DEPTH) ** 0.25
EPS = 1e-5
D_FF = 2816
LOG2_E = 1.4426950408889634

FFN_CHUNK = 256
TAIL_ROW_PARTS = 4
HGRN_CHUNK = 128
HGRN_UNROLL = 4
ATT_BLOCK = 256
FOX_WAVE = 4
SB_WAVE = 2
VMEM_LIMIT = 56 * 1024 * 1024


def _cparams(sem):
    return pltpu.CompilerParams(dimension_semantics=sem, vmem_limit_bytes=VMEM_LIMIT)


def _proj_kernel(cols_a, cols_b, x_ref, w_ref, oa_ref, ob_ref, wa_ref, wb_ref, *scratch):
    @pl.when(pl.program_id(0) == 0)
    def _():
        for dst_ref, cols in ((wa_ref, cols_a), (wb_ref, cols_b)):
            for dst, src, n in cols:
                if n % LANES:
                    slot = -(-n // LANES) * LANES
                    dst_ref[:, dst:dst + slot] = jnp.zeros((dst_ref.shape[0], slot), BF16)
                dst_ref[:, dst:dst + n] = w_ref[:, src:src + n].astype(BF16)

    if len(x_ref.shape) == 2:
        xb = x_ref[...].astype(BF16)
        oa_ref[...] = jnp.dot(xb, wa_ref[...], preferred_element_type=F32)
        ob_ref[...] = jnp.dot(xb, wb_ref[...], preferred_element_type=F32).astype(BF16)
    else:
        nseq, steps, d = x_ref.shape
        xt_ref = scratch[0]
        for b in range(nseq):
            for c in range(d // LANES):
                xt_ref[c, pl.ds(b, steps, stride=nseq), :] = x_ref[b, :, c * LANES:(c + 1) * LANES]
        xt = jnp.concatenate([xt_ref[c] for c in range(d // LANES)], axis=1)
        oa_ref[...] = jnp.dot(xt.astype(BF16), wa_ref[...], preferred_element_type=F32)
        xb = x_ref[...].reshape(nseq * steps, d).astype(BF16)
        ob = jnp.dot(xb, wb_ref[...], preferred_element_type=F32).astype(BF16)
        ob_ref[...] = ob.reshape(ob_ref.shape)


def _proj(x, w, cols_a, cols_b, a_time_major=None, tm=512):
    t, d = x.shape

    def width(cols):
        return max(dst + -(-n // LANES) * LANES for dst, _, n in cols)

    na, nb = width(cols_a), width(cols_b)
    row = lambda i: (i, 0)
    scratch = [pltpu.VMEM((d, na), BF16), pltpu.VMEM((d, nb), BF16)]
    if a_time_major is None:
        x_spec, b_spec, b_shape = pl.BlockSpec((tm, d), row), pl.BlockSpec((tm, nb), row), (t, nb)
    else:
        batch, seq = a_time_major
        steps = tm // batch
        x = x.reshape(batch, seq, d)
        x_spec = pl.BlockSpec((batch, steps, d), lambda i: (0, i, 0))
        b_spec, b_shape = pl.BlockSpec((batch, steps, nb), lambda i: (0, i, 0)), (batch, seq, nb)
        scratch.append(pltpu.VMEM((d // LANES, tm, LANES), F32))
    oa, ob = pl.pallas_call(
        functools.partial(_proj_kernel, cols_a, cols_b),
        grid=(t // tm,),
        in_specs=[x_spec, pl.BlockSpec(w.shape, lambda i: (0, 0), pipeline_mode=pl.Buffered(1))],
        out_specs=[pl.BlockSpec((tm, na), row), b_spec],
        out_shape=[jax.ShapeDtypeStruct((t, na), F32), jax.ShapeDtypeStruct(b_shape, BF16)],
        scratch_shapes=scratch,
        compiler_params=_cparams(("arbitrary",)),
        name="in_proj",
    )(x, w)
    return oa, ob.reshape(t, nb)


def _lockstep(gens):
    results = [None] * len(gens)
    live = list(range(len(gens)))
    while live:
        for k in list(live):
            try:
                next(gens[k])
            except StopIteration as done:
                results[k] = done.value
                live.remove(k)
    return results


def _layer_norm_rows(y, g, b):
    mu = jnp.mean(y, axis=-1, keepdims=True)
    yc = y - mu
    var = jnp.mean(yc * yc, axis=-1, keepdims=True)
    return yc * lax.rsqrt(var + EPS) * g + b


def _layer_tail_kernel(ya_ref, yb_ref, x_ref, wo_ref, g1_ref, b1_ref, w13_ref, w2_ref, g2_ref, b2_ref,
                       o_ref, xb_ref, h_ref):
    tf = FFN_CHUNK
    nf = w2_ref.shape[0] // tf
    tm = x_ref.shape[0]
    n_parts = TAIL_ROW_PARTS

    def part(p):
        rows = slice(p * tm // n_parts, (p + 1) * tm // n_parts)
        mix = jnp.dot(ya_ref[rows, :], wo_ref[0:HALF, :], preferred_element_type=F32)
        mix = mix + jnp.dot(yb_ref[rows, :], wo_ref[HALF:D_MODEL, :], preferred_element_type=F32)
        yield
        x1 = _layer_norm_rows(ALPHA * x_ref[rows, :] + mix, g1_ref[...], b1_ref[...])
        o_ref[rows, :] = x1
        xb_ref[rows, :] = x1.astype(BF16)
        for c in range(nf):
            xb = xb_ref[rows, :]
            a = jnp.dot(xb, w13_ref[:, c * tf:(c + 1) * tf], preferred_element_type=F32)
            b = jnp.dot(xb, w13_ref[:, (nf + c) * tf:(nf + c + 1) * tf], preferred_element_type=F32)
            yield
            h_ref[rows, c * tf:(c + 1) * tf] = (a * jax.nn.sigmoid(a) * b).astype(BF16)
        for n0 in range(0, D_MODEL, tf):
            cols = slice(n0, n0 + tf)
            down = jnp.dot(h_ref[rows, :], w2_ref[:, cols], preferred_element_type=F32)
            yield
            o_ref[rows, cols] = ALPHA * o_ref[rows, cols] + down
        o_ref[rows, :] = _layer_norm_rows(o_ref[rows, :], g2_ref[...], b2_ref[...])

    _lockstep([part(p) for p in range(n_parts)])


def _layer_tail(ya, yb, x, w_out, ln_g, ln_b, w13, w2, layer, slot, tm=1024):
    t = x.shape[0]
    nf = D_FF // FFN_CHUNK
    assert nf % 2 == 1 and nf * FFN_CHUNK == D_FF and D_MODEL % FFN_CHUNK == 0
    d = D_MODEL
    resident = pl.Buffered(1)
    row = lambda i: (i, 0)
    return pl.pallas_call(
        _layer_tail_kernel,
        grid=(t // tm,),
        in_specs=[
            pl.BlockSpec((tm, HALF), row),
            pl.BlockSpec((tm, HALF), row),
            pl.BlockSpec((tm, d), row),
            pl.BlockSpec((None, d, d), lambda i: (slot, 0, 0), pipeline_mode=resident),
            pl.BlockSpec((None, 1, d), lambda i: (2 * layer, 0, 0)),
            pl.BlockSpec((None, 1, d), lambda i: (2 * layer, 0, 0)),
            pl.BlockSpec((None, d, 2 * D_FF), lambda i: (layer, 0, 0), pipeline_mode=resident),
            pl.BlockSpec((None, D_FF, d), lambda i: (layer, 0, 0), pipeline_mode=resident),
            pl.BlockSpec((None, 1, d), lambda i: (2 * layer + 1, 0, 0)),
            pl.BlockSpec((None, 1, d), lambda i: (2 * layer + 1, 0, 0)),
        ],
        out_specs=pl.BlockSpec((tm, d), row),
        out_shape=jax.ShapeDtypeStruct((t, d), F32),
        scratch_shapes=[pltpu.VMEM((tm, d), BF16), pltpu.VMEM((tm, D_FF), BF16)],
        compiler_params=_cparams(("parallel",)),
        name="layer_tail",
    )(ya, yb, x, w_out, ln_g, ln_b, w13, w2, ln_g, ln_b)


def _tile_rows(n):
    return lax.broadcasted_iota(jnp.int32, (SUBLANES, n), 0)


def _bcast_row(tile, r):
    return jnp.broadcast_to(tile[r:r + 1, :], tile.shape)


def _segmented_scans(x):
    n = len(x)
    row = _tile_rows(x[0].shape[1])
    zero = jnp.zeros_like(x[0])
    r3 = row & 3
    up4 = (row & 4) != 0

    p = {1: list(x)}
    p[2] = [t + jnp.where((row & 1) == 1, pltpu.roll(t, 1, 0), 0.0) for t in x]
    p[4] = [t + jnp.where(r3 == 2, pltpu.roll(t, 1, 0),
                          jnp.where(r3 == 3, pltpu.roll(t, 2, 0), 0.0)) for t in p[2]]
    p[8] = [t + jnp.where(up4, _bcast_row(t, 3), 0.0) for t in p[4]]

    e = {1: [zero] * n}
    e[2] = [jnp.where((row & 1) == 0, pltpu.roll(t, SUBLANES - 1, 0), 0.0) for t in x]
    r2 = [a + b for a, b in zip(e[2], x)]
    e[4] = [t + jnp.where(r3 == 1, pltpu.roll(r, SUBLANES - 1, 0),
                          jnp.where(r3 == 0, pltpu.roll(r, SUBLANES - 2, 0), 0.0))
            for t, r in zip(e[2], r2)]
    r4 = [a + b for a, b in zip(e[4], x)]
    e[8] = [t + jnp.where(up4, 0.0, _bcast_row(r, 4)) for t, r in zip(e[4], r4)]

    m = SUBLANES
    while m < SUBLANES * n:
        nt = m // SUBLANES
        pn, en = [], []
        for g in range(n // (2 * nt)):
            lo = slice(2 * nt * g, 2 * nt * g + nt)
            hi = slice(2 * nt * g + nt, 2 * nt * (g + 1))
            tot = _bcast_row(p[m][lo][-1], SUBLANES - 1)
            pn += p[m][lo] + [t + tot for t in p[m][hi]]
            first = e[m][hi][0][0:1, :] + x[hi][0][0:1, :]
            tot_hi = jnp.broadcast_to(first, zero.shape)
            en += [t + tot_hi for t in e[m][lo]] + e[m][hi]
        p[2 * m], e[2 * m] = pn, en
        m *= 2
    return p, e


def _level_map(c):
    t = lax.broadcasted_iota(jnp.int32, (c, c), 0)
    s = lax.broadcasted_iota(jnp.int32, (c, c), 1)
    x = t ^ s
    lv = jnp.full((c, c), -1, jnp.int32)
    m = 1
    while m < c:
        lv = lv + (x >= m).astype(jnp.int32)
        m *= 2
    return jnp.where(t < s, -2, lv)


def _hgrn_kernel(q_ref, f_ref, g_ref, v_ref, lb_ref, ng_ref, o_ref):
    c = HGRN_CHUNK
    n_tiles = c // SUBLANES
    n_chunks = q_ref.shape[0] // c
    lb = lb_ref[...]
    one_m_lb = 1.0 - lb
    ng = ng_ref[...]
    lv = _level_map(c)

    def chunk(ci, state):
        rows = slice(ci * c, (ci + 1) * c)
        z = f_ref[rows, :]
        q = q_ref[rows, :]
        v = v_ref[rows, :]
        gg = g_ref[rows, :]
        ez = jnp.exp(-jnp.abs(z))
        rz = 1.0 / (1.0 + ez)
        pos = z >= 0.0
        sig = jnp.where(pos, rz, ez * rz)
        nsig = jnp.where(pos, ez * rz, rz)
        lf = jnp.log2(lb + one_m_lb * sig)
        kk = one_m_lb * nsig

        lf_tiles = [lf[SUBLANES * i:SUBLANES * (i + 1), :] for i in range(n_tiles)]
        p, e = _segmented_scans(lf_tiles)

        def cat(ts):
            return jnp.concatenate(ts, axis=0)

        nt_dims = (((1,), (1,)), ((), ()))
        qb, kb = q.astype(BF16), kk.astype(BF16)
        yield
        scores = jnp.where(
            lv == -1, lax.dot_general(qb, kb, nt_dims, preferred_element_type=F32), 0.0)
        m, idx = 1, 0
        while m < c:
            qm = (q * jnp.exp2(cat(p[m]))).astype(BF16)
            km = kb if m == 1 else (kk * jnp.exp2(cat(e[m]))).astype(BF16)
            yield
            sm = lax.dot_general(qm, km, nt_dims, preferred_element_type=F32)
            scores = jnp.where(lv == idx, sm, scores)
            m *= 2
            idx += 1

        b = cat(p[c])
        qc = (q * jnp.exp2(b)).astype(BF16)
        kc = (kk * jnp.exp2(cat(e[c]))).astype(BF16)
        decay = jnp.exp2(b[c - 1:c, :])
        sb = scores.astype(BF16)
        yield
        kv = lax.dot_general(kc, v, (((0,), (0,)), ((), ())), preferred_element_type=F32)
        o = jnp.dot(sb, v, preferred_element_type=F32)
        yield
        st = state[0]
        o = o + jnp.dot(qc, st.astype(BF16), preferred_element_type=F32)
        state[0] = st * jnp.broadcast_to(decay, st.shape).T + kv
        yield
        ms = jnp.mean(o * o, axis=-1, keepdims=True)
        on = o * lax.rsqrt(ms + EPS) * ng
        o_ref[rows, :] = (on * (gg * jax.nn.sigmoid(gg))).astype(o_ref.dtype)

    state = [jnp.zeros((HGRN_DIM, HGRN_DIM), F32)]
    for c0 in range(0, n_chunks, HGRN_UNROLL):
        _lockstep([chunk(c0 + k, state) for k in range(HGRN_UNROLL)])


def _hgrn(pa, pb, lb, ng, batch, seq):
    t = pa.shape[0]
    nh = HALF // HGRN_DIM
    return pl.pallas_call(
        _hgrn_kernel,
        grid=(batch, nh),
        in_specs=[
            pl.BlockSpec((seq, HGRN_DIM), lambda b, h: (b, h)),
            pl.BlockSpec((seq, HGRN_DIM), lambda b, h: (b, nh + h)),
            pl.BlockSpec((seq, HGRN_DIM), lambda b, h: (b, 2 * nh + h)),
            pl.BlockSpec((seq, HGRN_DIM), lambda b, h: (b, h)),
            pl.BlockSpec((1, HGRN_DIM), lambda b, h: (0, h)),
            pl.BlockSpec((1, HGRN_DIM), lambda b, h: (0, 0)),
        ],
        out_specs=pl.BlockSpec((seq, HGRN_DIM), lambda b, h: (b, h)),
        out_shape=jax.ShapeDtypeStruct((t, HALF), BF16),
        compiler_params=_cparams(("parallel", "parallel")),
        name="hgrn2",
    )(pa, pa, pa, pb, lb, ng)


def _fox_gate_kernel(f_ref, bf_ref, c_ref):
    s = f_ref.shape[0]
    ft = f_ref[...].T[0:SUBLANES, :]
    x = ft + bf_ref[...][:, 0:1]
    lf = jnp.minimum(x, 0.0) - jnp.log1p(jnp.exp(-jnp.abs(x)))
    lane = lax.broadcasted_iota(jnp.int32, lf.shape, 1)
    d = 1
    while d < s:
        lf = lf + jnp.where(lane >= d, pltpu.roll(lf, d, 1), 0.0)
        d *= 2
    c_ref[0] = lf


def _fox_gate(pa, col_block, bf_pad, batch, seq):
    return pl.pallas_call(
        _fox_gate_kernel,
        grid=(batch,),
        in_specs=[
            pl.BlockSpec((seq, LANES), lambda b: (b, col_block)),
            pl.BlockSpec((SUBLANES, LANES), lambda b: (0, 0)),
        ],
        out_specs=pl.BlockSpec((1, SUBLANES, seq), lambda b: (b, 0, 0)),
        out_shape=jax.ShapeDtypeStruct((batch, SUBLANES, seq), F32),
        compiler_params=_cparams(("parallel",)),
        name="fox_gate",
    )(pa, bf_pad)


def _head_lane_masks(shape):
    lane = lax.broadcasted_iota(jnp.int32, shape, len(shape) - 1)
    first = lane < HEAD_DIM
    return first, jnp.logical_not(first)


def _causal_tile_order(nq, u):
    assert nq % u == 0
    pairs = [(i, i) for i in range(nq)]
    left = {i: i for i in range(nq)}
    while any(left.values()):
        pick = sorted((i for i in left if left[i]), key=lambda i: -left[i])[:u]
        assert len(pick) == u, "tile list does not split into groups of distinct query blocks"
        for i in pick:
            pairs.append((i, left[i] - 1))
            left[i] -= 1
    return [p[0] for p in pairs], [p[1] for p in pairs]


def _split_heads(q_ref, v_ref, qh_ref, vh_ref):
    scale = HEAD_DIM ** -0.5
    t = ATT_BLOCK
    h0, h1 = _head_lane_masks((t, LANES))
    zero = jnp.zeros((t, LANES), BF16)
    for i in range(q_ref.shape[0] // t):
        q = q_ref[i * t:(i + 1) * t, :] * jnp.asarray(scale, BF16)
        v = v_ref[i * t:(i + 1) * t, :]
        qh_ref[i, 0:t, :] = jnp.where(h0, q, zero)
        qh_ref[i, t:2 * t, :] = jnp.where(h1, q, zero)
        vh_ref[i, 0:t, :] = jnp.where(h0, v, zero)
        vh_ref[i, t:2 * t, :] = jnp.where(h1, v, zero)


def _tile_waves(nq, tile_fn, u):
    qi, kj = _causal_tile_order(nq, u)
    state = [None] * nq
    for w0 in range(0, len(qi), u):
        wave = list(zip(qi[w0:w0 + u], kj[w0:w0 + u]))
        results = _lockstep([tile_fn(i, j, state[i], i == j) for i, j in wave])
        for (i, _), r in zip(wave, results):
            state[i] = r
    return state


def _foxt_kernel(q_ref, k_ref, v_ref, c_ref, o_ref, qt_ref, vt_ref, cb_ref):
    tq = tk = ATT_BLOCK
    seq = q_ref.shape[0]
    nq = seq // tq
    scale = HEAD_DIM ** -0.5
    lane0, lane1 = _head_lane_masks((tq, LANES))
    dim_head0 = lax.broadcasted_iota(jnp.int32, (LANES, tq), 0) < HEAD_DIM

    for i in range(nq):
        q = (q_ref[i * tq:(i + 1) * tq, :] * jnp.asarray(scale, BF16)).astype(F32)
        v = v_ref[i * tq:(i + 1) * tq, :].astype(F32)
        for h, lanes in enumerate((lane0, lane1)):
            qt_ref[i, :, h * tq:(h + 1) * tq] = jnp.where(lanes, q, 0.0).T.astype(BF16)
            vt_ref[i, :, h * tq:(h + 1) * tq] = jnp.where(lanes, v, 0.0).T.astype(BF16)
    for h in range(2):
        cb_ref[h] = jnp.broadcast_to(c_ref[0, 0, h:h + 1, :], (LANES, seq)).T

    def tile(i, j, state, diag):
        keys = slice(j * tk, (j + 1) * tk)
        s = jnp.dot(k_ref[keys, :], qt_ref[i], preferred_element_type=F32)
        yield
        wide = tq // LANES
        s = jnp.concatenate([s[:, 0:tq] - jnp.concatenate([cb_ref[0, keys, :]] * wide, axis=1),
                             s[:, tq:2 * tq] - jnp.concatenate([cb_ref[1, keys, :]] * wide, axis=1)],
                            axis=1)
        if diag:
            kpos = lax.broadcasted_iota(jnp.int32, s.shape, 0)
            qpos = lax.broadcasted_iota(jnp.int32, s.shape, 1) & (tq - 1)
            s = jnp.where(kpos <= qpos, s, -jnp.inf)
        col_max = jnp.max(s, axis=0, keepdims=True)
        if diag:
            m_new = col_max
        else:
            m_old, l_old, acc_old = state
            m_new = jnp.maximum(m_old, col_max)
            alpha = jnp.exp(m_old - m_new)
        p = jnp.exp(s - m_new)
        psum = jnp.sum(p, axis=0, keepdims=True)
        pb = p.astype(BF16)
        yield
        pv = jnp.dot(vt_ref[j], jnp.concatenate([pb[:, 0:tq], pb[:, tq:2 * tq]], axis=0),
                     preferred_element_type=F32)
        yield
        if diag:
            return m_new, psum, pv
        a_rows = jnp.where(dim_head0, alpha[:, 0:tq], alpha[:, tq:2 * tq])
        return m_new, alpha * l_old + psum, acc_old * a_rows + pv

    for i, (_, l, acc) in enumerate(_tile_waves(nq, tile, FOX_WAVE)):
        l_rows = jnp.where(dim_head0, l[:, 0:tq], l[:, tq:2 * tq])
        o_ref[i * tq:(i + 1) * tq, :] = (acc / l_rows).T.astype(o_ref.dtype)


def _foxt(pb, col0, cpair, batch, seq):
    t = pb.shape[0]
    npair = HALF // LANES
    nq = seq // ATT_BLOCK
    return pl.pallas_call(
        _foxt_kernel,
        grid=(batch, npair),
        in_specs=[
            pl.BlockSpec((seq, LANES), lambda b, p: (b, col0 + p)),
            pl.BlockSpec((seq, LANES), lambda b, p: (b, col0 + npair + p)),
            pl.BlockSpec((seq, LANES), lambda b, p: (b, col0 + 2 * npair + p)),
            pl.BlockSpec((1, 1, 2, seq), lambda b, p: (b, p, 0, 0)),
        ],
        out_specs=pl.BlockSpec((seq, LANES), lambda b, p: (b, p)),
        out_shape=jax.ShapeDtypeStruct((t, HALF), BF16),
        scratch_shapes=[
            pltpu.VMEM((nq, LANES, 2 * ATT_BLOCK), BF16),
            pltpu.VMEM((nq, LANES, 2 * ATT_BLOCK), BF16),
            pltpu.VMEM((2, seq, LANES), F32),
        ],
        compiler_params=_cparams(("parallel", "parallel")),
        name="fox_attention",
    )(pb, pb, pb, cpair)


def _suffix_matrix(tk):
    r = lax.broadcasted_iota(jnp.int32, (2 * tk, 2 * tk), 0) % tk
    cidx = lax.broadcasted_iota(jnp.int32, (2 * tk, 2 * tk), 1)
    keep = jnp.logical_or(cidx >= tk, r > cidx)
    return jnp.where(keep, 1.0, 0.0).astype(BF16)


def _sb_kernel(q_ref, k_ref, v_ref, o_ref, qh_ref, vh_ref):
    tq = tk = ATT_BLOCK
    nq = q_ref.shape[0] // tq
    nsub = tk // LANES
    nt_dims = (((1,), (1,)), ((), ()))
    umat = _suffix_matrix(LANES)

    _split_heads(q_ref, v_ref, qh_ref, vh_ref)

    def tile(i, j, state, diag):
        z = lax.dot_general(qh_ref[i], k_ref[j * tk:(j + 1) * tk, :], nt_dims,
                            preferred_element_type=F32)
        yield
        logb = jnp.minimum(z, 0.0) - jnp.log(1.0 + jnp.exp2(jnp.abs(z) * -LOG2_E))
        lom = logb - z
        if diag:
            qpos = lax.broadcasted_iota(jnp.int32, z.shape, 0) & (tq - 1)
            kpos = lax.broadcasted_iota(jnp.int32, z.shape, 1)
            mask = kpos < qpos
            lom = jnp.where(mask, lom, 0.0)
        parts = []
        for c in reversed(range(nsub)):
            part = lom[:, c * LANES:(c + 1) * LANES]
            hi = part.astype(BF16)
            lo = (part - hi.astype(F32)).astype(BF16)
            parts.append(jnp.concatenate([hi, lo], axis=1))
        yield
        r = jnp.dot(jnp.concatenate(parts, axis=0), umat, preferred_element_type=F32)
        yield
        carry = jnp.zeros((2 * tq, LANES), F32) if diag else state[0]
        suffix = [None] * nsub
        for n_c, c in enumerate(reversed(range(nsub))):
            rc = r[n_c * 2 * tq:(n_c + 1) * 2 * tq]
            suffix[c] = rc[:, 0:LANES] + carry
            carry = carry + rc[:, LANES:2 * LANES]
        w = jnp.exp(logb + jnp.concatenate(suffix, axis=1))
        if diag:
            w = jnp.where(mask, w, 0.0)
        wb = w.astype(BF16)
        yield
        wv = jnp.dot(jnp.concatenate([wb[0:tq], wb[tq:2 * tq]], axis=1), vh_ref[j],
                     preferred_element_type=F32)
        yield
        return carry, (wv if diag else state[1] + wv)

    for i, (_, acc) in enumerate(_tile_waves(nq, tile, SB_WAVE)):
        o_ref[i * tq:(i + 1) * tq, :] = acc.astype(o_ref.dtype)


def _sb(pb, col0, batch, seq):
    t = pb.shape[0]
    npair = HALF // LANES
    nq = seq // ATT_BLOCK
    return pl.pallas_call(
        _sb_kernel,
        grid=(batch, npair),
        in_specs=[
            pl.BlockSpec((seq, LANES), lambda b, p: (b, col0 + p)),
            pl.BlockSpec((seq, LANES), lambda b, p: (b, col0 + npair + p)),
            pl.BlockSpec((seq, LANES), lambda b, p: (b, col0 + 2 * npair + p)),
        ],
        out_specs=pl.BlockSpec((seq, LANES), lambda b, p: (b, p)),
        out_shape=jax.ShapeDtypeStruct((t, HALF), BF16),
        scratch_shapes=[
            pltpu.VMEM((nq, 2 * ATT_BLOCK, LANES), BF16),
            pltpu.VMEM((nq, 2 * ATT_BLOCK, LANES), BF16),
        ],
        compiler_params=_cparams(("parallel", "parallel")),
        name="sb_attention",
    )(pb, pb, pb)


def _lru_kernel(x_ref, gate_ref, cw_ref, cb_ref, wg_ref, bg_ref, lam_ref, o_ref,
                xe_ref, a_ref, u_ref, h_ref, y_ref):
    step = pl.program_id(0)
    rows, w = x_ref.shape
    hist = (CONV_WIDTH - 1) * SUBLANES

    @pl.when(step == 0)
    def _():
        xe_ref[0:hist, :] = jnp.zeros((hist, w), F32)
        h_ref[...] = jnp.zeros_like(h_ref)

    @pl.when(step > 0)
    def _():
        xe_ref[0:hist, :] = xe_ref[rows:rows + hist, :]

    x = x_ref[...]
    xe_ref[hist:hist + rows, :] = x
    xc = cb_ref[...] + cw_ref[CONV_WIDTH - 1:CONV_WIDTH, :] * x
    for d in range(1, CONV_WIDTH):
        back = hist - d * SUBLANES
        xc = xc + cw_ref[CONV_WIDTH - 1 - d:CONV_WIDTH - d, :] * xe_ref[back:back + rows, :]

    gates = jnp.dot(xc.astype(BF16), wg_ref[...], preferred_element_type=F32) + bg_ref[...]
    r = jax.nn.sigmoid(gates[:, 0:w])
    ig = jax.nn.sigmoid(gates[:, w:2 * w])
    lam = lam_ref[...]
    softplus_neg_lam = jnp.maximum(-lam, 0.0) + jnp.log1p(jnp.exp(-jnp.abs(lam)))
    log_a = -LRU_C * r * softplus_neg_lam
    a_ref[...] = jnp.exp(log_a)
    th = jnp.tanh(-log_a)
    u_ref[...] = jnp.sqrt(2.0 * th / (1.0 + th)) * (ig * xc)

    h = h_ref[...]
    for t0 in range(0, rows, SUBLANES):
        h = a_ref[t0:t0 + SUBLANES, :] * h + u_ref[t0:t0 + SUBLANES, :]
        u_ref[t0:t0 + SUBLANES, :] = h
    h_ref[...] = h
    y = u_ref[...] * jax.nn.gelu(gate_ref[...], approximate=True)
    nseq, steps, _ = o_ref.shape
    for c in range(w // LANES):
        y_ref[c] = y[:, c * LANES:(c + 1) * LANES]
    for b in range(nseq):
        for c in range(w // LANES):
            o_ref[b, :, c * LANES:(c + 1) * LANES] = (
                y_ref[c, pl.ds(b, steps, stride=nseq), :].astype(o_ref.dtype))


def _lru(pa, cw, cb, wg, bg, lam, batch, seq, rows=512):
    t = pa.shape[0]
    w = HALF
    steps = rows // batch
    fixed = lambda s: (0, 0)
    return pl.pallas_call(
        _lru_kernel,
        grid=(t // rows,),
        in_specs=[
            pl.BlockSpec((rows, w), lambda s: (s, 0)),
            pl.BlockSpec((rows, w), lambda s: (s, 1)),
            pl.BlockSpec((CONV_WIDTH, w), fixed),
            pl.BlockSpec((1, w), fixed),
            pl.BlockSpec((w, 2 * w), fixed),
            pl.BlockSpec((1, 2 * w), fixed),
            pl.BlockSpec((1, w), fixed),
        ],
        out_specs=pl.BlockSpec((batch, steps, w), lambda s: (0, s, 0)),
        out_shape=jax.ShapeDtypeStruct((batch, seq, w), BF16),
        scratch_shapes=[
            pltpu.VMEM((rows + (CONV_WIDTH - 1) * SUBLANES, w), F32),
            pltpu.VMEM((rows, w), F32),
            pltpu.VMEM((rows, w), F32),
            pltpu.VMEM((SUBLANES, w), F32),
            pltpu.VMEM((w // LANES, rows, LANES), F32),
        ],
        compiler_params=_cparams(("arbitrary",)),
        name="rg_lru",
    )(pa, pa, cw, cb, wg, bg, lam).reshape(t, w)


def _block_diag(wb):
    n, bd, _ = wb.shape
    eye = jnp.eye(n, dtype=wb.dtype)
    return (eye[:, None, :, None] * wb[:, :, None, :]).reshape(n * bd, n * bd)


def kernel(x, ev_w_in, ev_fox_bf, hgrn_lb, ev_hgrn_norm_g, ev_w_out, od_w_in, od_conv_w, od_conv_b,
           od_gate_a_w, od_gate_a_b, od_gate_x_w, od_gate_x_b, od_lru_lambda, od_w_out, ffn_w13,
           ffn_w2, ln_g, ln_b):
    batch, seq, d = x.shape
    t = batch * seq
    xf = x.reshape(t, d)
    lb_all = jnp.cumsum(jax.nn.softmax(hgrn_lb.astype(F32), axis=0), axis=0)
    h = HALF

    nfox = ev_fox_bf.shape[1]
    cols_a = [(0, 0, 2 * h), (2 * h, 3 * h, h), (3 * h, 7 * h, nfox)]
    cols_b = [(0, 2 * h, h), (h, 4 * h, 3 * h)]
    pa, pb = _proj(xf, ev_w_in[0], cols_a, cols_b)
    ya = _hgrn(pa, pb, lb_all[0].reshape(1, h), ev_hgrn_norm_g[0].reshape(1, HGRN_DIM), batch, seq)
    bf_pad = jnp.broadcast_to(ev_fox_bf[0].astype(F32).reshape(nfox, 1), (nfox, LANES))
    cum = _fox_gate(pa, 3 * h // LANES, bf_pad, batch, seq)
    yb = _foxt(pb, h // LANES, cum.reshape(batch, nfox // 2, 2, seq), batch, seq)
    lng = ln_g.astype(F32).reshape(2 * DEPTH, 1, d)
    lnb = ln_b.astype(F32).reshape(2 * DEPTH, 1, d)
    w13_bf, w2_bf = ffn_w13.astype(BF16), ffn_w2.astype(BF16)
    x2 = _layer_tail(ya, yb, xf, ev_w_out.astype(BF16), lng, lnb, w13_bf, w2_bf, layer=0, slot=0)

    assert batch == SUBLANES
    pa, pb = _proj(x2, od_w_in[0], [(0, 0, 2 * h)], [(0, 2 * h, 3 * h)], a_time_major=(batch, seq))
    wg = jnp.concatenate([_block_diag(od_gate_a_w[0]), _block_diag(od_gate_x_w[0])], axis=1).astype(BF16)
    bg = jnp.concatenate([od_gate_a_b[0], od_gate_x_b[0]]).reshape(1, 2 * h).astype(F32)
    yc = _lru(pa, od_conv_w[0], od_conv_b[0].reshape(1, h), wg, bg, od_lru_lambda[0].reshape(1, h),
              batch, seq)
    yd = _sb(pb, 0, batch, seq)
    x4 = _layer_tail(yc, yd, x2, od_w_out.astype(BF16), lng, lnb, w13_bf, w2_bf, layer=1, slot=0)
    return x4.reshape(batch, seq, d)
```

```python
import functools

import jax
import jax.numpy as jnp
from jax import lax
from jax.experimental import pallas as pl
from jax.experimental.pallas import tpu as pltpu

F32 = jnp.float32
BF16 = jnp.bfloat16

SUBLANES = 8
LANES = 128

D_MODEL = 1024
HALF = D_MODEL // 2
HEAD_DIM = 64
HGRN_DIM = 128
LRU_BLOCKS = 8
LRU_C = 8.0
CONV_WIDTH = 4
DEPTH = 2
ALPHA = (2 * DEPTH) ** 0.25
EPS = 1e-5
D_FF = 2816
LOG2_E = 1.4426950408889634

FFN_CHUNK = 256
TAIL_ROW_PARTS = 4
HGRN_CHUNK = 128
HGRN_UNROLL = 4
ATT_BLOCK = 256
FOX_WAVE = 4
SB_WAVE = 2
VMEM_LIMIT = 56 * 1024 * 1024


def _cparams(sem):
    return pltpu.CompilerParams(dimension_semantics=sem, vmem_limit_bytes=VMEM_LIMIT)


def _proj_kernel(cols_a, cols_b, x_ref, w_ref, oa_ref, ob_ref, wa_ref, wb_ref, *scratch):
    @pl.when(pl.program_id(0) == 0)
    def _():
        for dst_ref, cols in ((wa_ref, cols_a), (wb_ref, cols_b)):
            for dst, src, n in cols:
                if n % LANES:
                    slot = -(-n // LANES) * LANES
                    dst_ref[:, dst:dst + slot] = jnp.zeros((dst_ref.shape[0], slot), BF16)
                dst_ref[:, dst:dst + n] = w_ref[:, src:src + n].astype(BF16)

    if len(x_ref.shape) == 2:
        xb = x_ref[...].astype(BF16)
        oa_ref[...] = jnp.dot(xb, wa_ref[...], preferred_element_type=F32)
        ob_ref[...] = jnp.dot(xb, wb_ref[...], preferred_element_type=F32).astype(BF16)
    else:
        nseq, steps, d = x_ref.shape
        xt_ref = scratch[0]
        for b in range(nseq):
            for c in range(d // LANES):
                xt_ref[c, pl.ds(b, steps, stride=nseq), :] = x_ref[b, :, c * LANES:(c + 1) * LANES]
        xt = jnp.concatenate([xt_ref[c] for c in range(d // LANES)], axis=1)
        oa_ref[...] = jnp.dot(xt.astype(BF16), wa_ref[...], preferred_element_type=F32)
        xb = x_ref[...].reshape(nseq * steps, d).astype(BF16)
        ob = jnp.dot(xb, wb_ref[...], preferred_element_type=F32).astype(BF16)
        ob_ref[...] = ob.reshape(ob_ref.shape)


def _proj(x, w, cols_a, cols_b, a_time_major=None, tm=512):
    t, d = x.shape

    def width(cols):
        return max(dst + -(-n // LANES) * LANES for dst, _, n in cols)

    na, nb = width(cols_a), width(cols_b)
    row = lambda i: (i, 0)
    scratch = [pltpu.VMEM((d, na), BF16), pltpu.VMEM((d, nb), BF16)]
    if a_time_major is None:
        x_spec, b_spec, b_shape = pl.BlockSpec((tm, d), row), pl.BlockSpec((tm, nb), row), (t, nb)
    else:
        batch, seq = a_time_major
        steps = tm // batch
        x = x.reshape(batch, seq, d)
        x_spec = pl.BlockSpec((batch, steps, d), lambda i: (0, i, 0))
        b_spec, b_shape = pl.BlockSpec((batch, steps, nb), lambda i: (0, i, 0)), (batch, seq, nb)
        scratch.append(pltpu.VMEM((d // LANES, tm, LANES), F32))
    oa, ob = pl.pallas_call(
        functools.partial(_proj_kernel, cols_a, cols_b),
        grid=(t // tm,),
        in_specs=[x_spec, pl.BlockSpec(w.shape, lambda i: (0, 0), pipeline_mode=pl.Buffered(1))],
        out_specs=[pl.BlockSpec((tm, na), row), b_spec],
        out_shape=[jax.ShapeDtypeStruct((t, na), F32), jax.ShapeDtypeStruct(b_shape, BF16)],
        scratch_shapes=scratch,
        compiler_params=_cparams(("arbitrary",)),
        name="in_proj",
    )(x, w)
    return oa, ob.reshape(t, nb)


def _lockstep(gens):
    results = [None] * len(gens)
    live = list(range(len(gens)))
    while live:
        for k in list(live):
            try:
                next(gens[k])
            except StopIteration as done:
                results[k] = done.value
                live.remove(k)
    return results


def _layer_norm_rows(y, g, b):
    mu = jnp.mean(y, axis=-1, keepdims=True)
    yc = y - mu
    var = jnp.mean(yc * yc, axis=-1, keepdims=True)
    return yc * lax.rsqrt(var + EPS) * g + b


def _layer_tail_kernel(ya_ref, yb_ref, x_ref, wo_ref, g1_ref, b1_ref, w13_ref, w2_ref, g2_ref, b2_ref,
                       o_ref, xb_ref, h_ref):
    tf = FFN_CHUNK
    nf = w2_ref.shape[0] // tf
    tm = x_ref.shape[0]
    n_parts = TAIL_ROW_PARTS

    def part(p):
        rows = slice(p * tm // n_parts, (p + 1) * tm // n_parts)
        mix = jnp.dot(ya_ref[rows, :], wo_ref[0:HALF, :], preferred_element_type=F32)
        mix = mix + jnp.dot(yb_ref[rows, :], wo_ref[HALF:D_MODEL, :], preferred_element_type=F32)
        yield
        x1 = _layer_norm_rows(ALPHA * x_ref[rows, :] + mix, g1_ref[...], b1_ref[...])
        o_ref[rows, :] = x1
        xb_ref[rows, :] = x1.astype(BF16)
        for c in range(nf):
            xb = xb_ref[rows, :]
            a = jnp.dot(xb, w13_ref[:, c * tf:(c + 1) * tf], preferred_element_type=F32)
            b = jnp.dot(xb, w13_ref[:, (nf + c) * tf:(nf + c + 1) * tf], preferred_element_type=F32)
            yield
            h_ref[rows, c * tf:(c + 1) * tf] = (a * jax.nn.sigmoid(a) * b).astype(BF16)
        for n0 in range(0, D_MODEL, tf):
            cols = slice(n0, n0 + tf)
            down = jnp.dot(h_ref[rows, :], w2_ref[:, cols], preferred_element_type=F32)
            yield
            o_ref[rows, cols] = ALPHA * o_ref[rows, cols] + down
        o_ref[rows, :] = _layer_norm_rows(o_ref[rows, :], g2_ref[...], b2_ref[...])

    _lockstep([part(p) for p in range(n_parts)])


def _layer_tail(ya, yb, x, w_out, ln_g, ln_b, w13, w2, layer, slot, tm=1024):
    t = x.shape[0]
    nf = D_FF // FFN_CHUNK
    assert nf % 2 == 1 and nf * FFN_CHUNK == D_FF and D_MODEL % FFN_CHUNK == 0
    d = D_MODEL
    resident = pl.Buffered(1)
    row = lambda i: (i, 0)
    return pl.pallas_call(
        _layer_tail_kernel,
        grid=(t // tm,),
        in_specs=[
            pl.BlockSpec((tm, HALF), row),
            pl.BlockSpec((tm, HALF), row),
            pl.BlockSpec((tm, d), row),
            pl.BlockSpec((None, d, d), lambda i: (slot, 0, 0), pipeline_mode=resident),
            pl.BlockSpec((None, 1, d), lambda i: (2 * layer, 0, 0)),
            pl.BlockSpec((None, 1, d), lambda i: (2 * layer, 0, 0)),
            pl.BlockSpec((None, d, 2 * D_FF), lambda i: (layer, 0, 0), pipeline_mode=resident),
            pl.BlockSpec((None, D_FF, d), lambda i: (layer, 0, 0), pipeline_mode=resident),
            pl.BlockSpec((None, 1, d), lambda i: (2 * layer + 1, 0, 0)),
            pl.BlockSpec((None, 1, d), lambda i: (2 * layer + 1, 0, 0)),
        ],
        out_specs=pl.BlockSpec((tm, d), row),
        out_shape=jax.ShapeDtypeStruct((t, d), F32),
        scratch_shapes=[pltpu.VMEM((tm, d), BF16), pltpu.VMEM((tm, D_FF), BF16)],
        compiler_params=_cparams(("parallel",)),
        name="layer_tail",
    )(ya, yb, x, w_out, ln_g, ln_b, w13, w2, ln_g, ln_b)


def _tile_rows(n):
    return lax.broadcasted_iota(jnp.int32, (SUBLANES, n), 0)


def _bcast_row(tile, r):
    return jnp.broadcast_to(tile[r:r + 1, :], tile.shape)


def _segmented_products(x):
    n = len(x)
    row = _tile_rows(x[0].shape[1])
    one = jnp.ones_like(x[0])
    r3 = row & 3
    up4 = (row & 4) != 0

    p = {1: list(x)}
    p[2] = [t * jnp.where((row & 1) == 1, pltpu.roll(t, 1, 0), 1.0) for t in x]
    p[4] = [t * jnp.where(r3 == 2, pltpu.roll(t, 1, 0),
                          jnp.where(r3 == 3, pltpu.roll(t, 2, 0), 1.0)) for t in p[2]]
    p[8] = [t * jnp.where(up4, _bcast_row(t, 3), 1.0) for t in p[4]]

    e = {1: [one] * n}
    e[2] = [jnp.where((row & 1) == 0, pltpu.roll(t, SUBLANES - 1, 0), 1.0) for t in x]
    r2 = [a * b for a, b in zip(e[2], x)]
    e[4] = [t * jnp.where(r3 == 1, pltpu.roll(r, SUBLANES - 1, 0),
                          jnp.where(r3 == 0, pltpu.roll(r, SUBLANES - 2, 0), 1.0))
            for t, r in zip(e[2], r2)]
    r4 = [a * b for a, b in zip(e[4], x)]
    e[8] = [t * jnp.where(up4, 1.0, _bcast_row(r, 4)) for t, r in zip(e[4], r4)]

    m = SUBLANES
    while m < SUBLANES * n:
        nt = m // SUBLANES
        pn, en = [], []
        for g in range(n // (2 * nt)):
            lo = slice(2 * nt * g, 2 * nt * g + nt)
            hi = slice(2 * nt * g + nt, 2 * nt * (g + 1))
            tot = _bcast_row(p[m][lo][-1], SUBLANES - 1)
            pn += p[m][lo] + [t * tot for t in p[m][hi]]
            first = e[m][hi][0][0:1, :] * x[hi][0][0:1, :]
            tot_hi = jnp.broadcast_to(first, one.shape)
            en += [t * tot_hi for t in e[m][lo]] + e[m][hi]
        p[2 * m], e[2 * m] = pn, en
        m *= 2
    return p, e


def _level_map(c):
    t = lax.broadcasted_iota(jnp.int32, (c, c), 0)
    s = lax.broadcasted_iota(jnp.int32, (c, c), 1)
    x = t ^ s
    lv = jnp.full((c, c), -1, jnp.int32)
    m = 1
    while m < c:
        lv = lv + (x >= m).astype(jnp.int32)
        m *= 2
    return jnp.where(t < s, -2, lv)


def _hgrn_kernel(q_ref, f_ref, g_ref, v_ref, lb_ref, ng_ref, o_ref):
    c = HGRN_CHUNK
    n_tiles = c // SUBLANES
    n_chunks = q_ref.shape[0] // c
    lb = lb_ref[...]
    one_m_lb = 1.0 - lb
    ng = ng_ref[...]
    lv = _level_map(c)

    def chunk(ci, state):
        rows = slice(ci * c, (ci + 1) * c)
        z = f_ref[rows, :]
        q = q_ref[rows, :]
        v = v_ref[rows, :]
        gg = g_ref[rows, :]
        ez = jnp.exp(-jnp.abs(z))
        rz = 1.0 / (1.0 + ez)
        pos = z >= 0.0
        sig = jnp.where(pos, rz, ez * rz)
        nsig = jnp.where(pos, ez * rz, rz)
        fg = lb + one_m_lb * sig
        kk = one_m_lb * nsig

        fg_tiles = [fg[SUBLANES * i:SUBLANES * (i + 1), :] for i in range(n_tiles)]
        p, e = _segmented_products(fg_tiles)

        def cat(ts):
            return jnp.concatenate(ts, axis=0)

        nt_dims = (((1,), (1,)), ((), ()))
        qb, kb = q.astype(BF16), kk.astype(BF16)
        yield
        scores = jnp.where(
            lv == -1, lax.dot_general(qb, kb, nt_dims, preferred_element_type=F32), 0.0)
        m, idx = 1, 0
        while m < c:
            qm = (q * cat(p[m])).astype(BF16)
            km = kb if m == 1 else (kk * cat(e[m])).astype(BF16)
            yield
            sm = lax.dot_general(qm, km, nt_dims, preferred_element_type=F32)
            scores = jnp.where(lv == idx, sm, scores)
            m *= 2
            idx += 1

        b = cat(p[c])
        qc = (q * b).astype(BF16)
        kc = (kk * cat(e[c])).astype(BF16)
        decay = b[c - 1:c, :]
        sb = scores.astype(BF16)
        yield
        kv = lax.dot_general(kc, v, (((0,), (0,)), ((), ())), preferred_element_type=F32)
        o = jnp.dot(sb, v, preferred_element_type=F32)
        yield
        st = state[0]
        o = o + jnp.dot(qc, st.astype(BF16), preferred_element_type=F32)
        state[0] = st * jnp.broadcast_to(decay, st.shape).T + kv
        yield
        ms = jnp.mean(o * o, axis=-1, keepdims=True)
        on = o * lax.rsqrt(ms + EPS) * ng
        o_ref[rows, :] = (on * (gg * jax.nn.sigmoid(gg))).astype(o_ref.dtype)

    state = [jnp.zeros((HGRN_DIM, HGRN_DIM), F32)]
    for c0 in range(0, n_chunks, HGRN_UNROLL):
        _lockstep([chunk(c0 + k, state) for k in range(HGRN_UNROLL)])


def _hgrn(pa, pb, lb, ng, batch, seq):
    t = pa.shape[0]
    nh = HALF // HGRN_DIM
    return pl.pallas_call(
        _hgrn_kernel,
        grid=(batch, nh),
        in_specs=[
            pl.BlockSpec((seq, HGRN_DIM), lambda b, h: (b, h)),
            pl.BlockSpec((seq, HGRN_DIM), lambda b, h: (b, nh + h)),
            pl.BlockSpec((seq, HGRN_DIM), lambda b, h: (b, 2 * nh + h)),
            pl.BlockSpec((seq, HGRN_DIM), lambda b, h: (b, h)),
            pl.BlockSpec((1, HGRN_DIM), lambda b, h: (0, h)),
            pl.BlockSpec((1, HGRN_DIM), lambda b, h: (0, 0)),
        ],
        out_specs=pl.BlockSpec((seq, HGRN_DIM), lambda b, h: (b, h)),
        out_shape=jax.ShapeDtypeStruct((t, HALF), BF16),
        compiler_params=_cparams(("parallel", "parallel")),
        name="hgrn2",
    )(pa, pa, pa, pb, lb, ng)


def _fox_gate_kernel(f_ref, bf_ref, c_ref):
    s = f_ref.shape[0]
    ft = f_ref[...].T[0:SUBLANES, :]
    x = ft + bf_ref[...][:, 0:1]
    lf = jnp.minimum(x, 0.0) - jnp.log1p(jnp.exp(-jnp.abs(x)))
    lane = lax.broadcasted_iota(jnp.int32, lf.shape, 1)
    d = 1
    while d < s:
        lf = lf + jnp.where(lane >= d, pltpu.roll(lf, d, 1), 0.0)
        d *= 2
    c_ref[0] = lf


def _fox_gate(pa, col_block, bf_pad, batch, seq):
    return pl.pallas_call(
        _fox_gate_kernel,
        grid=(batch,),
        in_specs=[
            pl.BlockSpec((seq, LANES), lambda b: (b, col_block)),
            pl.BlockSpec((SUBLANES, LANES), lambda b: (0, 0)),
        ],
        out_specs=pl.BlockSpec((1, SUBLANES, seq), lambda b: (b, 0, 0)),
        out_shape=jax.ShapeDtypeStruct((batch, SUBLANES, seq), F32),
        compiler_params=_cparams(("parallel",)),
        name="fox_gate",
    )(pa, bf_pad)


def _head_lane_masks(shape):
    lane = lax.broadcasted_iota(jnp.int32, shape, len(shape) - 1)
    first = lane < HEAD_DIM
    return first, jnp.logical_not(first)


def _causal_tile_order(nq, u):
    assert nq % u == 0
    pairs = [(i, i) for i in range(nq)]
    left = {i: i for i in range(nq)}
    while any(left.values()):
        pick = sorted((i for i in left if left[i]), key=lambda i: -left[i])[:u]
        assert len(pick) == u, "tile list does not split into groups of distinct query blocks"
        for i in pick:
            pairs.append((i, left[i] - 1))
            left[i] -= 1
    return [p[0] for p in pairs], [p[1] for p in pairs]


def _split_heads(q_ref, v_ref, qh_ref, vh_ref):
    scale = HEAD_DIM ** -0.5
    t = ATT_BLOCK
    h0, h1 = _head_lane_masks((t, LANES))
    zero = jnp.zeros((t, LANES), BF16)
    for i in range(q_ref.shape[0] // t):
        q = q_ref[i * t:(i + 1) * t, :] * jnp.asarray(scale, BF16)
        v = v_ref[i * t:(i + 1) * t, :]
        qh_ref[i, 0:t, :] = jnp.where(h0, q, zero)
        qh_ref[i, t:2 * t, :] = jnp.where(h1, q, zero)
        vh_ref[i, 0:t, :] = jnp.where(h0, v, zero)
        vh_ref[i, t:2 * t, :] = jnp.where(h1, v, zero)


def _tile_waves(nq, tile_fn, u):
    qi, kj = _causal_tile_order(nq, u)
    state = [None] * nq
    for w0 in range(0, len(qi), u):
        wave = list(zip(qi[w0:w0 + u], kj[w0:w0 + u]))
        results = _lockstep([tile_fn(i, j, state[i], i == j) for i, j in wave])
        for (i, _), r in zip(wave, results):
            state[i] = r
    return state


def _foxt_kernel(q_ref, k_ref, v_ref, c_ref, o_ref, qt_ref, vt_ref, cb_ref):
    tq = tk = ATT_BLOCK
    seq = q_ref.shape[0]
    nq = seq // tq
    scale = HEAD_DIM ** -0.5
    lane0, lane1 = _head_lane_masks((tq, LANES))
    dim_head0 = lax.broadcasted_iota(jnp.int32, (LANES, tq), 0) < HEAD_DIM

    for i in range(nq):
        q = (q_ref[i * tq:(i + 1) * tq, :] * jnp.asarray(scale, BF16)).astype(F32)
        v = v_ref[i * tq:(i + 1) * tq, :].astype(F32)
        for h, lanes in enumerate((lane0, lane1)):
            qt_ref[i, :, h * tq:(h + 1) * tq] = jnp.where(lanes, q, 0.0).T.astype(BF16)
            vt_ref[i, :, h * tq:(h + 1) * tq] = jnp.where(lanes, v, 0.0).T.astype(BF16)
    for h in range(2):
        cb_ref[h] = jnp.broadcast_to(c_ref[0, 0, h:h + 1, :], (LANES, seq)).T

    def tile(i, j, state, diag):
        keys = slice(j * tk, (j + 1) * tk)
        s = jnp.dot(k_ref[keys, :], qt_ref[i], preferred_element_type=F32)
        yield
        wide = tq // LANES
        s = jnp.concatenate([s[:, 0:tq] - jnp.concatenate([cb_ref[0, keys, :]] * wide, axis=1),
                             s[:, tq:2 * tq] - jnp.concatenate([cb_ref[1, keys, :]] * wide, axis=1)],
                            axis=1)
        if diag:
            kpos = lax.broadcasted_iota(jnp.int32, s.shape, 0)
            qpos = lax.broadcasted_iota(jnp.int32, s.shape, 1) & (tq - 1)
            s = jnp.where(kpos <= qpos, s, -jnp.inf)
        col_max = jnp.max(s, axis=0, keepdims=True)
        if diag:
            m_new = col_max
        else:
            m_old, l_old, acc_old = state
            m_new = jnp.maximum(m_old, col_max)
            alpha = jnp.exp(m_old - m_new)
        p = jnp.exp(s - m_new)
        psum = jnp.sum(p, axis=0, keepdims=True)
        pb = p.astype(BF16)
        yield
        pv = jnp.dot(vt_ref[j], jnp.concatenate([pb[:, 0:tq], pb[:, tq:2 * tq]], axis=0),
                     preferred_element_type=F32)
        yield
        if diag:
            return m_new, psum, pv
        a_rows = jnp.where(dim_head0, alpha[:, 0:tq], alpha[:, tq:2 * tq])
        return m_new, alpha * l_old + psum, acc_old * a_rows + pv

    for i, (_, l, acc) in enumerate(_tile_waves(nq, tile, FOX_WAVE)):
        l_rows = jnp.where(dim_head0, l[:, 0:tq], l[:, tq:2 * tq])
        o_ref[i * tq:(i + 1) * tq, :] = (acc / l_rows).T.astype(o_ref.dtype)


def _foxt(pb, col0, cpair, batch, seq):
    t = pb.shape[0]
    npair = HALF // LANES
    nq = seq // ATT_BLOCK
    return pl.pallas_call(
        _foxt_kernel,
        grid=(batch, npair),
        in_specs=[
            pl.BlockSpec((seq, LANES), lambda b, p: (b, col0 + p)),
            pl.BlockSpec((seq, LANES), lambda b, p: (b, col0 + npair + p)),
            pl.BlockSpec((seq, LANES), lambda b, p: (b, col0 + 2 * npair + p)),
            pl.BlockSpec((1, 1, 2, seq), lambda b, p: (b, p, 0, 0)),
        ],
        out_specs=pl.BlockSpec((seq, LANES), lambda b, p: (b, p)),
        out_shape=jax.ShapeDtypeStruct((t, HALF), BF16),
        scratch_shapes=[
            pltpu.VMEM((nq, LANES, 2 * ATT_BLOCK), BF16),
            pltpu.VMEM((nq, LANES, 2 * ATT_BLOCK), BF16),
            pltpu.VMEM((2, seq, LANES), F32),
        ],
        compiler_params=_cparams(("parallel", "parallel")),
        name="fox_attention",
    )(pb, pb, pb, cpair)


def _suffix_matrix(tk):
    r = lax.broadcasted_iota(jnp.int32, (2 * tk, 2 * tk), 0) % tk
    cidx = lax.broadcasted_iota(jnp.int32, (2 * tk, 2 * tk), 1)
    keep = jnp.logical_or(cidx >= tk, r > cidx)
    return jnp.where(keep, 1.0, 0.0).astype(BF16)


def _sb_kernel(q_ref, k_ref, v_ref, o_ref, qh_ref, vh_ref):
    tq = tk = ATT_BLOCK
    nq = q_ref.shape[0] // tq
    nsub = tk // LANES
    nt_dims = (((1,), (1,)), ((), ()))
    umat = _suffix_matrix(LANES)

    _split_heads(q_ref, v_ref, qh_ref, vh_ref)

    def tile(i, j, state, diag):
        z = lax.dot_general(qh_ref[i], k_ref[j * tk:(j + 1) * tk, :], nt_dims,
                            preferred_element_type=F32)
        yield
        logb = jnp.minimum(z, 0.0) - jnp.log(1.0 + jnp.exp2(jnp.abs(z) * -LOG2_E))
        lom = logb - z
        if diag:
            qpos = lax.broadcasted_iota(jnp.int32, z.shape, 0) & (tq - 1)
            kpos = lax.broadcasted_iota(jnp.int32, z.shape, 1)
            mask = kpos < qpos
            lom = jnp.where(mask, lom, 0.0)
        parts = []
        for c in reversed(range(nsub)):
            part = lom[:, c * LANES:(c + 1) * LANES]
            hi = part.astype(BF16)
            lo = (part - hi.astype(F32)).astype(BF16)
            parts.append(jnp.concatenate([hi, lo], axis=1))
        yield
        r = jnp.dot(jnp.concatenate(parts, axis=0), umat, preferred_element_type=F32)
        yield
        carry = jnp.zeros((2 * tq, LANES), F32) if diag else state[0]
        suffix = [None] * nsub
        for n_c, c in enumerate(reversed(range(nsub))):
            rc = r[n_c * 2 * tq:(n_c + 1) * 2 * tq]
            suffix[c] = rc[:, 0:LANES] + carry
            carry = carry + rc[:, LANES:2 * LANES]
        w = jnp.exp(logb + jnp.concatenate(suffix, axis=1))
        if diag:
            w = jnp.where(mask, w, 0.0)
        wb = w.astype(BF16)
        yield
        wv = jnp.dot(jnp.concatenate([wb[0:tq], wb[tq:2 * tq]], axis=1), vh_ref[j],
                     preferred_element_type=F32)
        yield
        return carry, (wv if diag else state[1] + wv)

    for i, (_, acc) in enumerate(_tile_waves(nq, tile, SB_WAVE)):
        o_ref[i * tq:(i + 1) * tq, :] = acc.astype(o_ref.dtype)


def _sb(pb, col0, batch, seq):
    t = pb.shape[0]
    npair = HALF // LANES
    nq = seq // ATT_BLOCK
    return pl.pallas_call(
        _sb_kernel,
        grid=(batch, npair),
        in_specs=[
            pl.BlockSpec((seq, LANES), lambda b, p: (b, col0 + p)),
            pl.BlockSpec((seq, LANES), lambda b, p: (b, col0 + npair + p)),
            pl.BlockSpec((seq, LANES), lambda b, p: (b, col0 + 2 * npair + p)),
        ],
        out_specs=pl.BlockSpec((seq, LANES), lambda b, p: (b, p)),
        out_shape=jax.ShapeDtypeStruct((t, HALF), BF16),
        scratch_shapes=[
            pltpu.VMEM((nq, 2 * ATT_BLOCK, LANES), BF16),
            pltpu.VMEM((nq, 2 * ATT_BLOCK, LANES), BF16),
        ],
        compiler_params=_cparams(("parallel", "parallel")),
        name="sb_attention",
    )(pb, pb, pb)


def _lru_kernel(x_ref, gate_ref, cw_ref, cb_ref, wg_ref, bg_ref, lam_ref, o_ref,
                xe_ref, a_ref, u_ref, h_ref, y_ref):
    step = pl.program_id(0)
    rows, w = x_ref.shape
    hist = (CONV_WIDTH - 1) * SUBLANES

    @pl.when(step == 0)
    def _():
        xe_ref[0:hist, :] = jnp.zeros((hist, w), F32)
        h_ref[...] = jnp.zeros_like(h_ref)

    @pl.when(step > 0)
    def _():
        xe_ref[0:hist, :] = xe_ref[rows:rows + hist, :]

    x = x_ref[...]
    xe_ref[hist:hist + rows, :] = x
    xc = cb_ref[...] + cw_ref[CONV_WIDTH - 1:CONV_WIDTH, :] * x
    for d in range(1, CONV_WIDTH):
        back = hist - d * SUBLANES
        xc = xc + cw_ref[CONV_WIDTH - 1 - d:CONV_WIDTH - d, :] * xe_ref[back:back + rows, :]

    gates = jnp.dot(xc.astype(BF16), wg_ref[...], preferred_element_type=F32) + bg_ref[...]
    r = jax.nn.sigmoid(gates[:, 0:w])
    ig = jax.nn.sigmoid(gates[:, w:2 * w])
    lam = lam_ref[...]
    softplus_neg_lam = jnp.maximum(-lam, 0.0) + jnp.log1p(jnp.exp(-jnp.abs(lam)))
    log_a = -LRU_C * r * softplus_neg_lam
    a_ref[...] = jnp.exp(log_a)
    th = jnp.tanh(-log_a)
    u_ref[...] = jnp.sqrt(2.0 * th / (1.0 + th)) * (ig * xc)

    h = h_ref[...]
    for t0 in range(0, rows, SUBLANES):
        h = a_ref[t0:t0 + SUBLANES, :] * h + u_ref[t0:t0 + SUBLANES, :]
        u_ref[t0:t0 + SUBLANES, :] = h
    h_ref[...] = h
    y = u_ref[...] * jax.nn.gelu(gate_ref[...], approximate=True)
    nseq, steps, _ = o_ref.shape
    for c in range(w // LANES):
        y_ref[c] = y[:, c * LANES:(c + 1) * LANES]
    for b in range(nseq):
        for c in range(w // LANES):
            o_ref[b, :, c * LANES:(c + 1) * LANES] = (
                y_ref[c, pl.ds(b, steps, stride=nseq), :].astype(o_ref.dtype))


def _lru(pa, cw, cb, wg, bg, lam, batch, seq, rows=512):
    t = pa.shape[0]
    w = HALF
    steps = rows // batch
    fixed = lambda s: (0, 0)
    return pl.pallas_call(
        _lru_kernel,
        grid=(t // rows,),
        in_specs=[
            pl.BlockSpec((rows, w), lambda s: (s, 0)),
            pl.BlockSpec((rows, w), lambda s: (s, 1)),
            pl.BlockSpec((CONV_WIDTH, w), fixed),
            pl.BlockSpec((1, w), fixed),
            pl.BlockSpec((w, 2 * w), fixed),
            pl.BlockSpec((1, 2 * w), fixed),
            pl.BlockSpec((1, w), fixed),
        ],
        out_specs=pl.BlockSpec((batch, steps, w), lambda s: (0, s, 0)),
        out_shape=jax.ShapeDtypeStruct((batch, seq, w), BF16),
        scratch_shapes=[
            pltpu.VMEM((rows + (CONV_WIDTH - 1) * SUBLANES, w), F32),
            pltpu.VMEM((rows, w), F32),
            pltpu.VMEM((rows, w), F32),
            pltpu.VMEM((SUBLANES, w), F32),
            pltpu.VMEM((w // LANES, rows, LANES), F32),
        ],
        compiler_params=_cparams(("arbitrary",)),
        name="rg_lru",
    )(pa, pa, cw, cb, wg, bg, lam).reshape(t, w)


def _block_diag(wb):
    n, bd, _ = wb.shape
    eye = jnp.eye(n, dtype=wb.dtype)
    return (eye[:, None, :, None] * wb[:, :, None, :]).reshape(n * bd, n * bd)


def kernel(x, ev_w_in, ev_fox_bf, hgrn_lb, ev_hgrn_norm_g, ev_w_out, od_w_in, od_conv_w, od_conv_b,
           od_gate_a_w, od_gate_a_b, od_gate_x_w, od_gate_x_b, od_lru_lambda, od_w_out, ffn_w13,
           ffn_w2, ln_g, ln_b):
    batch, seq, d = x.shape
    t = batch * seq
    xf = x.reshape(t, d)
    lb_all = jnp.cumsum(jax.nn.softmax(hgrn_lb.astype(F32), axis=0), axis=0)
    h = HALF

    nfox = ev_fox_bf.shape[1]
    cols_a = [(0, 0, 2 * h), (2 * h, 3 * h, h), (3 * h, 7 * h, nfox)]
    cols_b = [(0, 2 * h, h), (h, 4 * h, 3 * h)]
    pa, pb = _proj(xf, ev_w_in[0], cols_a, cols_b)
    ya = _hgrn(pa, pb, lb_all[0].reshape(1, h), ev_hgrn_norm_g[0].reshape(1, HGRN_DIM), batch, seq)
    bf_pad = jnp.broadcast_to(ev_fox_bf[0].astype(F32).reshape(nfox, 1), (nfox, LANES))
    cum = _fox_gate(pa, 3 * h // LANES, bf_pad, batch, seq)
    yb = _foxt(pb, h // LANES, cum.reshape(batch, nfox // 2, 2, seq), batch, seq)
    lng = ln_g.astype(F32).reshape(2 * DEPTH, 1, d)
    lnb = ln_b.astype(F32).reshape(2 * DEPTH, 1, d)
    w13_bf, w2_bf = ffn_w13.astype(BF16), ffn_w2.astype(BF16)
    x2 = _layer_tail(ya, yb, xf, ev_w_out.astype(BF16), lng, lnb, w13_bf, w2_bf, layer=0, slot=0)

    assert batch == SUBLANES
    pa, pb = _proj(x2, od_w_in[0], [(0, 0, 2 * h)], [(0, 2 * h, 3 * h)], a_time_major=(batch, seq))
    wg = jnp.concatenate([_block_diag(od_gate_a_w[0]), _block_diag(od_gate_x_w[0])], axis=1).astype(BF16)
    bg = jnp.concatenate([od_gate_a_b[0], od_gate_x_b[0]]).reshape(1, 2 * h).astype(F32)
    yc = _lru(pa, od_conv_w[0], od_conv_b[0].reshape(1, h), wg, bg, od_lru_lambda[0].reshape(1, h),
              batch, seq)
    yd = _sb(pb, 0, batch, seq)
    x4 = _layer_tail(yc, yd, x2, od_w_out.astype(BF16), lng, lnb, w13_bf, w2_bf, layer=1, slot=0)
    return x4.reshape(batch, seq, d)
```

```python
import functools

import jax
import jax.numpy as jnp
from jax import lax
from jax.experimental import pallas as pl
from jax.experimental.pallas import tpu as pltpu

F32 = jnp.float32
BF16 = jnp.bfloat16

SUBLANES = 8
LANES = 128

D_MODEL = 1024
HALF = D_MODEL // 2
HEAD_DIM = 64
HGRN_DIM = 128
LRU_BLOCKS = 8
LRU_C = 8.0
CONV_WIDTH = 4
DEPTH = 2
ALPHA = (2 * DEPTH) ** 0.25
EPS = 1e-5
D_FF = 2816
LOG2_E = 1.4426950408889634

FFN_CHUNK = 256
TAIL_ROW_PARTS = 4
HGRN_CHUNK = 128
HGRN_UNROLL = 16
ATT_BLOCK = 256
FOX_WAVE = 4
SB_WAVE = 2
VMEM_LIMIT = 56 * 1024 * 1024


def _cparams(sem):
    return pltpu.CompilerParams(dimension_semantics=sem, vmem_limit_bytes=VMEM_LIMIT)


def _proj_kernel(cols_a, cols_b, x_ref, w_ref, oa_ref, ob_ref, wa_ref, wb_ref, *scratch):
    @pl.when(pl.program_id(0) == 0)
    def _():
        for dst_ref, cols in ((wa_ref, cols_a), (wb_ref, cols_b)):
            for dst, src, n in cols:
                if n % LANES:
                    slot = -(-n // LANES) * LANES
                    dst_ref[:, dst:dst + slot] = jnp.zeros((dst_ref.shape[0], slot), BF16)
                dst_ref[:, dst:dst + n] = w_ref[:, src:src + n].astype(BF16)

    if len(x_ref.shape) == 2:
        xb = x_ref[...].astype(BF16)
        oa_ref[...] = jnp.dot(xb, wa_ref[...], preferred_element_type=F32)
        ob_ref[...] = jnp.dot(xb, wb_ref[...], preferred_element_type=F32).astype(BF16)
    else:
        nseq, steps, d = x_ref.shape
        xt_ref = scratch[0]
        xb = x_ref[...].reshape(nseq * steps, d).astype(BF16)
        ob = jnp.dot(xb, wb_ref[...], preferred_element_type=F32).astype(BF16)
        ob_ref[...] = ob.reshape(ob_ref.shape)
        for b in range(nseq):
            for c in range(d // LANES):
                xt_ref[c, pl.ds(b, steps, stride=nseq), :] = x_ref[b, :, c * LANES:(c + 1) * LANES]
        xt = jnp.concatenate([xt_ref[c] for c in range(d // LANES)], axis=1)
        oa_ref[...] = jnp.dot(xt.astype(BF16), wa_ref[...], preferred_element_type=F32)


def _proj(x, w, cols_a, cols_b, a_time_major=None, tm=512):
    t, d = x.shape

    def width(cols):
        return max(dst + -(-n // LANES) * LANES for dst, _, n in cols)

    na, nb = width(cols_a), width(cols_b)
    row = lambda i: (i, 0)
    scratch = [pltpu.VMEM((d, na), BF16), pltpu.VMEM((d, nb), BF16)]
    if a_time_major is None:
        x_spec, b_spec, b_shape = pl.BlockSpec((tm, d), row), pl.BlockSpec((tm, nb), row), (t, nb)
    else:
        batch, seq = a_time_major
        steps = tm // batch
        x = x.reshape(batch, seq, d)
        x_spec = pl.BlockSpec((batch, steps, d), lambda i: (0, i, 0))
        b_spec, b_shape = pl.BlockSpec((batch, steps, nb), lambda i: (0, i, 0)), (batch, seq, nb)
        scratch.append(pltpu.VMEM((d // LANES, tm, LANES), F32))
    oa, ob = pl.pallas_call(
        functools.partial(_proj_kernel, cols_a, cols_b),
        grid=(t // tm,),
        in_specs=[x_spec, pl.BlockSpec(w.shape, lambda i: (0, 0), pipeline_mode=pl.Buffered(1))],
        out_specs=[pl.BlockSpec((tm, na), row), b_spec],
        out_shape=[jax.ShapeDtypeStruct((t, na), F32), jax.ShapeDtypeStruct(b_shape, BF16)],
        scratch_shapes=scratch,
        compiler_params=_cparams(("arbitrary",)),
        name="in_proj",
    )(x, w)
    return oa, ob.reshape(t, nb)


def _lockstep(gens):
    results = [None] * len(gens)
    live = list(range(len(gens)))
    while live:
        for k in list(live):
            try:
                next(gens[k])
            except StopIteration as done:
                results[k] = done.value
                live.remove(k)
    return results


def _layer_norm_rows(y, g, b):
    mu = jnp.mean(y, axis=-1, keepdims=True)
    yc = y - mu
    var = jnp.mean(yc * yc, axis=-1, keepdims=True)
    return yc * lax.rsqrt(var + EPS) * g + b


def _layer_tail_kernel(ya_ref, yb_ref, x_ref, wo_ref, g1_ref, b1_ref, w13_ref, w2_ref, g2_ref, b2_ref,
                       o_ref, xb_ref, h_ref):
    tf = FFN_CHUNK
    nf = w2_ref.shape[0] // tf
    tm = x_ref.shape[0]
    n_parts = TAIL_ROW_PARTS

    def part(p):
        rows = slice(p * tm // n_parts, (p + 1) * tm // n_parts)
        mix = jnp.dot(ya_ref[rows, :], wo_ref[0:HALF, :], preferred_element_type=F32)
        mix = mix + jnp.dot(yb_ref[rows, :], wo_ref[HALF:D_MODEL, :], preferred_element_type=F32)
        yield
        x1 = _layer_norm_rows(ALPHA * x_ref[rows, :] + mix, g1_ref[...], b1_ref[...])
        o_ref[rows, :] = x1
        xb_ref[rows, :] = x1.astype(BF16)
        for c in range(nf):
            xb = xb_ref[rows, :]
            a = jnp.dot(xb, w13_ref[:, c * tf:(c + 1) * tf], preferred_element_type=F32)
            b = jnp.dot(xb, w13_ref[:, (nf + c) * tf:(nf + c + 1) * tf], preferred_element_type=F32)
            yield
            h_ref[rows, c * tf:(c + 1) * tf] = (a * jax.nn.sigmoid(a) * b).astype(BF16)
        for n0 in range(0, D_MODEL, tf):
            cols = slice(n0, n0 + tf)
            down = jnp.dot(h_ref[rows, :], w2_ref[:, cols], preferred_element_type=F32)
            yield
            o_ref[rows, cols] = ALPHA * o_ref[rows, cols] + down
        o_ref[rows, :] = _layer_norm_rows(o_ref[rows, :], g2_ref[...], b2_ref[...])

    _lockstep([part(p) for p in range(n_parts)])


def _layer_tail(ya, yb, x, w_out, ln_g, ln_b, w13, w2, layer, slot, tm=1024):
    t = x.shape[0]
    nf = D_FF // FFN_CHUNK
    assert nf % 2 == 1 and nf * FFN_CHUNK == D_FF and D_MODEL % FFN_CHUNK == 0
    d = D_MODEL
    resident = pl.Buffered(1)
    row = lambda i: (i, 0)
    return pl.pallas_call(
        _layer_tail_kernel,
        grid=(t // tm,),
        in_specs=[
            pl.BlockSpec((tm, HALF), row),
            pl.BlockSpec((tm, HALF), row),
            pl.BlockSpec((tm, d), row),
            pl.BlockSpec((None, d, d), lambda i: (slot, 0, 0), pipeline_mode=resident),
            pl.BlockSpec((None, 1, d), lambda i: (2 * layer, 0, 0)),
            pl.BlockSpec((None, 1, d), lambda i: (2 * layer, 0, 0)),
            pl.BlockSpec((None, d, 2 * D_FF), lambda i: (layer, 0, 0), pipeline_mode=resident),
            pl.BlockSpec((None, D_FF, d), lambda i: (layer, 0, 0), pipeline_mode=resident),
            pl.BlockSpec((None, 1, d), lambda i: (2 * layer + 1, 0, 0)),
            pl.BlockSpec((None, 1, d), lambda i: (2 * layer + 1, 0, 0)),
        ],
        out_specs=pl.BlockSpec((tm, d), row),
        out_shape=jax.ShapeDtypeStruct((t, d), F32),
        scratch_shapes=[pltpu.VMEM((tm, d), BF16), pltpu.VMEM((tm, D_FF), BF16)],
        compiler_params=_cparams(("parallel",)),
        name="layer_tail",
    )(ya, yb, x, w_out, ln_g, ln_b, w13, w2, ln_g, ln_b)


def _tile_rows(n):
    return lax.broadcasted_iota(jnp.int32, (SUBLANES, n), 0)


def _bcast_row(tile, r):
    return jnp.broadcast_to(tile[r:r + 1, :], tile.shape)


def _segmented_products(x):
    n = len(x)
    row = _tile_rows(x[0].shape[1])
    one = jnp.ones_like(x[0])
    r3 = row & 3
    up4 = (row & 4) != 0

    p = {1: list(x)}
    p[2] = [t * jnp.where((row & 1) == 1, pltpu.roll(t, 1, 0), 1.0) for t in x]
    p[4] = [t * jnp.where(r3 == 2, pltpu.roll(t, 1, 0),
                          jnp.where(r3 == 3, pltpu.roll(t, 2, 0), 1.0)) for t in p[2]]
    p[8] = [t * jnp.where(up4, _bcast_row(t, 3), 1.0) for t in p[4]]

    e = {1: [one] * n}
    e[2] = [jnp.where((row & 1) == 0, pltpu.roll(t, SUBLANES - 1, 0), 1.0) for t in x]
    r2 = [a * b for a, b in zip(e[2], x)]
    e[4] = [t * jnp.where(r3 == 1, pltpu.roll(r, SUBLANES - 1, 0),
                          jnp.where(r3 == 0, pltpu.roll(r, SUBLANES - 2, 0), 1.0))
            for t, r in zip(e[2], r2)]
    r4 = [a * b for a, b in zip(e[4], x)]
    e[8] = [t * jnp.where(up4, 1.0, _bcast_row(r, 4)) for t, r in zip(e[4], r4)]

    m = SUBLANES
    while m < SUBLANES * n:
        nt = m // SUBLANES
        pn, en = [], []
        for g in range(n // (2 * nt)):
            lo = slice(2 * nt * g, 2 * nt * g + nt)
            hi = slice(2 * nt * g + nt, 2 * nt * (g + 1))
            tot = _bcast_row(p[m][lo][-1], SUBLANES - 1)
            pn += p[m][lo] + [t * tot for t in p[m][hi]]
            first = e[m][hi][0][0:1, :] * x[hi][0][0:1, :]
            tot_hi = jnp.broadcast_to(first, one.shape)
            en += [t * tot_hi for t in e[m][lo]] + e[m][hi]
        p[2 * m], e[2 * m] = pn, en
        m *= 2
    return p, e


def _level_map(c):
    t = lax.broadcasted_iota(jnp.int32, (c, c), 0)
    s = lax.broadcasted_iota(jnp.int32, (c, c), 1)
    x = t ^ s
    lv = jnp.full((c, c), -1, jnp.int32)
    m = 1
    while m < c:
        lv = lv + (x >= m).astype(jnp.int32)
        m *= 2
    return jnp.where(t < s, -2, lv)


def _hgrn_kernel(q_ref, f_ref, g_ref, v_ref, lb_ref, ng_ref, o_ref):
    c = HGRN_CHUNK
    n_tiles = c // SUBLANES
    n_chunks = q_ref.shape[0] // c
    lb = lb_ref[...]
    one_m_lb = 1.0 - lb
    ng = ng_ref[...]
    lv = _level_map(c)

    def chunk(ci, state):
        rows = slice(ci * c, (ci + 1) * c)
        z = f_ref[rows, :]
        q = q_ref[rows, :]
        v = v_ref[rows, :]
        gg = g_ref[rows, :]
        ez = jnp.exp(-jnp.abs(z))
        rz = 1.0 / (1.0 + ez)
        pos = z >= 0.0
        sig = jnp.where(pos, rz, ez * rz)
        nsig = jnp.where(pos, ez * rz, rz)
        fg = lb + one_m_lb * sig
        kk = one_m_lb * nsig

        fg_tiles = [fg[SUBLANES * i:SUBLANES * (i + 1), :] for i in range(n_tiles)]
        p, e = _segmented_products(fg_tiles)

        def cat(ts):
            return jnp.concatenate(ts, axis=0)

        nt_dims = (((1,), (1,)), ((), ()))
        qb, kb = q.astype(BF16), kk.astype(BF16)
        yield
        scores = jnp.where(
            lv == -1, lax.dot_general(qb, kb, nt_dims, preferred_element_type=F32), 0.0)
        m, idx = 1, 0
        while m < c:
            qm = (q * cat(p[m])).astype(BF16)
            km = kb if m == 1 else (kk * cat(e[m])).astype(BF16)
            yield
            sm = lax.dot_general(qm, km, nt_dims, preferred_element_type=F32)
            scores = jnp.where(lv == idx, sm, scores)
            m *= 2
            idx += 1

        b = cat(p[c])
        qc = (q * b).astype(BF16)
        kc = (kk * cat(e[c])).astype(BF16)
        decay = b[c - 1:c, :]
        sb = scores.astype(BF16)
        yield
        kv = lax.dot_general(kc, v, (((0,), (0,)), ((), ())), preferred_element_type=F32)
        o = jnp.dot(sb, v, preferred_element_type=F32)
        yield
        st = state[0]
        o = o + jnp.dot(qc, st.astype(BF16), preferred_element_type=F32)
        state[0] = st * jnp.broadcast_to(decay, st.shape).T + kv
        yield
        ms = jnp.mean(o * o, axis=-1, keepdims=True)
        on = o * lax.rsqrt(ms + EPS) * ng
        o_ref[rows, :] = (on * (gg * jax.nn.sigmoid(gg))).astype(o_ref.dtype)

    state = [jnp.zeros((HGRN_DIM, HGRN_DIM), F32)]
    for c0 in range(0, n_chunks, HGRN_UNROLL):
        _lockstep([chunk(c0 + k, state) for k in range(HGRN_UNROLL)])


def _hgrn(pa, pb, lb, ng, batch, seq):
    t = pa.shape[0]
    nh = HALF // HGRN_DIM
    return pl.pallas_call(
        _hgrn_kernel,
        grid=(batch, nh),
        in_specs=[
            pl.BlockSpec((seq, HGRN_DIM), lambda b, h: (b, h)),
            pl.BlockSpec((seq, HGRN_DIM), lambda b, h: (b, nh + h)),
            pl.BlockSpec((seq, HGRN_DIM), lambda b, h: (b, 2 * nh + h)),
            pl.BlockSpec((seq, HGRN_DIM), lambda b, h: (b, h)),
            pl.BlockSpec((1, HGRN_DIM), lambda b, h: (0, h)),
            pl.BlockSpec((1, HGRN_DIM), lambda b, h: (0, 0)),
        ],
        out_specs=pl.BlockSpec((seq, HGRN_DIM), lambda b, h: (b, h)),
        out_shape=jax.ShapeDtypeStruct((t, HALF), BF16),
        compiler_params=_cparams(("parallel", "parallel")),
        name="hgrn2",
    )(pa, pa, pa, pb, lb, ng)


def _fox_gate_kernel(f_ref, bf_ref, c_ref):
    s = f_ref.shape[0]
    ft = f_ref[...].T[0:SUBLANES, :]
    x = ft + bf_ref[...][:, 0:1]
    lf = jnp.minimum(x, 0.0) - jnp.log1p(jnp.exp(-jnp.abs(x)))
    lane = lax.broadcasted_iota(jnp.int32, lf.shape, 1)
    d = 1
    while d < s:
        lf = lf + jnp.where(lane >= d, pltpu.roll(lf, d, 1), 0.0)
        d *= 2
    c_ref[0] = lf


def _fox_gate(pa, col_block, bf_pad, batch, seq):
    return pl.pallas_call(
        _fox_gate_kernel,
        grid=(batch,),
        in_specs=[
            pl.BlockSpec((seq, LANES), lambda b: (b, col_block)),
            pl.BlockSpec((SUBLANES, LANES), lambda b: (0, 0)),
        ],
        out_specs=pl.BlockSpec((1, SUBLANES, seq), lambda b: (b, 0, 0)),
        out_shape=jax.ShapeDtypeStruct((batch, SUBLANES, seq), F32),
        compiler_params=_cparams(("parallel",)),
        name="fox_gate",
    )(pa, bf_pad)


def _head_lane_masks(shape):
    lane = lax.broadcasted_iota(jnp.int32, shape, len(shape) - 1)
    first = lane < HEAD_DIM
    return first, jnp.logical_not(first)


def _causal_tile_order(nq, u):
    assert nq % u == 0
    pairs = [(i, i) for i in range(nq)]
    left = {i: i for i in range(nq)}
    while any(left.values()):
        pick = sorted((i for i in left if left[i]), key=lambda i: -left[i])[:u]
        assert len(pick) == u, "tile list does not split into groups of distinct query blocks"
        for i in pick:
            pairs.append((i, left[i] - 1))
            left[i] -= 1
    return [p[0] for p in pairs], [p[1] for p in pairs]


def _split_heads(q_ref, v_ref, qh_ref, vh_ref):
    scale = HEAD_DIM ** -0.5
    t = ATT_BLOCK
    h0, h1 = _head_lane_masks((t, LANES))
    zero = jnp.zeros((t, LANES), BF16)
    for i in range(q_ref.shape[0] // t):
        q = q_ref[i * t:(i + 1) * t, :] * jnp.asarray(scale, BF16)
        v = v_ref[i * t:(i + 1) * t, :]
        qh_ref[i, 0:t, :] = jnp.where(h0, q, zero)
        qh_ref[i, t:2 * t, :] = jnp.where(h1, q, zero)
        vh_ref[i, 0:t, :] = jnp.where(h0, v, zero)
        vh_ref[i, t:2 * t, :] = jnp.where(h1, v, zero)


def _tile_waves(nq, tile_fn, u):
    qi, kj = _causal_tile_order(nq, u)
    state = [None] * nq
    for w0 in range(0, len(qi), u):
        wave = list(zip(qi[w0:w0 + u], kj[w0:w0 + u]))
        results = _lockstep([tile_fn(i, j, state[i], i == j) for i, j in wave])
        for (i, _), r in zip(wave, results):
            state[i] = r
    return state


def _foxt_kernel(q_ref, k_ref, v_ref, c_ref, o_ref, qt_ref, vt_ref, cb_ref):
    tq = tk = ATT_BLOCK
    seq = q_ref.shape[0]
    nq = seq // tq
    scale = HEAD_DIM ** -0.5
    lane0, lane1 = _head_lane_masks((tq, LANES))
    dim_head0 = lax.broadcasted_iota(jnp.int32, (LANES, tq), 0) < HEAD_DIM

    for i in range(nq):
        q = (q_ref[i * tq:(i + 1) * tq, :] * jnp.asarray(scale, BF16)).astype(F32)
        v = v_ref[i * tq:(i + 1) * tq, :].astype(F32)
        for h, lanes in enumerate((lane0, lane1)):
            qt_ref[i, :, h * tq:(h + 1) * tq] = jnp.where(lanes, q, 0.0).T.astype(BF16)
            vt_ref[i, :, h * tq:(h + 1) * tq] = jnp.where(lanes, v, 0.0).T.astype(BF16)
    for h in range(2):
        cb_ref[h] = jnp.broadcast_to(c_ref[0, 0, h:h + 1, :], (LANES, seq)).T

    def tile(i, j, state, diag):
        keys = slice(j * tk, (j + 1) * tk)
        s = jnp.dot(k_ref[keys, :], qt_ref[i], preferred_element_type=F32)
        yield
        wide = tq // LANES
        s = jnp.concatenate([s[:, 0:tq] - jnp.concatenate([cb_ref[0, keys, :]] * wide, axis=1),
                             s[:, tq:2 * tq] - jnp.concatenate([cb_ref[1, keys, :]] * wide, axis=1)],
                            axis=1)
        if diag:
            kpos = lax.broadcasted_iota(jnp.int32, s.shape, 0)
            qpos = lax.broadcasted_iota(jnp.int32, s.shape, 1) & (tq - 1)
            s = jnp.where(kpos <= qpos, s, -jnp.inf)
        col_max = jnp.max(s, axis=0, keepdims=True)
        if diag:
            m_new = col_max
        else:
            m_old, l_old, acc_old = state
            m_new = jnp.maximum(m_old, col_max)
            alpha = jnp.exp(m_old - m_new)
        p = jnp.exp(s - m_new)
        psum = jnp.sum(p, axis=0, keepdims=True)
        pb = p.astype(BF16)
        yield
        pv = jnp.dot(vt_ref[j], jnp.concatenate([pb[:, 0:tq], pb[:, tq:2 * tq]], axis=0),
                     preferred_element_type=F32)
        yield
        if diag:
            return m_new, psum, pv
        a_rows = jnp.where(dim_head0, alpha[:, 0:tq], alpha[:, tq:2 * tq])
        return m_new, alpha * l_old + psum, acc_old * a_rows + pv

    for i, (_, l, acc) in enumerate(_tile_waves(nq, tile, FOX_WAVE)):
        l_rows = jnp.where(dim_head0, l[:, 0:tq], l[:, tq:2 * tq])
        o_ref[i * tq:(i + 1) * tq, :] = (acc / l_rows).T.astype(o_ref.dtype)


def _foxt(pb, col0, cpair, batch, seq):
    t = pb.shape[0]
    npair = HALF // LANES
    nq = seq // ATT_BLOCK
    return pl.pallas_call(
        _foxt_kernel,
        grid=(batch, npair),
        in_specs=[
            pl.BlockSpec((seq, LANES), lambda b, p: (b, col0 + p)),
            pl.BlockSpec((seq, LANES), lambda b, p: (b, col0 + npair + p)),
            pl.BlockSpec((seq, LANES), lambda b, p: (b, col0 + 2 * npair + p)),
            pl.BlockSpec((1, 1, 2, seq), lambda b, p: (b, p, 0, 0)),
        ],
        out_specs=pl.BlockSpec((seq, LANES), lambda b, p: (b, p)),
        out_shape=jax.ShapeDtypeStruct((t, HALF), BF16),
        scratch_shapes=[
            pltpu.VMEM((nq, LANES, 2 * ATT_BLOCK), BF16),
            pltpu.VMEM((nq, LANES, 2 * ATT_BLOCK), BF16),
            pltpu.VMEM((2, seq, LANES), F32),
        ],
        compiler_params=_cparams(("parallel", "parallel")),
        name="fox_attention",
    )(pb, pb, pb, cpair)


def _suffix_matrix(tk):
    r = lax.broadcasted_iota(jnp.int32, (2 * tk, 2 * tk), 0) % tk
    cidx = lax.broadcasted_iota(jnp.int32, (2 * tk, 2 * tk), 1)
    keep = jnp.logical_or(cidx >= tk, r > cidx)
    return jnp.where(keep, 1.0, 0.0).astype(BF16)


def _sb_kernel(q_ref, k_ref, v_ref, o_ref, qh_ref, vh_ref):
    tq = tk = ATT_BLOCK
    nq = q_ref.shape[0] // tq
    nsub = tk // LANES
    nt_dims = (((1,), (1,)), ((), ()))
    umat = _suffix_matrix(LANES)

    _split_heads(q_ref, v_ref, qh_ref, vh_ref)

    def tile(i, j, state, diag):
        z = lax.dot_general(qh_ref[i], k_ref[j * tk:(j + 1) * tk, :], nt_dims,
                            preferred_element_type=F32)
        yield
        logb = jnp.minimum(z, 0.0) - jnp.log(1.0 + jnp.exp2(jnp.abs(z) * -LOG2_E))
        lom = logb - z
        if diag:
            qpos = lax.broadcasted_iota(jnp.int32, z.shape, 0) & (tq - 1)
            kpos = lax.broadcasted_iota(jnp.int32, z.shape, 1)
            mask = kpos < qpos
            lom = jnp.where(mask, lom, 0.0)
        parts = []
        for c in reversed(range(nsub)):
            part = lom[:, c * LANES:(c + 1) * LANES]
            hi = part.astype(BF16)
            lo = (part - hi.astype(F32)).astype(BF16)
            parts.append(jnp.concatenate([hi, lo], axis=1))
        yield
        r = jnp.dot(jnp.concatenate(parts, axis=0), umat, preferred_element_type=F32)
        yield
        carry = jnp.zeros((2 * tq, LANES), F32) if diag else state[0]
        suffix = [None] * nsub
        for n_c, c in enumerate(reversed(range(nsub))):
            rc = r[n_c * 2 * tq:(n_c + 1) * 2 * tq]
            suffix[c] = rc[:, 0:LANES] + carry
            carry = carry + rc[:, LANES:2 * LANES]
        w = jnp.exp(logb + jnp.concatenate(suffix, axis=1))
        if diag:
            w = jnp.where(mask, w, 0.0)
        wb = w.astype(BF16)
        yield
        wv = jnp.dot(jnp.concatenate([wb[0:tq], wb[tq:2 * tq]], axis=1), vh_ref[j],
                     preferred_element_type=F32)
        yield
        return carry, (wv if diag else state[1] + wv)

    for i, (_, acc) in enumerate(_tile_waves(nq, tile, SB_WAVE)):
        o_ref[i * tq:(i + 1) * tq, :] = acc.astype(o_ref.dtype)


def _sb(pb, col0, batch, seq):
    t = pb.shape[0]
    npair = HALF // LANES
    nq = seq // ATT_BLOCK
    return pl.pallas_call(
        _sb_kernel,
        grid=(batch, npair),
        in_specs=[
            pl.BlockSpec((seq, LANES), lambda b, p: (b, col0 + p)),
            pl.BlockSpec((seq, LANES), lambda b, p: (b, col0 + npair + p)),
            pl.BlockSpec((seq, LANES), lambda b, p: (b, col0 + 2 * npair + p)),
        ],
        out_specs=pl.BlockSpec((seq, LANES), lambda b, p: (b, p)),
        out_shape=jax.ShapeDtypeStruct((t, HALF), BF16),
        scratch_shapes=[
            pltpu.VMEM((nq, 2 * ATT_BLOCK, LANES), BF16),
            pltpu.VMEM((nq, 2 * ATT_BLOCK, LANES), BF16),
        ],
        compiler_params=_cparams(("parallel", "parallel")),
        name="sb_attention",
    )(pb, pb, pb)


def _lru_kernel(x_ref, gate_ref, cw_ref, cb_ref, wg_ref, bg_ref, lam_ref, o_ref,
                xe_ref, a_ref, u_ref, h_ref, y_ref):
    step = pl.program_id(0)
    rows, w = x_ref.shape
    hist = (CONV_WIDTH - 1) * SUBLANES

    @pl.when(step == 0)
    def _():
        xe_ref[0:hist, :] = jnp.zeros((hist, w), F32)
        h_ref[...] = jnp.zeros_like(h_ref)

    @pl.when(step > 0)
    def _():
        xe_ref[0:hist, :] = xe_ref[rows:rows + hist, :]

    x = x_ref[...]
    xe_ref[hist:hist + rows, :] = x
    xc = cb_ref[...] + cw_ref[CONV_WIDTH - 1:CONV_WIDTH, :] * x
    for d in range(1, CONV_WIDTH):
        back = hist - d * SUBLANES
        xc = xc + cw_ref[CONV_WIDTH - 1 - d:CONV_WIDTH - d, :] * xe_ref[back:back + rows, :]

    gates = jnp.dot(xc.astype(BF16), wg_ref[...], preferred_element_type=F32) + bg_ref[...]
    r = jax.nn.sigmoid(gates[:, 0:w])
    ig = jax.nn.sigmoid(gates[:, w:2 * w])
    lam = lam_ref[...]
    softplus_neg_lam = jnp.maximum(-lam, 0.0) + jnp.log1p(jnp.exp(-jnp.abs(lam)))
    log_a = -LRU_C * r * softplus_neg_lam
    a_ref[...] = jnp.exp(log_a)
    th = jnp.tanh(-log_a)
    u_ref[...] = jnp.sqrt(2.0 * th / (1.0 + th)) * (ig * xc)

    h = h_ref[...]
    for t0 in range(0, rows, SUBLANES):
        h = a_ref[t0:t0 + SUBLANES, :] * h + u_ref[t0:t0 + SUBLANES, :]
        u_ref[t0:t0 + SUBLANES, :] = h
    h_ref[...] = h
    y = u_ref[...] * jax.nn.gelu(gate_ref[...], approximate=True)
    nseq, steps, _ = o_ref.shape
    for c in range(w // LANES):
        y_ref[c] = y[:, c * LANES:(c + 1) * LANES]
    for b in range(nseq):
        for c in range(w // LANES):
            o_ref[b, :, c * LANES:(c + 1) * LANES] = (
                y_ref[c, pl.ds(b, steps, stride=nseq), :].astype(o_ref.dtype))


def _lru(pa, cw, cb, wg, bg, lam, batch, seq, rows=512):
    t = pa.shape[0]
    w = HALF
    steps = rows // batch
    fixed = lambda s: (0, 0)
    return pl.pallas_call(
        _lru_kernel,
        grid=(t // rows,),
        in_specs=[
            pl.BlockSpec((rows, w), lambda s: (s, 0)),
            pl.BlockSpec((rows, w), lambda s: (s, 1)),
            pl.BlockSpec((CONV_WIDTH, w), fixed),
            pl.BlockSpec((1, w), fixed),
            pl.BlockSpec((w, 2 * w), fixed),
            pl.BlockSpec((1, 2 * w), fixed),
            pl.BlockSpec((1, w), fixed),
        ],
        out_specs=pl.BlockSpec((batch, steps, w), lambda s: (0, s, 0)),
        out_shape=jax.ShapeDtypeStruct((batch, seq, w), BF16),
        scratch_shapes=[
            pltpu.VMEM((rows + (CONV_WIDTH - 1) * SUBLANES, w), F32),
            pltpu.VMEM((rows, w), F32),
            pltpu.VMEM((rows, w), F32),
            pltpu.VMEM((SUBLANES, w), F32),
            pltpu.VMEM((w // LANES, rows, LANES), F32),
        ],
        compiler_params=_cparams(("arbitrary",)),
        name="rg_lru",
    )(pa, pa, cw, cb, wg, bg, lam).reshape(t, w)


def _block_diag(wb):
    n, bd, _ = wb.shape
    eye = jnp.eye(n, dtype=wb.dtype)
    return (eye[:, None, :, None] * wb[:, :, None, :]).reshape(n * bd, n * bd)


def kernel(x, ev_w_in, ev_fox_bf, hgrn_lb, ev_hgrn_norm_g, ev_w_out, od_w_in, od_conv_w, od_conv_b,
           od_gate_a_w, od_gate_a_b, od_gate_x_w, od_gate_x_b, od_lru_lambda, od_w_out, ffn_w13,
           ffn_w2, ln_g, ln_b):
    batch, seq, d = x.shape
    t = batch * seq
    xf = x.reshape(t, d)
    lb_all = jnp.cumsum(jax.nn.softmax(hgrn_lb.astype(F32), axis=0), axis=0)
    h = HALF

    nfox = ev_fox_bf.shape[1]
    cols_a = [(0, 0, 2 * h), (2 * h, 3 * h, h), (3 * h, 7 * h, nfox)]
    cols_b = [(0, 2 * h, h), (h, 4 * h, 3 * h)]
    pa, pb = _proj(xf, ev_w_in[0], cols_a, cols_b)
    ya = _hgrn(pa, pb, lb_all[0].reshape(1, h), ev_hgrn_norm_g[0].reshape(1, HGRN_DIM), batch, seq)
    bf_pad = jnp.broadcast_to(ev_fox_bf[0].astype(F32).reshape(nfox, 1), (nfox, LANES))
    cum = _fox_gate(pa, 3 * h // LANES, bf_pad, batch, seq)
    yb = _foxt(pb, h // LANES, cum.reshape(batch, nfox // 2, 2, seq), batch, seq)
    lng = ln_g.astype(F32).reshape(2 * DEPTH, 1, d)
    lnb = ln_b.astype(F32).reshape(2 * DEPTH, 1, d)
    w13_bf, w2_bf = ffn_w13.astype(BF16), ffn_w2.astype(BF16)
    x2 = _layer_tail(ya, yb, xf, ev_w_out.astype(BF16), lng, lnb, w13_bf, w2_bf, layer=0, slot=0)

    assert batch == SUBLANES
    pa, pb = _proj(x2, od_w_in[0], [(0, 0, 2 * h)], [(0, 2 * h, 3 * h)], a_time_major=(batch, seq))
    wg = jnp.concatenate([_block_diag(od_gate_a_w[0]), _block_diag(od_gate_x_w[0])], axis=1).astype(BF16)
    bg = jnp.concatenate([od_gate_a_b[0], od_gate_x_b[0]]).reshape(1, 2 * h).astype(F32)
    yc = _lru(pa, od_conv_w[0], od_conv_b[0].reshape(1, h), wg, bg, od_lru_lambda[0].reshape(1, h),
              batch, seq)
    yd = _sb(pb, 0, batch, seq)
    x4 = _layer_tail(yc, yd, x2, od_w_out.astype(BF16), lng, lnb, w13_bf, w2_bf, layer=1, slot=0)
    return x4.reshape(batch, seq, d)
```

```python
import functools

import jax
import jax.numpy as jnp
from jax import lax
from jax.experimental import pallas as pl
from jax.experimental.pallas import tpu as pltpu

F32 = jnp.float32
BF16 = jnp.bfloat16

SUBLANES = 8
LANES = 128

D_MODEL = 1024
HALF = D_MODEL // 2
HEAD_DIM = 64
HGRN_DIM = 128
LRU_BLOCKS = 8
LRU_C = 8.0
CONV_WIDTH = 4
DEPTH = 2
ALPHA = (2 * DEPTH) ** 0.25
EPS = 1e-5
D_FF = 2816
LOG2_E = 1.4426950408889634

FFN_CHUNK = 256
TAIL_ROW_PARTS = 4
HGRN_CHUNK = 128
HGRN_UNROLL = 16
ATT_BLOCK = 256
FOX_WAVE = 4
SB_WAVE = 2
VMEM_LIMIT = 56 * 1024 * 1024


def _cparams(sem):
    return pltpu.CompilerParams(dimension_semantics=sem, vmem_limit_bytes=VMEM_LIMIT)


def _proj_kernel(cols_a, cols_b, x_ref, w_ref, oa_ref, ob_ref, wa_ref, wb_ref, *scratch):
    @pl.when(pl.program_id(0) == 0)
    def _():
        for dst_ref, cols in ((wa_ref, cols_a), (wb_ref, cols_b)):
            for dst, src, n in cols:
                if n % LANES:
                    slot = -(-n // LANES) * LANES
                    dst_ref[:, dst:dst + slot] = jnp.zeros((dst_ref.shape[0], slot), BF16)
                dst_ref[:, dst:dst + n] = w_ref[:, src:src + n].astype(BF16)

    if len(x_ref.shape) == 2:
        xb = x_ref[...].astype(BF16)
        oa_ref[...] = jnp.dot(xb, wa_ref[...], preferred_element_type=F32)
        ob_ref[...] = jnp.dot(xb, wb_ref[...], preferred_element_type=F32).astype(BF16)
    else:
        nseq, steps, d = x_ref.shape
        xt_ref = scratch[0]
        xb = x_ref[...].reshape(nseq * steps, d).astype(BF16)
        ob = jnp.dot(xb, wb_ref[...], preferred_element_type=F32).astype(BF16)
        ob_ref[...] = ob.reshape(ob_ref.shape)
        for b in range(nseq):
            for c in range(d // LANES):
                xt_ref[c, pl.ds(b, steps, stride=nseq), :] = x_ref[b, :, c * LANES:(c + 1) * LANES]
        xt = jnp.concatenate([xt_ref[c] for c in range(d // LANES)], axis=1)
        oa_ref[...] = jnp.dot(xt.astype(BF16), wa_ref[...], preferred_element_type=F32)


def _proj(x, w, cols_a, cols_b, a_time_major=None, tm=512):
    t, d = x.shape

    def width(cols):
        return max(dst + -(-n // LANES) * LANES for dst, _, n in cols)

    na, nb = width(cols_a), width(cols_b)
    row = lambda i: (i, 0)
    scratch = [pltpu.VMEM((d, na), BF16), pltpu.VMEM((d, nb), BF16)]
    if a_time_major is None:
        x_spec, b_spec, b_shape = pl.BlockSpec((tm, d), row), pl.BlockSpec((tm, nb), row), (t, nb)
    else:
        batch, seq = a_time_major
        steps = tm // batch
        x = x.reshape(batch, seq, d)
        x_spec = pl.BlockSpec((batch, steps, d), lambda i: (0, i, 0))
        b_spec, b_shape = pl.BlockSpec((batch, steps, nb), lambda i: (0, i, 0)), (batch, seq, nb)
        scratch.append(pltpu.VMEM((d // LANES, tm, LANES), F32))
    oa, ob = pl.pallas_call(
        functools.partial(_proj_kernel, cols_a, cols_b),
        grid=(t // tm,),
        in_specs=[x_spec, pl.BlockSpec(w.shape, lambda i: (0, 0), pipeline_mode=pl.Buffered(1))],
        out_specs=[pl.BlockSpec((tm, na), row), b_spec],
        out_shape=[jax.ShapeDtypeStruct((t, na), F32), jax.ShapeDtypeStruct(b_shape, BF16)],
        scratch_shapes=scratch,
        compiler_params=_cparams(("arbitrary",)),
        name="in_proj",
    )(x, w)
    return oa, ob.reshape(t, nb)


def _lockstep(gens):
    results = [None] * len(gens)
    live = list(range(len(gens)))
    while live:
        for k in list(live):
            try:
                next(gens[k])
            except StopIteration as done:
                results[k] = done.value
                live.remove(k)
    return results


def _layer_norm_rows(y, g, b):
    mu = jnp.mean(y, axis=-1, keepdims=True)
    yc = y - mu
    var = jnp.mean(yc * yc, axis=-1, keepdims=True)
    return yc * lax.rsqrt(var + EPS) * g + b


def _layer_tail_kernel(ya_ref, yb_ref, x_ref, wo_ref, g1_ref, b1_ref, w13_ref, w2_ref, g2_ref, b2_ref,
                       o_ref, xb_ref, h_ref):
    tf = FFN_CHUNK
    nf = w2_ref.shape[0] // tf
    tm = x_ref.shape[0]
    n_parts = TAIL_ROW_PARTS

    def part(p):
        rows = slice(p * tm // n_parts, (p + 1) * tm // n_parts)
        mix = jnp.dot(ya_ref[rows, :], wo_ref[0:HALF, :], preferred_element_type=F32)
        mix = mix + jnp.dot(yb_ref[rows, :], wo_ref[HALF:D_MODEL, :], preferred_element_type=F32)
        yield
        x1 = _layer_norm_rows(ALPHA * x_ref[rows, :] + mix, g1_ref[...], b1_ref[...])
        o_ref[rows, :] = x1
        xb_ref[rows, :] = x1.astype(BF16)
        for c in range(nf):
            xb = xb_ref[rows, :]
            a = jnp.dot(xb, w13_ref[:, c * tf:(c + 1) * tf], preferred_element_type=F32)
            b = jnp.dot(xb, w13_ref[:, (nf + c) * tf:(nf + c + 1) * tf], preferred_element_type=F32)
            yield
            h_ref[rows, c * tf:(c + 1) * tf] = (a * jax.nn.sigmoid(a) * b).astype(BF16)
        for n0 in range(0, D_MODEL, tf):
            cols = slice(n0, n0 + tf)
            down = jnp.dot(h_ref[rows, :], w2_ref[:, cols], preferred_element_type=F32)
            yield
            o_ref[rows, cols] = ALPHA * o_ref[rows, cols] + down
        o_ref[rows, :] = _layer_norm_rows(o_ref[rows, :], g2_ref[...], b2_ref[...])

    _lockstep([part(p) for p in range(n_parts)])


def _layer_tail(ya, yb, x, w_out, ln_g, ln_b, w13, w2, layer, slot, tm=1024):
    t = x.shape[0]
    nf = D_FF // FFN_CHUNK
    assert nf % 2 == 1 and nf * FFN_CHUNK == D_FF and D_MODEL % FFN_CHUNK == 0
    d = D_MODEL
    resident = pl.Buffered(1)
    row = lambda i: (i, 0)
    return pl.pallas_call(
        _layer_tail_kernel,
        grid=(t // tm,),
        in_specs=[
            pl.BlockSpec((tm, HALF), row),
            pl.BlockSpec((tm, HALF), row),
            pl.BlockSpec((tm, d), row),
            pl.BlockSpec((None, d, d), lambda i: (slot, 0, 0), pipeline_mode=resident),
            pl.BlockSpec((None, 1, d), lambda i: (2 * layer, 0, 0)),
            pl.BlockSpec((None, 1, d), lambda i: (2 * layer, 0, 0)),
            pl.BlockSpec((None, d, 2 * D_FF), lambda i: (layer, 0, 0), pipeline_mode=resident),
            pl.BlockSpec((None, D_FF, d), lambda i: (layer, 0, 0), pipeline_mode=resident),
            pl.BlockSpec((None, 1, d), lambda i: (2 * layer + 1, 0, 0)),
            pl.BlockSpec((None, 1, d), lambda i: (2 * layer + 1, 0, 0)),
        ],
        out_specs=pl.BlockSpec((tm, d), row),
        out_shape=jax.ShapeDtypeStruct((t, d), F32),
        scratch_shapes=[pltpu.VMEM((tm, d), BF16), pltpu.VMEM((tm, D_FF), BF16)],
        compiler_params=_cparams(("parallel",)),
        name="layer_tail",
    )(ya, yb, x, w_out, ln_g, ln_b, w13, w2, ln_g, ln_b)


def _tile_rows(n):
    return lax.broadcasted_iota(jnp.int32, (SUBLANES, n), 0)


def _bcast_row(tile, r):
    return jnp.broadcast_to(tile[r:r + 1, :], tile.shape)


def _segmented_products(x):
    n = len(x)
    row = _tile_rows(x[0].shape[1])
    one = jnp.ones_like(x[0])
    r3 = row & 3
    up4 = (row & 4) != 0

    p = {1: list(x)}
    p[2] = [t * jnp.where((row & 1) == 1, pltpu.roll(t, 1, 0), 1.0) for t in x]
    p[4] = [t * jnp.where(r3 == 2, pltpu.roll(t, 1, 0),
                          jnp.where(r3 == 3, pltpu.roll(t, 2, 0), 1.0)) for t in p[2]]
    p[8] = [t * jnp.where(up4, _bcast_row(t, 3), 1.0) for t in p[4]]

    e = {1: [one] * n}
    e[2] = [jnp.where((row & 1) == 0, pltpu.roll(t, SUBLANES - 1, 0), 1.0) for t in x]
    r2 = [a * b for a, b in zip(e[2], x)]
    e[4] = [t * jnp.where(r3 == 1, pltpu.roll(r, SUBLANES - 1, 0),
                          jnp.where(r3 == 0, pltpu.roll(r, SUBLANES - 2, 0), 1.0))
            for t, r in zip(e[2], r2)]
    r4 = [a * b for a, b in zip(e[4], x)]
    e[8] = [t * jnp.where(up4, 1.0, _bcast_row(r, 4)) for t, r in zip(e[4], r4)]

    m = SUBLANES
    while m < SUBLANES * n:
        nt = m // SUBLANES
        pn, en = [], []
        for g in range(n // (2 * nt)):
            lo = slice(2 * nt * g, 2 * nt * g + nt)
            hi = slice(2 * nt * g + nt, 2 * nt * (g + 1))
            tot = _bcast_row(p[m][lo][-1], SUBLANES - 1)
            pn += p[m][lo] + [t * tot for t in p[m][hi]]
            first = e[m][hi][0][0:1, :] * x[hi][0][0:1, :]
            tot_hi = jnp.broadcast_to(first, one.shape)
            en += [t * tot_hi for t in e[m][lo]] + e[m][hi]
        p[2 * m], e[2 * m] = pn, en
        m *= 2
    return p, e


def _level_map(c):
    t = lax.broadcasted_iota(jnp.int32, (c, c), 0)
    s = lax.broadcasted_iota(jnp.int32, (c, c), 1)
    x = t ^ s
    lv = jnp.full((c, c), -1, jnp.int32)
    m = 1
    while m < c:
        lv = lv + (x >= m).astype(jnp.int32)
        m *= 2
    return jnp.where(t < s, -2, lv)


def _hgrn_kernel(q_ref, f_ref, g_ref, v_ref, lb_ref, ng_ref, o_ref):
    c = HGRN_CHUNK
    n_tiles = c // SUBLANES
    n_chunks = q_ref.shape[0] // c
    lb = lb_ref[...]
    one_m_lb = 1.0 - lb
    ng = ng_ref[...]
    lv = _level_map(c)

    def chunk(ci, state):
        rows = slice(ci * c, (ci + 1) * c)
        z = f_ref[rows, :]
        q = q_ref[rows, :]
        v = v_ref[rows, :]
        gg = g_ref[rows, :]
        ez = jnp.exp(-jnp.abs(z))
        rz = 1.0 / (1.0 + ez)
        pos = z >= 0.0
        sig = jnp.where(pos, rz, ez * rz)
        nsig = jnp.where(pos, ez * rz, rz)
        fg = lb + one_m_lb * sig
        kk = one_m_lb * nsig

        fg_tiles = [fg[SUBLANES * i:SUBLANES * (i + 1), :] for i in range(n_tiles)]
        p, e = _segmented_products(fg_tiles)

        def cat(ts):
            return jnp.concatenate(ts, axis=0)

        nt_dims = (((1,), (1,)), ((), ()))
        qb, kb = q.astype(BF16), kk.astype(BF16)
        yield
        scores = jnp.where(
            lv == -1, lax.dot_general(qb, kb, nt_dims, preferred_element_type=F32), 0.0)
        m, idx = 1, 0
        while m < c:
            qm = (q * cat(p[m])).astype(BF16)
            km = kb if m == 1 else (kk * cat(e[m])).astype(BF16)
            yield
            sm = lax.dot_general(qm, km, nt_dims, preferred_element_type=F32)
            scores = jnp.where(lv == idx, sm, scores)
            m *= 2
            idx += 1

        b = cat(p[c])
        qc = (q * b).astype(BF16)
        kc = (kk * cat(e[c])).astype(BF16)
        decay = b[c - 1:c, :]
        sb = scores.astype(BF16)
        yield
        kv = lax.dot_general(kc, v, (((0,), (0,)), ((), ())), preferred_element_type=F32)
        o = jnp.dot(sb, v, preferred_element_type=F32)
        yield
        st = state[0]
        o = o + jnp.dot(qc, st.astype(BF16), preferred_element_type=F32)
        state[0] = st * jnp.broadcast_to(decay, st.shape).T + kv
        yield
        ms = jnp.mean(o * o, axis=-1, keepdims=True)
        on = o * lax.rsqrt(ms + EPS) * ng
        o_ref[rows, :] = (on * (gg * jax.nn.sigmoid(gg))).astype(o_ref.dtype)

    state = [jnp.zeros((HGRN_DIM, HGRN_DIM), F32)]
    for c0 in range(0, n_chunks, HGRN_UNROLL):
        _lockstep([chunk(c0 + k, state) for k in range(HGRN_UNROLL)])


def _hgrn(pa, pb, lb, ng, batch, seq):
    t = pa.shape[0]
    nh = HALF // HGRN_DIM
    return pl.pallas_call(
        _hgrn_kernel,
        grid=(batch, nh),
        in_specs=[
            pl.BlockSpec((seq, HGRN_DIM), lambda b, h: (b, h)),
            pl.BlockSpec((seq, HGRN_DIM), lambda b, h: (b, nh + h)),
            pl.BlockSpec((seq, HGRN_DIM), lambda b, h: (b, 2 * nh + h)),
            pl.BlockSpec((seq, HGRN_DIM), lambda b, h: (b, h)),
            pl.BlockSpec((1, HGRN_DIM), lambda b, h: (0, h)),
            pl.BlockSpec((1, HGRN_DIM), lambda b, h: (0, 0)),
        ],
        out_specs=pl.BlockSpec((seq, HGRN_DIM), lambda b, h: (b, h)),
        out_shape=jax.ShapeDtypeStruct((t, HALF), BF16),
        compiler_params=_cparams(("parallel", "parallel")),
        name="hgrn2",
    )(pa, pa, pa, pb, lb, ng)


def _fox_gate_kernel(f_ref, bf_ref, c_ref):
    s = f_ref.shape[0]
    ft = f_ref[...].T[0:SUBLANES, :]
    x = ft + bf_ref[...][:, 0:1]
    lf = jnp.minimum(x, 0.0) - jnp.log1p(jnp.exp(-jnp.abs(x)))
    lane = lax.broadcasted_iota(jnp.int32, lf.shape, 1)
    d = 1
    while d < s:
        lf = lf + jnp.where(lane >= d, pltpu.roll(lf, d, 1), 0.0)
        d *= 2
    c_ref[0] = lf


def _fox_gate(pa, col_block, bf_pad, batch, seq):
    return pl.pallas_call(
        _fox_gate_kernel,
        grid=(batch,),
        in_specs=[
            pl.BlockSpec((seq, LANES), lambda b: (b, col_block)),
            pl.BlockSpec((SUBLANES, LANES), lambda b: (0, 0)),
        ],
        out_specs=pl.BlockSpec((1, SUBLANES, seq), lambda b: (b, 0, 0)),
        out_shape=jax.ShapeDtypeStruct((batch, SUBLANES, seq), F32),
        compiler_params=_cparams(("parallel",)),
        name="fox_gate",
    )(pa, bf_pad)


def _head_lane_masks(shape):
    lane = lax.broadcasted_iota(jnp.int32, shape, len(shape) - 1)
    first = lane < HEAD_DIM
    return first, jnp.logical_not(first)


def _causal_tile_order(nq, u):
    assert nq % u == 0
    pairs = [(i, i) for i in range(nq)]
    left = {i: i for i in range(nq)}
    while any(left.values()):
        pick = sorted((i for i in left if left[i]), key=lambda i: -left[i])[:u]
        assert len(pick) == u, "tile list does not split into groups of distinct query blocks"
        for i in pick:
            pairs.append((i, left[i] - 1))
            left[i] -= 1
    return [p[0] for p in pairs], [p[1] for p in pairs]


def _split_heads(q_ref, v_ref, qh_ref, vh_ref):
    scale = HEAD_DIM ** -0.5
    t = ATT_BLOCK
    h0, h1 = _head_lane_masks((t, LANES))
    zero = jnp.zeros((t, LANES), BF16)
    for i in range(q_ref.shape[0] // t):
        q = q_ref[i * t:(i + 1) * t, :] * jnp.asarray(scale, BF16)
        v = v_ref[i * t:(i + 1) * t, :]
        qh_ref[i, 0:t, :] = jnp.where(h0, q, zero)
        qh_ref[i, t:2 * t, :] = jnp.where(h1, q, zero)
        vh_ref[i, 0:t, :] = jnp.where(h0, v, zero)
        vh_ref[i, t:2 * t, :] = jnp.where(h1, v, zero)


def _tile_waves(nq, tile_fn, u):
    qi, kj = _causal_tile_order(nq, u)
    state = [None] * nq
    for w0 in range(0, len(qi), u):
        wave = list(zip(qi[w0:w0 + u], kj[w0:w0 + u]))
        results = _lockstep([tile_fn(i, j, state[i], i == j) for i, j in wave])
        for (i, _), r in zip(wave, results):
            state[i] = r
    return state


def _foxt_kernel(q_ref, k_ref, v_ref, c_ref, o_ref, qt_ref, vt_ref, cb_ref):
    tq = tk = ATT_BLOCK
    seq = q_ref.shape[0]
    nq = seq // tq
    scale = HEAD_DIM ** -0.5
    lane0, lane1 = _head_lane_masks((tq, LANES))
    dim_head0 = lax.broadcasted_iota(jnp.int32, (LANES, tq), 0) < HEAD_DIM

    for i in range(nq):
        q = (q_ref[i * tq:(i + 1) * tq, :] * jnp.asarray(scale, BF16)).astype(F32)
        v = v_ref[i * tq:(i + 1) * tq, :].astype(F32)
        for h, lanes in enumerate((lane0, lane1)):
            qt_ref[i, :, h * tq:(h + 1) * tq] = jnp.where(lanes, q, 0.0).T.astype(BF16)
            vt_ref[i, :, h * tq:(h + 1) * tq] = jnp.where(lanes, v, 0.0).T.astype(BF16)
    for h in range(2):
        cb_ref[h] = jnp.broadcast_to(c_ref[0, 0, h:h + 1, :], (LANES, seq)).T

    def tile(i, j, state, diag):
        keys = slice(j * tk, (j + 1) * tk)
        s = jnp.dot(k_ref[keys, :], qt_ref[i], preferred_element_type=F32)
        yield
        wide = tq // LANES
        s = jnp.concatenate([s[:, 0:tq] - jnp.concatenate([cb_ref[0, keys, :]] * wide, axis=1),
                             s[:, tq:2 * tq] - jnp.concatenate([cb_ref[1, keys, :]] * wide, axis=1)],
                            axis=1)
        if diag:
            kpos = lax.broadcasted_iota(jnp.int32, s.shape, 0)
            qpos = lax.broadcasted_iota(jnp.int32, s.shape, 1) & (tq - 1)
            s = jnp.where(kpos <= qpos, s, -jnp.inf)
        col_max = jnp.max(s, axis=0, keepdims=True)
        if diag:
            m_new = col_max
        else:
            m_old, l_old, acc_old = state
            m_new = jnp.maximum(m_old, col_max)
            alpha = jnp.exp(m_old - m_new)
        p = jnp.exp(s - m_new)
        psum = jnp.sum(p, axis=0, keepdims=True)
        pb = p.astype(BF16)
        yield
        pv = jnp.dot(vt_ref[j], jnp.concatenate([pb[:, 0:tq], pb[:, tq:2 * tq]], axis=0),
                     preferred_element_type=F32)
        yield
        if diag:
            return m_new, psum, pv
        a_rows = jnp.where(dim_head0, alpha[:, 0:tq], alpha[:, tq:2 * tq])
        return m_new, alpha * l_old + psum, acc_old * a_rows + pv

    for i, (_, l, acc) in enumerate(_tile_waves(nq, tile, FOX_WAVE)):
        l_rows = jnp.where(dim_head0, l[:, 0:tq], l[:, tq:2 * tq])
        o_ref[i * tq:(i + 1) * tq, :] = (acc / l_rows).T.astype(o_ref.dtype)


def _foxt(pb, col0, cpair, batch, seq):
    t = pb.shape[0]
    npair = HALF // LANES
    nq = seq // ATT_BLOCK
    return pl.pallas_call(
        _foxt_kernel,
        grid=(batch, npair),
        in_specs=[
            pl.BlockSpec((seq, LANES), lambda b, p: (b, col0 + p)),
            pl.BlockSpec((seq, LANES), lambda b, p: (b, col0 + npair + p)),
            pl.BlockSpec((seq, LANES), lambda b, p: (b, col0 + 2 * npair + p)),
            pl.BlockSpec((1, 1, 2, seq), lambda b, p: (b, p, 0, 0)),
        ],
        out_specs=pl.BlockSpec((seq, LANES), lambda b, p: (b, p)),
        out_shape=jax.ShapeDtypeStruct((t, HALF), BF16),
        scratch_shapes=[
            pltpu.VMEM((nq, LANES, 2 * ATT_BLOCK), BF16),
            pltpu.VMEM((nq, LANES, 2 * ATT_BLOCK), BF16),
            pltpu.VMEM((2, seq, LANES), F32),
        ],
        compiler_params=_cparams(("parallel", "parallel")),
        name="fox_attention",
    )(pb, pb, pb, cpair)


def _suffix_matrix(tk):
    r = lax.broadcasted_iota(jnp.int32, (2 * tk, 2 * tk), 0) % tk
    cidx = lax.broadcasted_iota(jnp.int32, (2 * tk, 2 * tk), 1)
    keep = jnp.logical_or(cidx >= tk, r > cidx)
    return jnp.where(keep, 1.0, 0.0).astype(BF16)


def _sb_kernel(q_ref, k_ref, v_ref, o_ref, qh_ref, vh_ref):
    tq = tk = ATT_BLOCK
    nq = q_ref.shape[0] // tq
    nsub = tk // LANES
    nt_dims = (((1,), (1,)), ((), ()))
    umat = _suffix_matrix(LANES)

    _split_heads(q_ref, v_ref, qh_ref, vh_ref)

    def tile(i, j, state, diag):
        z = lax.dot_general(qh_ref[i], k_ref[j * tk:(j + 1) * tk, :], nt_dims,
                            preferred_element_type=F32)
        yield
        logb = jnp.minimum(z, 0.0) - jnp.log(1.0 + jnp.exp2(jnp.abs(z) * -LOG2_E))
        lom = logb - z
        if diag:
            qpos = lax.broadcasted_iota(jnp.int32, z.shape, 0) & (tq - 1)
            kpos = lax.broadcasted_iota(jnp.int32, z.shape, 1)
            mask = kpos < qpos
            lom = jnp.where(mask, lom, 0.0)
        parts = []
        for c in reversed(range(nsub)):
            part = lom[:, c * LANES:(c + 1) * LANES]
            hi = part.astype(BF16)
            lo = (part - hi.astype(F32)).astype(BF16)
            parts.append(jnp.concatenate([hi, lo], axis=1))
        yield
        r = jnp.dot(jnp.concatenate(parts, axis=0), umat, preferred_element_type=F32)
        yield
        carry = jnp.zeros((2 * tq, LANES), F32) if diag else state[0]
        suffix = [None] * nsub
        for n_c, c in enumerate(reversed(range(nsub))):
            rc = r[n_c * 2 * tq:(n_c + 1) * 2 * tq]
            suffix[c] = rc[:, 0:LANES] + carry
            carry = carry + rc[:, LANES:2 * LANES]
        w = jnp.exp(logb + jnp.concatenate(suffix, axis=1))
        if diag:
            w = jnp.where(mask, w, 0.0)
        wb = w.astype(BF16)
        yield
        wv = jnp.dot(jnp.concatenate([wb[0:tq], wb[tq:2 * tq]], axis=1), vh_ref[j],
                     preferred_element_type=F32)
        yield
        return carry, (wv if diag else state[1] + wv)

    for i, (_, acc) in enumerate(_tile_waves(nq, tile, SB_WAVE)):
        o_ref[i * tq:(i + 1) * tq, :] = acc.astype(o_ref.dtype)


def _sb(pb, col0, batch, seq):
    t = pb.shape[0]
    npair = HALF // LANES
    nq = seq // ATT_BLOCK
    return pl.pallas_call(
        _sb_kernel,
        grid=(batch, npair),
        in_specs=[
            pl.BlockSpec((seq, LANES), lambda b, p: (b, col0 + p)),
            pl.BlockSpec((seq, LANES), lambda b, p: (b, col0 + npair + p)),
            pl.BlockSpec((seq, LANES), lambda b, p: (b, col0 + 2 * npair + p)),
        ],
        out_specs=pl.BlockSpec((seq, LANES), lambda b, p: (b, p)),
        out_shape=jax.ShapeDtypeStruct((t, HALF), BF16),
        scratch_shapes=[
            pltpu.VMEM((nq, 2 * ATT_BLOCK, LANES), BF16),
            pltpu.VMEM((nq, 2 * ATT_BLOCK, LANES), BF16),
        ],
        compiler_params=_cparams(("parallel", "parallel")),
        name="sb_attention",
    )(pb, pb, pb)


def _lru_kernel(x_ref, gate_ref, cw_ref, cb_ref, wg_ref, bg_ref, lam_ref, o_ref,
                xe_ref, a_ref, u_ref, h_ref, y_ref):
    step = pl.program_id(0)
    rows, w = x_ref.shape
    hist = (CONV_WIDTH - 1) * SUBLANES

    @pl.when(step == 0)
    def _():
        xe_ref[0:hist, :] = jnp.zeros((hist, w), F32)
        h_ref[...] = jnp.zeros_like(h_ref)

    @pl.when(step > 0)
    def _():
        xe_ref[0:hist, :] = xe_ref[rows:rows + hist, :]

    x = x_ref[...]
    xe_ref[hist:hist + rows, :] = x
    xc = cb_ref[...] + cw_ref[CONV_WIDTH - 1:CONV_WIDTH, :] * x
    for d in range(1, CONV_WIDTH):
        back = hist - d * SUBLANES
        xc = xc + cw_ref[CONV_WIDTH - 1 - d:CONV_WIDTH - d, :] * xe_ref[back:back + rows, :]

    gates = jnp.dot(xc.astype(BF16), wg_ref[...], preferred_element_type=F32) + bg_ref[...]
    r = jax.nn.sigmoid(gates[:, 0:w])
    ig = jax.nn.sigmoid(gates[:, w:2 * w])
    lam = lam_ref[...]
    softplus_neg_lam = jnp.maximum(-lam, 0.0) + jnp.log1p(jnp.exp(-jnp.abs(lam)))
    log_a = -LRU_C * r * softplus_neg_lam
    a_ref[...] = jnp.exp(log_a)
    th = jnp.tanh(-log_a)
    u_ref[...] = jnp.sqrt(2.0 * th / (1.0 + th)) * (ig * xc)

    h = h_ref[...]
    for t0 in range(0, rows, SUBLANES):
        h = a_ref[t0:t0 + SUBLANES, :] * h + u_ref[t0:t0 + SUBLANES, :]
        u_ref[t0:t0 + SUBLANES, :] = h
    h_ref[...] = h
    y = u_ref[...] * jax.nn.gelu(gate_ref[...], approximate=True)
    nseq, steps, _ = o_ref.shape
    for c in range(w // LANES):
        y_ref[c] = y[:, c * LANES:(c + 1) * LANES]
    for b in range(nseq):
        for c in range(w // LANES):
            o_ref[b, :, c * LANES:(c + 1) * LANES] = (
                y_ref[c, pl.ds(b, steps, stride=nseq), :].astype(o_ref.dtype))


def _lru(pa, cw, cb, wg, bg, lam, batch, seq, rows=1024):
    t = pa.shape[0]
    w = HALF
    steps = rows // batch
    fixed = lambda s: (0, 0)
    return pl.pallas_call(
        _lru_kernel,
        grid=(t // rows,),
        in_specs=[
            pl.BlockSpec((rows, w), lambda s: (s, 0)),
            pl.BlockSpec((rows, w), lambda s: (s, 1)),
            pl.BlockSpec((CONV_WIDTH, w), fixed),
            pl.BlockSpec((1, w), fixed),
            pl.BlockSpec((w, 2 * w), fixed),
            pl.BlockSpec((1, 2 * w), fixed),
            pl.BlockSpec((1, w), fixed),
        ],
        out_specs=pl.BlockSpec((batch, steps, w), lambda s: (0, s, 0)),
        out_shape=jax.ShapeDtypeStruct((batch, seq, w), BF16),
        scratch_shapes=[
            pltpu.VMEM((rows + (CONV_WIDTH - 1) * SUBLANES, w), F32),
            pltpu.VMEM((rows, w), F32),
            pltpu.VMEM((rows, w), F32),
            pltpu.VMEM((SUBLANES, w), F32),
            pltpu.VMEM((w // LANES, rows, LANES), F32),
        ],
        compiler_params=_cparams(("arbitrary",)),
        name="rg_lru",
    )(pa, pa, cw, cb, wg, bg, lam).reshape(t, w)


def _block_diag(wb):
    n, bd, _ = wb.shape
    eye = jnp.eye(n, dtype=wb.dtype)
    return (eye[:, None, :, None] * wb[:, :, None, :]).reshape(n * bd, n * bd)


def kernel(x, ev_w_in, ev_fox_bf, hgrn_lb, ev_hgrn_norm_g, ev_w_out, od_w_in, od_conv_w, od_conv_b,
           od_gate_a_w, od_gate_a_b, od_gate_x_w, od_gate_x_b, od_lru_lambda, od_w_out, ffn_w13,
           ffn_w2, ln_g, ln_b):
    batch, seq, d = x.shape
    t = batch * seq
    xf = x.reshape(t, d)
    lb_all = jnp.cumsum(jax.nn.softmax(hgrn_lb.astype(F32), axis=0), axis=0)
    h = HALF

    nfox = ev_fox_bf.shape[1]
    cols_a = [(0, 0, 2 * h), (2 * h, 3 * h, h), (3 * h, 7 * h, nfox)]
    cols_b = [(0, 2 * h, h), (h, 4 * h, 3 * h)]
    pa, pb = _proj(xf, ev_w_in[0], cols_a, cols_b)
    ya = _hgrn(pa, pb, lb_all[0].reshape(1, h), ev_hgrn_norm_g[0].reshape(1, HGRN_DIM), batch, seq)
    bf_pad = jnp.broadcast_to(ev_fox_bf[0].astype(F32).reshape(nfox, 1), (nfox, LANES))
    cum = _fox_gate(pa, 3 * h // LANES, bf_pad, batch, seq)
    yb = _foxt(pb, h // LANES, cum.reshape(batch, nfox // 2, 2, seq), batch, seq)
    lng = ln_g.astype(F32).reshape(2 * DEPTH, 1, d)
    lnb = ln_b.astype(F32).reshape(2 * DEPTH, 1, d)
    w13_bf, w2_bf = ffn_w13.astype(BF16), ffn_w2.astype(BF16)
    x2 = _layer_tail(ya, yb, xf, ev_w_out.astype(BF16), lng, lnb, w13_bf, w2_bf, layer=0, slot=0)

    assert batch == SUBLANES
    pa, pb = _proj(x2, od_w_in[0], [(0, 0, 2 * h)], [(0, 2 * h, 3 * h)], a_time_major=(batch, seq))
    wg = jnp.concatenate([_block_diag(od_gate_a_w[0]), _block_diag(od_gate_x_w[0])], axis=1).astype(BF16)
    bg = jnp.concatenate([od_gate_a_b[0], od_gate_x_b[0]]).reshape(1, 2 * h).astype(F32)
    yc = _lru(pa, od_conv_w[0], od_conv_b[0].reshape(1, h), wg, bg, od_lru_lambda[0].reshape(1, h),
              batch, seq)
    yd = _sb(pb, 0, batch, seq)
    x4 = _layer_tail(yc, yd, x2, od_w_out.astype(BF16), lng, lnb, w13_bf, w2_bf, layer=1, slot=0)
    return x4.reshape(batch, seq, d)
```
